```python
import math
import jax
import jax.numpy as jnp
from jax import lax
import numpy as np

D_MODEL = 1024
BATCH = 2
SEQ = 8192
DEPTH = 4
DEC_BATCH = 32
DEC_SEQ = 1
PAST_LEN = 8192
PAGE_SIZE = 128

N_EVEN = (DEPTH + 1) // 2
N_ODD = DEPTH // 2
DH_A = 64
H_A = (D_MODEL // 2) // DH_A
H_IDX = 8
D_IDX = 64
TOPK_MAX = 256
QBLK = 128
NUM_BUCKETS = 32
MAX_DISTANCE = 128
H_R = 4
DV_R = (D_MODEL // 2) // H_R
DK_R = DV_R // 2
RET_CHUNK = 128
ROPE_BASE = 10000.0
D_INNER = 2 * D_MODEL
P_SSM = 64
NH_SSM = D_INNER // P_SSM
G_SSM = 8
HPG = NH_SSM // G_SSM
N_STATE = 128
CONV_W = 4
CONV_DIM = D_INNER + 2 * G_SSM * N_STATE
SSD_CHUNK = 128
D_FF = 4 * D_MODEL
EPS = 1e-6

EVEN_SIZES = (H_A * DH_A, H_A * DH_A, H_A * DH_A, H_IDX * D_IDX, D_IDX, H_IDX,
              H_R * DK_R, H_R * DK_R, H_R * DV_R, H_R * DV_R)
EVEN_IN = sum(EVEN_SIZES)
EVEN_OUT = H_A * DH_A + H_R * DV_R
ODD_SIZES = (D_INNER, CONV_DIM, NH_SSM)
ODD_IN = sum(ODD_SIZES)

kernel_name = 'dsa_retention_ssd_hybrid_step'


def _split(x, sizes):
    return jnp.split(x, np.cumsum(sizes)[:-1].tolist(), axis=-1)


def to_blocks(a, size):
    b, l = a.shape[0], a.shape[1]
    return jnp.moveaxis(a.reshape((b, l // size, size) + a.shape[2:]), 1, 0)


def from_blocks(a):
    a = jnp.moveaxis(a, 0, 1)
    return a.reshape((a.shape[0], a.shape[1] * a.shape[2]) + a.shape[3:])


def rmsnorm(x, g):
    x32 = x.astype(jnp.float32)
    y = x32 * lax.rsqrt(jnp.mean(x32 * x32, axis=-1, keepdims=True) + EPS)
    return (y * g.astype(jnp.float32)).astype(x.dtype)


def mlp(h, w_up, w_down):
    return jnp.square(jax.nn.relu(h @ w_up)) @ w_down


def rotary(x, pos):
    half = x.shape[-1] // 2
    inv = ROPE_BASE ** (-jnp.arange(half, dtype=jnp.float32) / half)
    ang = pos.astype(jnp.float32)[:, None] * inv[None, :]
    cos = jnp.cos(ang)[None, :, None, :]
    sin = jnp.sin(ang)[None, :, None, :]
    x32 = x.astype(jnp.float32)
    x1, x2 = x32[..., :half], x32[..., half:]
    return jnp.concatenate([x1 * cos - x2 * sin, x1 * sin + x2 * cos], axis=-1).astype(x.dtype)


def t5_bucket(dist):
    max_exact = NUM_BUCKETS // 2
    d = jnp.maximum(dist, 0)
    ratio = jnp.log(jnp.maximum(d, max_exact).astype(jnp.float32) / max_exact) / math.log(MAX_DISTANCE / max_exact)
    large = max_exact + (ratio * (NUM_BUCKETS - max_exact)).astype(jnp.int32)
    return jnp.where(d < max_exact, d, jnp.minimum(large, NUM_BUCKETS - 1))


def indexer_topk(iq, iw, ik, tpos, k_sel):
    s = jnp.einsum('bqhd,bld->bqhl', iq, ik) * (D_IDX ** -0.5)
    score = jnp.einsum('bqh,bqhl->bql', iw, jax.nn.relu(s)).astype(jnp.float32)
    keypos = jnp.arange(ik.shape[1])
    score = jnp.where(keypos[None, None, :] <= tpos[None, :, None], score, -jnp.inf)
    return lax.top_k(score, k_sel)[1]


def sparse_attend(q, kg, vg, idx, tpos, rel_bias):
    dist = tpos[None, :, None] - idx
    bias = jnp.moveaxis(rel_bias[t5_bucket(dist)], -1, 2).astype(jnp.float32)
    logits = jnp.einsum('bqhd,bqkhd->bqhk', q, kg).astype(jnp.float32) * (DH_A ** -0.5) + bias
    logits = jnp.where((dist >= 0)[:, :, None, :], logits, -jnp.inf)
    p = jax.nn.softmax(logits, axis=-1).astype(vg.dtype)
    return jnp.einsum('bqhk,bqkhd->bqhd', p, vg)


def even_project(h, w_in, pos):
    b, l, _ = h.shape
    q, k, v, iq, ik, iw, rq, rk, rv, rg = _split(h @ w_in, EVEN_SIZES)
    q = q.reshape(b, l, H_A, DH_A)
    k = k.reshape(b, l, H_A, DH_A)
    v = v.reshape(b, l, H_A, DH_A)
    iq = iq.reshape(b, l, H_IDX, D_IDX)
    iw = iw * (H_IDX ** -0.5)
    rq = rotary(rq.reshape(b, l, H_R, DK_R), pos)
    rk = rotary(rk.reshape(b, l, H_R, DK_R), pos) * (DK_R ** -0.5)
    rv = rv.reshape(b, l, H_R, DV_R)
    return q, k, v, iq, ik, iw, rq, rk, rv, rg


def retention_chunk(state, q, k, v, log_gamma):
    l = q.shape[1]
    q, k, v = q.astype(jnp.float32), k.astype(jnp.float32), v.astype(jnp.float32)
    n = jnp.arange(l, dtype=jnp.float32)
    diff = n[:, None] - n[None, :]
    causal = diff >= 0
    decay = jnp.where(causal[None], jnp.exp(jnp.where(causal, diff, 0.0)[None] * log_gamma[:, None, None]), 0.0)
    scores = jnp.einsum('bthd,bshd->bhts', q, k) * decay[None]
    inner = jnp.einsum('bhts,bshe->bthe', scores, v)
    cross = jnp.einsum('bthd,bhde->bthe', q, state) * jnp.exp((n[:, None] + 1.0) * log_gamma[None, :])[None, :, :, None]
    kdec = k * jnp.exp((l - 1.0 - n)[:, None] * log_gamma[None, :])[None, :, :, None]
    new_state = state * jnp.exp(l * log_gamma)[None, :, None, None] + jnp.einsum('bshd,bshe->bhde', kdec, v)
    return new_state, inner + cross


def retention_scan(state, rq, rk, rv, chunk):
    log_gamma = jnp.log(1.0 - jnp.exp2(-5.0 - jnp.arange(H_R, dtype=jnp.float32)))
    def step(st, xs):
        return retention_chunk(st, xs[0], xs[1], xs[2], log_gamma)
    st, o = lax.scan(step, state.astype(jnp.float32),
                     (to_blocks(rq, chunk), to_blocks(rk, chunk), to_blocks(rv, chunk)))
    return st, from_blocks(o)


def merge_even(att, o, rg, w_out):
    b, l = att.shape[0], att.shape[1]
    mu = jnp.mean(o, axis=-1, keepdims=True)
    var = jnp.mean(jnp.square(o - mu), axis=-1, keepdims=True)
    on = ((o - mu) * lax.rsqrt(var + EPS)).reshape(b, l, H_R * DV_R)
    ret = jax.nn.silu(rg) * on.astype(rg.dtype)
    return jnp.concatenate([att.reshape(b, l, H_A * DH_A), ret], axis=-1) @ w_out


def even_mixer_prompt(h, w_in, w_out, rel_bias):
    b, s, _ = h.shape
    q, k, v, iq, ik, iw, rq, rk, rv, rg = even_project(h, w_in, jnp.arange(s))
    k_sel = min(TOPK_MAX, s // 4)
    bi = jnp.arange(b)[:, None, None]

    def attend_block(args):
        qb, iqb, iwb, start = args
        tpos = start + jnp.arange(QBLK)
        idx = indexer_topk(iqb, iwb, ik, tpos, k_sel)
        return sparse_attend(qb, k[bi, idx], v[bi, idx], idx, tpos, rel_bias)

    starts = jnp.arange(s // QBLK) * QBLK
    att = from_blocks(lax.map(attend_block, (to_blocks(q, QBLK), to_blocks(iq, QBLK), to_blocks(iw, QBLK), starts)))
    st0 = jnp.zeros((b, H_R, DK_R, DV_R), jnp.float32)
    st, o = retention_scan(st0, rq, rk, rv, RET_CHUNK)
    return merge_even(att, o, rg, w_out), k, v, ik, st


def even_mixer_sample(h, ck, cv, cik, page_table, ret_state, w_in, w_out, rel_bias):
    b, l, _ = h.shape
    past = page_table.shape[1] * PAGE_SIZE
    pos = past + jnp.arange(l)
    q, k, v, iq, ik, iw, rq, rk, rv, rg = even_project(h, w_in, pos)
    ik_past = cik[page_table].reshape(b, past, D_IDX)
    ik_all = jnp.concatenate([ik_past.astype(ik.dtype), ik], axis=1)
    k_sel = min(TOPK_MAX, (past + l) // 4)
    idx = indexer_topk(iq, iw, ik_all, pos, k_sel)
    bi = jnp.arange(b)[:, None, None]
    in_past = (idx < past)[..., None, None]
    pidx = jnp.minimum(idx, past - 1)
    page = page_table[bi, pidx // PAGE_SIZE]
    off = pidx % PAGE_SIZE
    nidx = jnp.clip(idx - past, 0, l - 1)
    kg = jnp.where(in_past, ck[page, off], k[bi, nidx])
    vg = jnp.where(in_past, cv[page, off], v[bi, nidx])
    att = sparse_attend(q, kg, vg, idx, pos, rel_bias)
    st, o = retention_scan(ret_state, rq, rk, rv, l)
    return merge_even(att, o, rg, w_out), k, v, ik, st


def causal_dwconv(xpad, w, bias):
    l = xpad.shape[1] - (CONV_W - 1)
    out = bias
    for j in range(CONV_W):
        out = out + xpad[:, j:j + l] * w[j]
    return out


def ssd_chunk(state, x, dt, bm, cm, a):
    l = x.shape[1]
    x, bm, cm = x.astype(jnp.float32), bm.astype(jnp.float32), cm.astype(jnp.float32)
    acs = jnp.cumsum(dt * a, axis=1)
    causal = jnp.tril(jnp.ones((l, l), dtype=bool))
    seg = acs[:, :, None] - acs[:, None, :]
    lmat = jnp.exp(jnp.where(causal[None, :, :, None, None], seg, -jnp.inf))
    xdt = x * dt[..., None]
    cb = jnp.einsum('btgn,bsgn->btsg', cm, bm)
    y_diag = jnp.einsum('btsg,btsgh,bsghp->btghp', cb, lmat, xdt)
    y_off = jnp.einsum('btgn,bghpn->btghp', cm, state) * jnp.exp(acs)[..., None]
    dec_end = jnp.exp(acs[:, -1:] - acs)
    new_state = state * jnp.exp(acs[:, -1])[..., None, None] + jnp.einsum('bsgn,bsgh,bsghp->bghpn', bm, dec_end, xdt)
    return new_state, y_diag + y_off


def mamba_mixer(h, hist, ssm_state, w_in, conv_w, conv_b, dt_bias, a_log, d_skip, norm_w, w_out, chunk):
    b, l, _ = h.shape
    z, xbc, dt = _split(h @ w_in, ODD_SIZES)
    xpad = jnp.concatenate([hist.astype(xbc.dtype), xbc], axis=1)
    new_hist = xpad[:, l:]
    xbc = jax.nn.silu(causal_dwconv(xpad, conv_w, conv_b))
    xs, bm, cm = _split(xbc, (D_INNER, G_SSM * N_STATE, G_SSM * N_STATE))
    xs = xs.reshape(b, l, G_SSM, HPG, P_SSM)
    bm = bm.reshape(b, l, G_SSM, N_STATE)
    cm = cm.reshape(b, l, G_SSM, N_STATE)
    dt = jax.nn.softplus((dt + dt_bias).astype(jnp.float32)).reshape(b, l, G_SSM, HPG)
    a = -jnp.exp(a_log.astype(jnp.float32)).reshape(G_SSM, HPG)

    def step(st, inp):
        return ssd_chunk(st, inp[0], inp[1], inp[2], inp[3], a)

    st0 = ssm_state.astype(jnp.float32).reshape(b, G_SSM, HPG, P_SSM, N_STATE)
    st, y = lax.scan(step, st0, (to_blocks(xs, chunk), to_blocks(dt, chunk), to_blocks(bm, chunk), to_blocks(cm, chunk)))
    y = from_blocks(y) + d_skip.astype(jnp.float32).reshape(G_SSM, HPG)[:, :, None] * xs.astype(jnp.float32)
    g = (y.reshape(b, l, D_INNER) * jax.nn.silu(z.astype(jnp.float32))).reshape(b, l, G_SSM, D_INNER // G_SSM)
    g = g * lax.rsqrt(jnp.mean(g * g, axis=-1, keepdims=True) + EPS)
    out = (g.reshape(b, l, D_INNER) * norm_w.astype(jnp.float32)).astype(h.dtype) @ w_out
    return out, new_hist, st.reshape(b, NH_SSM, P_SSM, N_STATE)


def setup_inputs(seed: int = 0) -> dict:
    key = jax.random.key(seed)
    ks = jax.random.split(key, 32)
    f32 = jnp.float32

    def nrm(k, shape, scale):
        return jax.random.normal(k, shape, f32) * scale

    n_pages = PAST_LEN // PAGE_SIZE
    n_used = DEC_BATCH * n_pages
    n_pool = n_used + max(1, n_used // 4)
    page_table = jax.random.permutation(ks[0], n_pool)[:n_used].reshape(DEC_BATCH, n_pages).astype(jnp.int32)
    dt0 = jnp.exp(jax.random.uniform(ks[1], (N_ODD, NH_SSM), f32, math.log(1e-3), math.log(1e-1)))
    return {
        'x_prompt': nrm(ks[2], (BATCH, SEQ, D_MODEL), 1.0),
        'x_sample': nrm(ks[3], (DEC_BATCH, DEC_SEQ, D_MODEL), 1.0),
        'cache_k': nrm(ks[4], (N_EVEN, n_pool, PAGE_SIZE, H_A, DH_A), 1.0),
        'cache_v': nrm(ks[5], (N_EVEN, n_pool, PAGE_SIZE, H_A, DH_A), 1.0),
        'cache_idx_k': nrm(ks[6], (N_EVEN, n_pool, PAGE_SIZE, D_IDX), 1.0),
        'state_ret': nrm(ks[7], (N_EVEN, DEC_BATCH, H_R, DK_R, DV_R), 0.5),
        'state_conv': nrm(ks[8], (N_ODD, DEC_BATCH, CONV_W - 1, CONV_DIM), 1.0),
        'state_ssm': nrm(ks[9], (N_ODD, DEC_BATCH, NH_SSM, P_SSM, N_STATE), 0.5),
        'page_table': page_table,
        'rel_bias': nrm(ks[10], (NUM_BUCKETS, H_A), 0.5),
        'norm_mix': 1.0 + nrm(ks[11], (DEPTH, D_MODEL), 0.1),
        'norm_mlp': 1.0 + nrm(ks[12], (DEPTH, D_MODEL), 0.1),
        'norm_final': 1.0 + nrm(ks[13], (D_MODEL,), 0.1),
        'w_in_even': nrm(ks[14], (N_EVEN, D_MODEL, EVEN_IN), D_MODEL ** -0.5),
        'w_out_even': nrm(ks[15], (N_EVEN, EVEN_OUT, D_MODEL), EVEN_OUT ** -0.5),
        'w_in_odd': nrm(ks[16], (N_ODD, D_MODEL, ODD_IN), D_MODEL ** -0.5),
        'conv_w': nrm(ks[17], (N_ODD, CONV_W, CONV_DIM), CONV_W ** -0.5),
        'conv_b': nrm(ks[18], (N_ODD, CONV_DIM), 0.01),
        'dt_bias': dt0 + jnp.log(-jnp.expm1(-dt0)),
        'a_log': jnp.log(jax.random.uniform(ks[19], (N_ODD, NH_SSM), f32, 1.0, 16.0)),
        'd_skip': 1.0 + nrm(ks[20], (N_ODD, NH_SSM), 0.1),
        'ssm_norm': 1.0 + nrm(ks[21], (N_ODD, D_INNER), 0.1),
        'w_out_odd': nrm(ks[22], (N_ODD, D_INNER, D_MODEL), D_INNER ** -0.5),
        'w_up': nrm(ks[23], (DEPTH, D_MODEL, D_FF), D_MODEL ** -0.5),
        'w_down': nrm(ks[24], (DEPTH, D_FF, D_MODEL), D_FF ** -0.5),
    }


def reference(x_prompt, x_sample, cache_k, cache_v, cache_idx_k, state_ret, state_conv, state_ssm, page_table,
              rel_bias, norm_mix, norm_mlp, norm_final, w_in_even, w_out_even, w_in_odd, conv_w, conv_b,
              dt_bias, a_log, d_skip, ssm_norm, w_out_odd, w_up, w_down):
    xp, xs = x_prompt, x_sample
    bp = xp.shape[0]
    k_p, v_p, ik_p, k_s, v_s, ik_s, ret_p, ret_s = [], [], [], [], [], [], [], []
    conv_p, conv_s, ssm_p, ssm_s = [], [], [], []
    for layer in range(DEPTH):
        hp = rmsnorm(xp, norm_mix[layer])
        hs = rmsnorm(xs, norm_mix[layer])
        if layer % 2 == 0:
            e = layer // 2
            yp, kk, vv, ik, st = even_mixer_prompt(hp, w_in_even[e], w_out_even[e], rel_bias)
            k_p.append(kk); v_p.append(vv); ik_p.append(ik); ret_p.append(st)
            ys, kk, vv, ik, st = even_mixer_sample(hs, cache_k[e], cache_v[e], cache_idx_k[e], page_table,
                                                   state_ret[e], w_in_even[e], w_out_even[e], rel_bias)
            k_s.append(kk); v_s.append(vv); ik_s.append(ik); ret_s.append(st)
        else:
            o = layer // 2
            hist0 = jnp.zeros((bp, CONV_W - 1, CONV_DIM), hp.dtype)
            st0 = jnp.zeros((bp, NH_SSM, P_SSM, N_STATE), jnp.float32)
            yp, hc, st = mamba_mixer(hp, hist0, st0, w_in_odd[o], conv_w[o], conv_b[o], dt_bias[o], a_log[o],
                                     d_skip[o], ssm_norm[o], w_out_odd[o], SSD_CHUNK)
            conv_p.append(hc); ssm_p.append(st)
            ys, hc, st = mamba_mixer(hs, state_conv[o], state_ssm[o], w_in_odd[o], conv_w[o], conv_b[o],
                                     dt_bias[o], a_log[o], d_skip[o], ssm_norm[o], w_out_odd[o], xs.shape[1])
            conv_s.append(hc); ssm_s.append(st)
        xp = xp + yp
        xs = xs + ys
        xp = xp + mlp(rmsnorm(xp, norm_mlp[layer]), w_up[layer], w_down[layer])
        xs = xs + mlp(rmsnorm(xs, norm_mlp[layer]), w_up[layer], w_down[layer])
    y_prompt = rmsnorm(xp, norm_final)
    y_sample = rmsnorm(xs, norm_final)
    return (y_prompt, y_sample, jnp.stack(k_p), jnp.stack(v_p), jnp.stack(ik_p), jnp.stack(k_s), jnp.stack(v_s),
            jnp.stack(ik_s), jnp.stack(ret_p), jnp.stack(ret_s), jnp.stack(conv_p), jnp.stack(conv_s),
            jnp.stack(ssm_p), jnp.stack(ssm_s))
```

```python
import functools
import math

import jax
import jax.numpy as jnp
import numpy as np
from jax import lax
from jax.experimental import pallas as pl
from jax.experimental.pallas import tpu as pltpu

F32 = jnp.float32
BF16 = jnp.bfloat16
I32 = jnp.int32

D_MODEL = 1024
DEPTH = 4
PAGE_SIZE = 128
DH_A = 64
H_A = (D_MODEL // 2) // DH_A
D_ATT = H_A * DH_A
H_IDX = 8
D_IDX = 64
TOPK_MAX = 256
NUM_BUCKETS = 32
MAX_DISTANCE = 128
H_R = 4
DV_R = (D_MODEL // 2) // H_R
DK_R = DV_R // 2
D_RK = H_R * DK_R
D_RV = H_R * DV_R
ROPE_BASE = 10000.0
D_INNER = 2 * D_MODEL
P_SSM = 64
NH_SSM = D_INNER // P_SSM
G_SSM = 8
HPG = NH_SSM // G_SSM
N_STATE = 128
CONV_W = 4
D_BC = G_SSM * N_STATE
CONV_DIM = D_INNER + 2 * D_BC
D_FF = 4 * D_MODEL
EPS = 1e-6

EVEN_SIZES = (D_ATT, D_ATT, D_ATT, H_IDX * D_IDX, D_IDX, H_IDX, D_RK, D_RK, D_RV, D_RV)
ODD_IN = D_INNER + CONV_DIM + NH_SSM

E_Q, E_K, E_V, E_IQ = 0, 512, 1024, 1536
E_RV, E_RG, E_RQ, E_RK, E_IK = 2048, 2560, 3072, 3328, 3584
E_IW = D_IDX
EVEN_PAD = 3712
O_Z, O_XS, O_BC, O_DT = 0, 2048, 4096, 6144
ODD_PAD = 6272

LANE = 128
VMEM_LIMIT = 56 * 1024 * 1024
NEG_BIG = -1e30
INT_MIN = -(2 ** 31)
KEY_NEG_INF = int(np.array(0xFF800000, np.uint32).view(np.int32)) ^ 0x7FFFFFFF

NT_DIMS = (((1,), (1,)), ((), ()))
TN_DIMS = (((0,), (0,)), ((), ()))


def _cparams(*sem):
    return pltpu.CompilerParams(dimension_semantics=sem, vmem_limit_bytes=VMEM_LIMIT)


def _const_spec(shape):
    nd = len(shape)
    return pl.BlockSpec(shape, lambda *_: (0,) * nd, pipeline_mode=pl.Buffered(1))


def _rms(x, g):
    return x * lax.rsqrt(jnp.mean(x * x, axis=-1, keepdims=True) + EPS) * g


def _silu(x):
    return x / (1.0 + jnp.exp(-x))


def _softplus(x):
    return jnp.maximum(x, 0.0) + jnp.log(1.0 + jnp.exp(-jnp.abs(x)))


def _rms_matmul_kernel(x_ref, g_ref, w_ref, of_ref, *ob_refs, bf_slices, n_chunk):
    h = _rms(x_ref[...], g_ref[...]).astype(BF16)
    n = w_ref.shape[1]
    for c0 in range(0, n, n_chunk):
        c1 = min(c0 + n_chunk, n)
        of_ref[:, c0:c1] = jnp.dot(h, w_ref[:, c0:c1], preferred_element_type=F32)
    for (start, width), r in zip(bf_slices, ob_refs):
        r[...] = of_ref[:, start:start + width].astype(BF16)


def _rms_matmul(x, g, w, tm, bf_slices=()):
    t, d = x.shape
    n = w.shape[1]
    out_shape = [jax.ShapeDtypeStruct((t, n), F32)]
    out_specs = [pl.BlockSpec((tm, n), lambda i: (i, 0))]
    for _, width in bf_slices:
        out_shape.append(jax.ShapeDtypeStruct((t, width), BF16))
        out_specs.append(pl.BlockSpec((tm, width), lambda i: (i, 0)))
    return pl.pallas_call(
        functools.partial(_rms_matmul_kernel, bf_slices=tuple(bf_slices), n_chunk=512),
        grid=(t // tm,),
        in_specs=[pl.BlockSpec((tm, d), lambda i: (i, 0)), _const_spec((1, d)), _const_spec((d, n))],
        out_specs=out_specs,
        out_shape=out_shape,
        compiler_params=_cparams("parallel"),
        name="rms_matmul",
    )(x, g.reshape(1, d), w)


def _mlp_tail(x, gm_ref, wup_ref, wdn_ref, f_chunk=1024):
    h = _rms(x, gm_ref[...]).astype(BF16)
    acc = x
    for f in range(0, D_FF, f_chunk):
        u = jnp.dot(h, wup_ref[:, f:f + f_chunk], preferred_element_type=F32)
        u = jnp.maximum(u, 0.0)
        u = (u * u).astype(BF16)
        acc = acc + jnp.dot(u, wdn_ref[f:f + f_chunk, :], preferred_element_type=F32)
    return acc


def _even_merge_kernel(x_ref, att_ref, o_ref, rg_ref, wo_ref, out_ref):
    o = o_ref[...]
    parts = []
    for h in range(H_R):
        oh = o[:, h * DV_R:(h + 1) * DV_R]
        d = oh - jnp.mean(oh, axis=-1, keepdims=True)
        parts.append(d * lax.rsqrt(jnp.mean(d * d, axis=-1, keepdims=True) + EPS))
    ret = _silu(rg_ref[...]) * jnp.concatenate(parts, axis=1)
    x = x_ref[...]
    x = x + jnp.dot(att_ref[...].astype(BF16), wo_ref[0:D_ATT, :], preferred_element_type=F32)
    out_ref[...] = x + jnp.dot(ret.astype(BF16), wo_ref[D_ATT:D_ATT + D_RV, :], preferred_element_type=F32)


def _odd_merge_kernel(x_ref, y_ref, z_ref, nw_ref, wo_ref, out_ref):
    g = y_ref[...] * _silu(z_ref[...])
    gw = D_INNER // G_SSM
    parts = []
    for k in range(G_SSM):
        gk = g[:, k * gw:(k + 1) * gw]
        parts.append(gk * lax.rsqrt(jnp.mean(gk * gk, axis=-1, keepdims=True) + EPS))
    gn = (jnp.concatenate(parts, axis=1) * nw_ref[...]).astype(BF16)
    out_ref[...] = x_ref[...] + jnp.dot(gn, wo_ref[...], preferred_element_type=F32)


def _mlp_kernel(x_ref, gm_ref, wup_ref, wdn_ref, gf_ref, out_ref, *, final):
    x = _mlp_tail(x_ref[...], gm_ref, wup_ref, wdn_ref)
    if final:
        x = _rms(x, gf_ref[...])
    out_ref[...] = x


def _mlp(x, gm, wup, wdn, gf, tm, final):
    t, d = x.shape
    row = lambda i: (i, 0)
    return pl.pallas_call(
        functools.partial(_mlp_kernel, final=final),
        grid=(t // tm,),
        in_specs=[pl.BlockSpec((tm, d), row), _const_spec((1, d)), _const_spec(wup.shape), _const_spec(wdn.shape),
                  _const_spec((1, d))],
        out_specs=pl.BlockSpec((tm, d), row),
        out_shape=jax.ShapeDtypeStruct((t, d), F32),
        compiler_params=_cparams("parallel"),
        name="mlp",
    )(x, gm.reshape(1, d), wup, wdn, gf.reshape(1, d))


def _even_merge(x, att, o, pf, wo, tm):
    t, d = x.shape
    row = lambda i: (i, 0)
    return pl.pallas_call(
        _even_merge_kernel,
        grid=(t // tm,),
        in_specs=[pl.BlockSpec((tm, d), row), pl.BlockSpec((tm, D_ATT), row), pl.BlockSpec((tm, D_RV), row),
                  pl.BlockSpec((tm, D_RV), lambda i: (i, E_RG // D_RV)), _const_spec(wo.shape)],
        out_specs=pl.BlockSpec((tm, d), row),
        out_shape=jax.ShapeDtypeStruct((t, d), F32),
        compiler_params=_cparams("parallel"),
        name="even_merge",
    )(x, att, o, pf, wo)


def _odd_merge(x, y, pf, nw, wo, tm):
    t, d = x.shape
    row = lambda i: (i, 0)
    return pl.pallas_call(
        _odd_merge_kernel,
        grid=(t // tm,),
        in_specs=[pl.BlockSpec((tm, d), row), pl.BlockSpec((tm, D_INNER), row),
                  pl.BlockSpec((tm, D_INNER), lambda i: (i, O_Z // D_INNER)),
                  _const_spec((1, D_INNER)), _const_spec(wo.shape)],
        out_specs=pl.BlockSpec((tm, d), row),
        out_shape=jax.ShapeDtypeStruct((t, d), F32),
        compiler_params=_cparams("parallel"),
        name="odd_merge",
    )(x, y, pf, nw.reshape(1, D_INNER), wo)


def _score_key(score):
    bits = lax.bitcast_convert_type(score, I32)
    return bits ^ ((bits >> 31) & 0x7FFFFFFF)


def _kth_largest(count_ge, rows, k_sel):
    def bit_step(b, t):
        cand = t + lax.shift_left(jnp.int32(1), 31 - b)
        return jnp.where(count_ge(cand) >= k_sel, cand, t)
    return lax.fori_loop(0, 32, bit_step, jnp.full((rows, 1), INT_MIN, I32))


def _tie_limit(count_tie_lt, need, rows, n_bits):
    def bit_step(b, x):
        cand = x + lax.shift_left(jnp.int32(1), n_bits - 1 - b)
        return jnp.where(count_tie_lt(cand) < need, cand, x)
    return lax.fori_loop(0, n_bits, bit_step, jnp.zeros((rows, 1), I32))


def _attn_prompt_kernel(q_ref, iq_ref, iw_ref, k_ref, v_ref, ik_ref, tb_ref, b31_ref, o_ref,
                        sk_ref, x_ref, m_ref, l_ref, acc_ref, *, tq, tk, k_sel, idx_bits):
    i = pl.program_id(1)
    r = tk // tq
    nkc = (i + r) // r
    nfar = jnp.maximum(i - 1, 0) // r
    row = lax.broadcasted_iota(I32, (tq, tk), 0) + i * tq
    col0 = lax.broadcasted_iota(I32, (tq, tk), 1)
    iw = iw_ref[:, E_IW:E_IW + H_IDX] * ((D_IDX ** -0.5) * (H_IDX ** -0.5))

    def score_chunk(c, carry):
        ikc = ik_ref[c][:, 0:D_IDX]
        acc = jnp.zeros((tq, tk), F32)
        for h in range(H_IDX):
            s = lax.dot_general(iq_ref[:, h * D_IDX:(h + 1) * D_IDX], ikc, NT_DIMS, preferred_element_type=F32)
            acc = acc + iw[:, h:h + 1] * jnp.maximum(s, 0.0)
        kidx = col0 + c * tk
        sk_ref[c] = _score_key(jnp.where(kidx <= row, acc, -jnp.inf))
        return carry
    lax.fori_loop(0, nkc, score_chunk, 0)

    def lane_fold(m):
        part = m[:, 0:LANE]
        for t in range(1, tk // LANE):
            part = part + m[:, t * LANE:(t + 1) * LANE]
        return part

    def count_ge(cand):
        def body(c, part):
            return part + lane_fold(jnp.where(sk_ref[c] >= cand, 1.0, 0.0))
        part = lax.fori_loop(0, nkc, body, jnp.zeros((tq, LANE), F32))
        return jnp.sum(part, axis=1, keepdims=True)

    thr = _kth_largest(count_ge, tq, float(k_sel))
    n_gt = count_ge(thr + 1)
    n_ge = count_ge(thr)
    need = float(k_sel) - n_gt
    excess = jnp.logical_and(n_ge - n_gt > need, thr > KEY_NEG_INF)
    x_ref[...] = jnp.full((tq, LANE), 2 ** idx_bits, I32)

    @pl.when(jnp.max(jnp.where(excess, 1.0, 0.0)) > 0.0)
    def _():
        def count_tie_lt(cand):
            def body(c, part):
                hit = jnp.logical_and(sk_ref[c] == thr, col0 + c * tk < cand)
                return part + lane_fold(jnp.where(hit, 1.0, 0.0))
            part = lax.fori_loop(0, nkc, body, jnp.zeros((tq, LANE), F32))
            return jnp.sum(part, axis=1, keepdims=True)
        lim = _tie_limit(count_tie_lt, need, tq, idx_bits)
        x_ref[...] = jnp.broadcast_to(jnp.where(excess, lim, 2 ** idx_bits), (tq, LANE))

    xlim = x_ref[:, 0:1]

    m_ref[...] = jnp.full(m_ref.shape, NEG_BIG, F32)
    l_ref[...] = jnp.zeros(l_ref.shape, F32)
    acc_ref[...] = jnp.zeros(acc_ref.shape, F32)

    def attend_chunk(c, near):
        key = sk_ref[c]
        kidx = col0 + c * tk
        sel = jnp.logical_or(key > thr, jnp.logical_and(key == thr, kidx <= xlim))
        if near:
            sel = jnp.logical_and(sel, kidx <= row)
        kc = k_ref[c]
        vc = v_ref[c]
        for h in range(H_A):
            hs = slice(h * DH_A, (h + 1) * DH_A)
            s = lax.dot_general(q_ref[:, hs], kc[:, hs], NT_DIMS, preferred_element_type=F32) * (DH_A ** -0.5)
            b_far = b31_ref[0:1, h:h + 1]
            if near:
                tiles = []
                for t in range(r):
                    jt = c * r + t
                    tiles.append(jnp.where(jt == i, tb_ref[h, :, tq:2 * tq],
                                           jnp.where(jt == i - 1, tb_ref[h, :, 0:tq], b_far)))
                s = s + (tiles[0] if r == 1 else jnp.concatenate(tiles, axis=1))
            else:
                s = s + b_far
            s = jnp.where(sel, s, NEG_BIG)
            m_old = m_ref[h][:, 0:1]
            m_new = jnp.maximum(m_old, jnp.max(s, axis=1, keepdims=True))
            alpha = jnp.exp(m_old - m_new)
            p = jnp.exp(s - m_new)
            l_new = alpha * l_ref[h][:, 0:1] + jnp.sum(p, axis=1, keepdims=True)
            pv = jnp.dot(p.astype(BF16), vc[:, hs], preferred_element_type=F32)
            acc_ref[:, hs] = alpha * acc_ref[:, hs] + pv
            m_ref[h] = jnp.broadcast_to(m_new, (tq, LANE))
            l_ref[h] = jnp.broadcast_to(l_new, (tq, LANE))

    def far_body(c, carry):
        attend_chunk(c, False)
        return carry

    def near_body(c, carry):
        attend_chunk(c, True)
        return carry

    lax.fori_loop(0, nfar, far_body, 0)
    lax.fori_loop(nfar, nkc, near_body, 0)

    for h in range(H_A):
        hs = slice(h * DH_A, (h + 1) * DH_A)
        o_ref[:, hs] = acc_ref[:, hs] / l_ref[h][:, 0:1]


def _attn_prompt(pf, pbq, pbi, tb, b31, b, s, tq, tk):
    nq, nkc = s // tq, s // tk
    k_sel = min(TOPK_MAX, s // 4)
    idx_bits = int(math.log2(s))
    assert 2 ** idx_bits == s and tk % tq == 0 and tq % LANE == 0
    pbq4 = pbq.reshape(b, nkc, tk, pbq.shape[1])
    pbi4 = pbi.reshape(b, nkc, tk, LANE)
    return pl.pallas_call(
        functools.partial(_attn_prompt_kernel, tq=tq, tk=tk, k_sel=k_sel, idx_bits=idx_bits),
        grid=(b, nq),
        in_specs=[
            pl.BlockSpec((tq, D_ATT), lambda bb, i: (bb * nq + i, E_Q // D_ATT)),
            pl.BlockSpec((tq, D_ATT), lambda bb, i: (bb * nq + i, E_IQ // D_ATT)),
            pl.BlockSpec((tq, LANE), lambda bb, i: (bb * nq + i, E_IK // LANE)),
            pl.BlockSpec((None, nkc, tk, D_ATT), lambda bb, i: (bb, 0, 0, E_K // D_ATT), pipeline_mode=pl.Buffered(1)),
            pl.BlockSpec((None, nkc, tk, D_ATT), lambda bb, i: (bb, 0, 0, E_V // D_ATT), pipeline_mode=pl.Buffered(1)),
            pl.BlockSpec((None, nkc, tk, LANE), lambda bb, i: (bb, 0, 0, 0), pipeline_mode=pl.Buffered(1)),
            _const_spec(tb.shape), _const_spec(b31.shape),
        ],
        out_specs=pl.BlockSpec((tq, D_ATT), lambda bb, i: (bb * nq + i, 0)),
        out_shape=jax.ShapeDtypeStruct((b * s, D_ATT), F32),
        scratch_shapes=[pltpu.VMEM((nkc, tq, tk), I32), pltpu.VMEM((tq, LANE), I32),
                        pltpu.VMEM((H_A, tq, LANE), F32), pltpu.VMEM((H_A, tq, LANE), F32),
                        pltpu.VMEM((tq, D_ATT), F32)],
        compiler_params=_cparams("parallel", "arbitrary"),
        name="attn_prompt",
    )(pbq, pbq, pf, pbq4, pbq4, pbi4, tb, b31)


def _rotate_half(x, cos_t, sin_t):
    half = DK_R // 2
    width = x.shape[1]
    lane = lax.broadcasted_iota(I32, x.shape, 1)
    partner = jnp.where(lane % DK_R < half, pltpu.roll(x, width - half, 1), pltpu.roll(x, half, 1))
    return x * cos_t + partner * sin_t


def _ret_prompt_kernel(rq_ref, rk_ref, rv_ref, cos_ref, sin_ref, dm_ref, cd_ref, kd_ref, gl_ref,
                       o_ref, st_ref, s_ref):
    c = pl.program_id(1)

    @pl.when(c == 0)
    def _():
        s_ref[...] = jnp.zeros(s_ref.shape, F32)

    cos_t, sin_t = cos_ref[...], sin_ref[...]
    rq = _rotate_half(rq_ref[...], cos_t, sin_t)
    rk = _rotate_half(rk_ref[...], cos_t, sin_t) * (DK_R ** -0.5)
    rv = rv_ref[...]
    for h in range(H_R):
        qh = rq[:, h * DK_R:(h + 1) * DK_R].astype(BF16)
        kh = rk[:, h * DK_R:(h + 1) * DK_R]
        vh = rv[:, h * DV_R:(h + 1) * DV_R].astype(BF16)
        st = s_ref[h]
        scores = lax.dot_general(qh, kh.astype(BF16), NT_DIMS, preferred_element_type=F32) * dm_ref[h]
        inner = jnp.dot(scores.astype(BF16), vh, preferred_element_type=F32)
        cross = jnp.dot(qh, st.astype(BF16), preferred_element_type=F32) * cd_ref[:, h:h + 1]
        o_ref[:, h * DV_R:(h + 1) * DV_R] = inner + cross
        kdec = (kh * kd_ref[:, h:h + 1]).astype(BF16)
        upd = lax.dot_general(kdec, vh, TN_DIMS, preferred_element_type=F32)
        s_ref[h] = st * gl_ref[0:1, h:h + 1] + upd

    @pl.when(c == pl.num_programs(1) - 1)
    def _():
        st_ref[...] = s_ref[...]


def _ret_tables(ch):
    lg = jnp.log(1.0 - jnp.exp2(-5.0 - jnp.arange(H_R, dtype=F32)))
    n = jnp.arange(ch, dtype=F32)
    diff = n[:, None] - n[None, :]
    causal = diff >= 0
    dm = jnp.where(causal[None], jnp.exp(jnp.where(causal, diff, 0.0)[None] * lg[:, None, None]), 0.0)
    cd = jnp.exp((n[:, None] + 1.0) * lg[None, :])
    kd = jnp.exp((ch - 1.0 - n)[:, None] * lg[None, :])
    gl = jnp.exp(ch * lg)[None, :]
    pad = lambda a: jnp.pad(a, ((0, 0), (0, LANE - H_R)))
    return dm, pad(cd), pad(kd), pad(gl)


def _rope_tables(pos):
    half = DK_R // 2
    inv = ROPE_BASE ** (-jnp.arange(half, dtype=F32) / half)
    ang = pos.astype(F32)[:, None] * inv[None, :]
    cos, sin = jnp.cos(ang), jnp.sin(ang)
    cos_t = jnp.tile(jnp.concatenate([cos, cos], axis=1), (1, H_R))
    sin_t = jnp.tile(jnp.concatenate([-sin, sin], axis=1), (1, H_R))
    return cos_t, sin_t


def _ret_prompt(pf, cos_t, sin_t, b, s, ch):
    nc = s // ch
    dm, cd, kd, gl = _ret_tables(ch)
    tok = lambda col: (lambda bb, c: (bb * nc + c, col))
    return pl.pallas_call(
        _ret_prompt_kernel,
        grid=(b, nc),
        in_specs=[pl.BlockSpec((ch, D_RK), tok(E_RQ // D_RK)), pl.BlockSpec((ch, D_RK), tok(E_RK // D_RK)),
                  pl.BlockSpec((ch, D_RV), tok(E_RV // D_RV)),
                  pl.BlockSpec((ch, D_RK), lambda bb, c: (c, 0)), pl.BlockSpec((ch, D_RK), lambda bb, c: (c, 0)),
                  _const_spec(dm.shape), _const_spec(cd.shape), _const_spec(kd.shape), _const_spec(gl.shape)],
        out_specs=[pl.BlockSpec((ch, D_RV), lambda bb, c: (bb * nc + c, 0)),
                   pl.BlockSpec((None, H_R, DK_R, DV_R), lambda bb, c: (bb, 0, 0, 0))],
        out_shape=[jax.ShapeDtypeStruct((b * s, D_RV), F32), jax.ShapeDtypeStruct((b, H_R, DK_R, DV_R), F32)],
        scratch_shapes=[pltpu.VMEM((H_R, DK_R, DV_R), F32)],
        compiler_params=_cparams("parallel", "arbitrary"),
        name="retention_prompt",
    )(pf, pf, pf, cos_t, sin_t, dm, cd, kd, gl)


def _shift_rows(x, prev8, j):
    rolled = pltpu.roll(x, j, 0)
    rid = lax.broadcasted_iota(I32, prev8.shape, 0)
    head = jnp.where(rid < j, pltpu.roll(prev8, j, 0), rolled[0:8])
    return jnp.concatenate([head, rolled[8:]], axis=0)


def _conv_silu(x, prev8, w_ref, b_ref, col0, width):
    out = b_ref[0:1, col0:col0 + width] + x * w_ref[CONV_W - 1:CONV_W, col0:col0 + width]
    for j in range(1, CONV_W):
        out = out + _shift_rows(x, prev8, j) * w_ref[CONV_W - 1 - j:CONV_W - j, col0:col0 + width]
    return _silu(out)


def _ssd_prompt_kernel(xs_ref, bc_ref, dt_ref, cw_ref, cb_ref, dtb_ref, alog_ref, dsk_ref, tri_ref,
                       y_ref, st_ref, s_ref, pxs_ref, pbc_ref, *, ch):
    c = pl.program_id(1)

    @pl.when(c == 0)
    def _():
        s_ref[...] = jnp.zeros(s_ref.shape, F32)
        pxs_ref[...] = jnp.zeros(pxs_ref.shape, F32)
        pbc_ref[...] = jnp.zeros(pbc_ref.shape, F32)

    xs_raw = xs_ref[...]
    bc_raw = bc_ref[...]
    xs = _conv_silu(xs_raw, pxs_ref[...], cw_ref, cb_ref, 0, D_INNER)
    bc = _conv_silu(bc_raw, pbc_ref[...], cw_ref, cb_ref, D_INNER, 2 * D_BC)
    pxs_ref[...] = xs_raw[ch - 8:ch]
    pbc_ref[...] = bc_raw[ch - 8:ch]

    dt = _softplus(dt_ref[...] + dtb_ref[...])
    a = -jnp.exp(alog_ref[...])
    acs = jnp.dot(tri_ref[...], dt * a, preferred_element_type=F32, precision=lax.Precision.HIGHEST)
    acs_t = jnp.transpose(acs)
    e_acs = jnp.exp(acs)
    dec_end = jnp.exp(acs[ch - 1:ch, :] - acs)
    e_last = jnp.exp(acs[ch - 1:ch, :])
    tri = tri_ref[...] > 0.5
    dsk = dsk_ref[...]

    for g in range(G_SSM):
        bm = bc[:, g * N_STATE:(g + 1) * N_STATE].astype(BF16)
        cm = bc[:, D_BC + g * N_STATE:D_BC + (g + 1) * N_STATE].astype(BF16)
        cb = lax.dot_general(cm, bm, NT_DIMS, preferred_element_type=F32)
        for hh in range(HPG):
            h = g * HPG + hh
            xh = xs[:, h * P_SSM:(h + 1) * P_SSM]
            xdt = xh * dt[:, h:h + 1]
            seg = acs[:, h:h + 1] - acs_t[h:h + 1, :]
            lmat = jnp.exp(jnp.where(tri, seg, -jnp.inf))
            y_diag = jnp.dot((cb * lmat).astype(BF16), xdt.astype(BF16), preferred_element_type=F32)
            st = s_ref[h]
            y_off = lax.dot_general(cm, st.astype(BF16), NT_DIMS, preferred_element_type=F32) * e_acs[:, h:h + 1]
            y_ref[:, h * P_SSM:(h + 1) * P_SSM] = y_diag + y_off + dsk[0:1, h:h + 1] * xh
            xd = (xdt * dec_end[:, h:h + 1]).astype(BF16)
            upd = lax.dot_general(xd, bm, TN_DIMS, preferred_element_type=F32)
            s_ref[h] = st * e_last[0:1, h:h + 1] + upd

    @pl.when(c == pl.num_programs(1) - 1)
    def _():
        st_ref[...] = s_ref[...]


def _pad_lanes(v):
    return jnp.pad(v.reshape(1, -1), ((0, 0), (0, LANE - v.shape[-1])))


def _ssd_prompt(pf, cw, cb, dtb, alog, dsk, b, s, ch):
    nc = s // ch
    tri = jnp.tril(jnp.ones((ch, ch), F32))
    tok = lambda col: (lambda bb, c: (bb * nc + c, col))
    return pl.pallas_call(
        functools.partial(_ssd_prompt_kernel, ch=ch),
        grid=(b, nc),
        in_specs=[pl.BlockSpec((ch, D_INNER), tok(O_XS // D_INNER)), pl.BlockSpec((ch, 2 * D_BC), tok(O_BC // (2 * D_BC))),
                  pl.BlockSpec((ch, LANE), tok(O_DT // LANE)),
                  _const_spec(cw.shape), _const_spec((1, CONV_DIM)), _const_spec((1, LANE)), _const_spec((1, LANE)),
                  _const_spec((1, LANE)), _const_spec((ch, ch))],
        out_specs=[pl.BlockSpec((ch, D_INNER), lambda bb, c: (bb * nc + c, 0)),
                   pl.BlockSpec((None, NH_SSM, P_SSM, N_STATE), lambda bb, c: (bb, 0, 0, 0))],
        out_shape=[jax.ShapeDtypeStruct((b * s, D_INNER), F32),
                   jax.ShapeDtypeStruct((b, NH_SSM, P_SSM, N_STATE), F32)],
        scratch_shapes=[pltpu.VMEM((NH_SSM, P_SSM, N_STATE), F32), pltpu.VMEM((8, D_INNER), F32),
                        pltpu.VMEM((8, 2 * D_BC), F32)],
        compiler_params=_cparams("parallel", "arbitrary"),
        name="ssd_prompt",
    )(pf, pf, pf, cw, cb.reshape(1, CONV_DIM), _pad_lanes(dtb), _pad_lanes(alog), _pad_lanes(dsk), tri)


def _row_to_col(row, n):
    eye = lax.broadcasted_iota(I32, (n, n), 0) == lax.broadcasted_iota(I32, (n, n), 1)
    return jnp.sum(jnp.where(eye, jnp.broadcast_to(row, (n, n)), 0.0), axis=1, keepdims=True)


def _col_to_row(col, n):
    eye = lax.broadcasted_iota(I32, (n, n), 0) == lax.broadcasted_iota(I32, (n, n), 1)
    return jnp.sum(jnp.where(eye, jnp.broadcast_to(col, (n, n)), 0.0), axis=0, keepdims=True)


def _sample_scores_kernel(pt_ref, iq_ref, iw_ref, *refs, pg):
    page_refs, o_ref = refs[:pg], refs[pg]
    iq = iq_ref[...]
    iw = iw_ref[...] * ((D_IDX ** -0.5) * (H_IDX ** -0.5))
    for u in range(pg):
        s = lax.dot_general(iq, page_refs[u][...].astype(BF16), NT_DIMS, preferred_element_type=F32)
        o_ref[u:u + 1, :] = jnp.sum(iw * jnp.maximum(s, 0.0), axis=0, keepdims=True)


def _sample_scores(iq, iw, cik, page_table, pg):
    bs, n_pages = page_table.shape
    page_spec = lambda u: pl.BlockSpec((None, PAGE_SIZE, D_IDX), lambda b, j, pt: (pt[b, j * pg + u], 0, 0))
    grid_spec = pltpu.PrefetchScalarGridSpec(
        num_scalar_prefetch=1, grid=(bs, n_pages // pg),
        in_specs=[pl.BlockSpec((None, H_IDX, D_IDX), lambda b, j, pt: (b, 0, 0)),
                  pl.BlockSpec((None, H_IDX, 1), lambda b, j, pt: (b, 0, 0))] + [page_spec(u) for u in range(pg)],
        out_specs=pl.BlockSpec((None, pg, PAGE_SIZE), lambda b, j, pt: (b, j, 0)))
    return pl.pallas_call(
        functools.partial(_sample_scores_kernel, pg=pg),
        grid_spec=grid_spec,
        out_shape=jax.ShapeDtypeStruct((bs, n_pages, PAGE_SIZE), F32),
        compiler_params=_cparams("parallel", "arbitrary"),
        name="sample_scores",
    )(page_table, iq, iw, *([cik] * pg))


def _sample_attn_kernel(pt_ref, sc_ref, q_ref, kn_ref, vn_ref, iqf_ref, ikn_ref, iwf_ref, tbs_ref, b31_ref, b0_ref,
                        *refs, pg, n_pages, k_sel, idx_bits):
    k_refs, v_refs = refs[:pg], refs[pg:2 * pg]
    o_ref, sk_ref, sel_ref, m_ref, l_ref, acc_ref = refs[2 * pg:]
    j = pl.program_id(1)
    lane = lax.broadcasted_iota(I32, (H_A, D_ATT), 1)
    sub = lax.broadcasted_iota(I32, (H_A, D_ATT), 0)
    head_mask = lane // DH_A == sub
    qbd = jnp.where(head_mask, jnp.broadcast_to(q_ref[...], (H_A, D_ATT)), 0.0)

    @pl.when(j == 0)
    def _():
        iqf = iqf_ref[...]
        s_new = jnp.sum(iqf * ikn_ref[...], axis=1, keepdims=True)
        w = iwf_ref[...] * ((D_IDX ** -0.5) * (H_IDX ** -0.5))
        sc_new = jnp.sum(w * jnp.maximum(s_new, 0.0), axis=0, keepdims=True)
        key_new = _score_key(sc_new)
        keys = _score_key(sc_ref[...])
        sk_ref[...] = keys
        kidx = (lax.broadcasted_iota(I32, keys.shape, 0) * PAGE_SIZE + lax.broadcasted_iota(I32, keys.shape, 1))
        idx_new = n_pages * PAGE_SIZE

        def total(mask, extra):
            part = jnp.sum(jnp.where(mask, 1.0, 0.0), axis=0, keepdims=True)
            return jnp.sum(part, axis=1, keepdims=True) + jnp.where(extra, 1.0, 0.0)

        def count_ge(cand):
            return total(keys >= cand, key_new >= cand)

        thr = _kth_largest(count_ge, 1, float(k_sel))
        n_gt = count_ge(thr + 1)
        need = float(k_sel) - n_gt

        def count_tie_lt(cand):
            return total(jnp.logical_and(keys == thr, kidx < cand), jnp.logical_and(key_new == thr, idx_new < cand))

        lim = _tie_limit(count_tie_lt, need, 1, idx_bits)
        sel = jnp.logical_or(keys > thr, jnp.logical_and(keys == thr, kidx <= lim))
        sel_ref[...] = jnp.where(sel, 1.0, 0.0)
        sel_new = jnp.logical_or(key_new > thr, jnp.logical_and(key_new == thr, idx_new <= lim))
        lg_new = (jnp.sum(qbd * kn_ref[...], axis=1, keepdims=True) * (DH_A ** -0.5) + b0_ref[...])
        m_ref[...] = jnp.broadcast_to(jnp.where(sel_new, lg_new, NEG_BIG), m_ref.shape)
        l0 = jnp.broadcast_to(jnp.where(sel_new, 1.0, 0.0), (H_A, 1))
        l_ref[...] = jnp.broadcast_to(l0, l_ref.shape)
        acc_ref[...] = l0 * jnp.broadcast_to(vn_ref[...], (H_A, D_ATT))

    qb = qbd.astype(BF16)
    for u in range(pg):
        p_idx = j * pg + u
        kp = k_refs[u][...].astype(BF16)
        vp = v_refs[u][...].astype(BF16)
        s = lax.dot_general(qb, kp, NT_DIMS, preferred_element_type=F32) * (DH_A ** -0.5)
        s = s + jnp.where(p_idx == n_pages - 1, tbs_ref[...], b31_ref[...])
        sel = sel_ref[pl.ds(p_idx, 1), :] > 0.5
        s = jnp.where(sel, s, NEG_BIG)
        m_old = m_ref[:, 0:1]
        m_new = jnp.maximum(m_old, jnp.max(s, axis=1, keepdims=True))
        alpha = jnp.exp(m_old - m_new)
        p = jnp.exp(s - m_new)
        l_new = alpha * l_ref[:, 0:1] + jnp.sum(p, axis=1, keepdims=True)
        acc_ref[...] = alpha * acc_ref[...] + jnp.dot(p.astype(BF16), vp, preferred_element_type=F32)
        m_ref[...] = jnp.broadcast_to(m_new, m_ref.shape)
        l_ref[...] = jnp.broadcast_to(l_new, l_ref.shape)

    @pl.when(j == pl.num_programs(1) - 1)
    def _():
        out = jnp.where(head_mask, acc_ref[...] / l_ref[:, 0:1], 0.0)
        o_ref[...] = jnp.sum(out, axis=0, keepdims=True)


def _sample_attn(scores, q, k_new, v_new, iqf, ik_new, iwf, ck, cv, page_table, tbs, b31c, b0c, pg):
    bs, n_pages = page_table.shape
    k_sel = min(TOPK_MAX, (n_pages * PAGE_SIZE + 1) // 4)
    idx_bits = int(math.log2(n_pages * PAGE_SIZE)) + 1
    per_b = lambda shape: pl.BlockSpec((None,) + shape, lambda b, j, pt: (b,) + (0,) * len(shape))
    page_spec = lambda u: pl.BlockSpec((None, PAGE_SIZE, D_ATT), lambda b, j, pt: (pt[b, j * pg + u], 0, 0))
    const = lambda shape: pl.BlockSpec(shape, lambda b, j, pt: (0,) * len(shape))
    grid_spec = pltpu.PrefetchScalarGridSpec(
        num_scalar_prefetch=1, grid=(bs, n_pages // pg),
        in_specs=[per_b((n_pages, PAGE_SIZE)), per_b((1, D_ATT)), per_b((1, D_ATT)), per_b((1, D_ATT)),
                  per_b((H_IDX, D_IDX)), per_b((1, D_IDX)), per_b((H_IDX, 1)),
                  const(tbs.shape), const(b31c.shape), const(b0c.shape)]
        + [page_spec(u) for u in range(pg)] + [page_spec(u) for u in range(pg)],
        out_specs=per_b((1, D_ATT)),
        scratch_shapes=[pltpu.VMEM((n_pages, PAGE_SIZE), I32), pltpu.VMEM((n_pages, PAGE_SIZE), F32),
                        pltpu.VMEM((H_A, LANE), F32), pltpu.VMEM((H_A, LANE), F32), pltpu.VMEM((H_A, D_ATT), F32)])
    return pl.pallas_call(
        functools.partial(_sample_attn_kernel, pg=pg, n_pages=n_pages, k_sel=k_sel, idx_bits=idx_bits),
        grid_spec=grid_spec,
        out_shape=jax.ShapeDtypeStruct((bs, 1, D_ATT), F32),
        compiler_params=_cparams("parallel", "arbitrary"),
        name="sample_attn",
    )(page_table, scores, q, k_new, v_new, iqf, ik_new, iwf, tbs, b31c, b0c, *([ck] * pg), *([cv] * pg))


def _sample_ret_kernel(rq_ref, rk_ref, rv_ref, cos_ref, sin_ref, g_ref, s_ref, o_ref, sn_ref):
    cos_t, sin_t = cos_ref[...], sin_ref[...]
    rq = _rotate_half(rq_ref[...], cos_t, sin_t)
    rk = _rotate_half(rk_ref[...], cos_t, sin_t) * (DK_R ** -0.5)
    rv = rv_ref[...]
    for h in range(H_R):
        qc = _row_to_col(rq[:, h * DK_R:(h + 1) * DK_R], DK_R)
        kc = _row_to_col(rk[:, h * DK_R:(h + 1) * DK_R], DK_R)
        vh = rv[:, h * DV_R:(h + 1) * DV_R]
        s_new = s_ref[h] * g_ref[0:1, h:h + 1] + kc * vh
        sn_ref[h] = s_new
        o_ref[:, h * DV_R:(h + 1) * DV_R] = jnp.sum(qc * s_new, axis=0, keepdims=True)


def _sample_ret(rq, rk, rv, cos_t, sin_t, state):
    bs = rq.shape[0]
    gamma = _pad_lanes(1.0 - jnp.exp2(-5.0 - jnp.arange(H_R, dtype=F32)))
    per_b = lambda shape: pl.BlockSpec((None,) + shape, lambda b: (b,) + (0,) * len(shape))
    return pl.pallas_call(
        _sample_ret_kernel,
        grid=(bs,),
        in_specs=[per_b((1, D_RK)), per_b((1, D_RK)), per_b((1, D_RV)), _const_spec((1, D_RK)), _const_spec((1, D_RK)),
                  _const_spec((1, LANE)), per_b((H_R, DK_R, DV_R))],
        out_specs=[per_b((1, D_RV)), per_b((H_R, DK_R, DV_R))],
        out_shape=[jax.ShapeDtypeStruct((bs, 1, D_RV), F32), jax.ShapeDtypeStruct(state.shape, F32)],
        compiler_params=_cparams("parallel"),
        name="sample_retention",
    )(rq, rk, rv, cos_t, sin_t, gamma, state)


def _sample_ssd_kernel(xbc_ref, dt_ref, hist_ref, cw_ref, cb_ref, dtb_ref, alog_ref, dsk_ref, s_ref,
                       y_ref, hn_ref, sn_ref):
    x_new = xbc_ref[...]
    hist = hist_ref[...]
    conv = cb_ref[...] + x_new * cw_ref[CONV_W - 1:CONV_W, :]
    for j in range(CONV_W - 1):
        conv = conv + hist[j:j + 1, :] * cw_ref[j:j + 1, :]
    xbc = _silu(conv)
    hn_ref[0:CONV_W - 2, :] = hist[1:CONV_W - 1, :]
    hn_ref[CONV_W - 2:CONV_W - 1, :] = x_new
    dt = _softplus(dt_ref[...] + dtb_ref[...])
    d_a = jnp.exp(dt * (-jnp.exp(alog_ref[...])))
    dsk = dsk_ref[...]
    for g in range(G_SSM):
        bm = xbc[:, D_INNER + g * N_STATE:D_INNER + (g + 1) * N_STATE]
        cm = xbc[:, D_INNER + D_BC + g * N_STATE:D_INNER + D_BC + (g + 1) * N_STATE]
        for hh in range(HPG):
            h = g * HPG + hh
            xh = xbc[:, h * P_SSM:(h + 1) * P_SSM]
            xc = _row_to_col(xh * dt[:, h:h + 1], P_SSM)
            s_new = s_ref[h] * d_a[0:1, h:h + 1] + xc * bm
            sn_ref[h] = s_new
            yc = jnp.sum(s_new * cm, axis=1, keepdims=True)
            y_ref[:, h * P_SSM:(h + 1) * P_SSM] = _col_to_row(yc, P_SSM) + dsk[0:1, h:h + 1] * xh


def _sample_ssd(xbc, dt, hist, cw, cb, dtb, alog, dsk, state):
    bs = xbc.shape[0]
    per_b = lambda shape: pl.BlockSpec((None,) + shape, lambda b: (b,) + (0,) * len(shape))
    return pl.pallas_call(
        _sample_ssd_kernel,
        grid=(bs,),
        in_specs=[per_b((1, CONV_DIM)), per_b((1, LANE)), per_b((CONV_W - 1, CONV_DIM)),
                  _const_spec(cw.shape), _const_spec((1, CONV_DIM)), _const_spec((1, LANE)), _const_spec((1, LANE)),
                  _const_spec((1, LANE)), per_b((NH_SSM, P_SSM, N_STATE))],
        out_specs=[per_b((1, D_INNER)), per_b((CONV_W - 1, CONV_DIM)), per_b((NH_SSM, P_SSM, N_STATE))],
        out_shape=[jax.ShapeDtypeStruct((bs, 1, D_INNER), F32), jax.ShapeDtypeStruct(hist.shape, F32),
                   jax.ShapeDtypeStruct(state.shape, F32)],
        compiler_params=_cparams("parallel"),
        name="sample_ssd",
    )(xbc, dt, hist, cw, cb.reshape(1, CONV_DIM), _pad_lanes(dtb), _pad_lanes(alog), _pad_lanes(dsk), state)


def _even_weight(w):
    q, k, v, iq, ik, iw, rq, rk, rv, rg = jnp.split(w, np.cumsum(EVEN_SIZES)[:-1].tolist(), axis=1)
    pad = jnp.zeros((w.shape[0], EVEN_PAD - sum(EVEN_SIZES)), w.dtype)
    return jnp.concatenate([q, k, v, iq, rv, rg, rq, rk, ik, iw, pad], axis=1).astype(BF16)


def _odd_weight(w):
    return jnp.pad(w, ((0, 0), (0, ODD_PAD - ODD_IN))).astype(BF16)


def _t5_bucket(dist):
    max_exact = NUM_BUCKETS // 2
    d = jnp.maximum(dist, 0)
    ratio = jnp.log(jnp.maximum(d, max_exact).astype(F32) / max_exact) / math.log(MAX_DISTANCE / max_exact)
    large = max_exact + (ratio * (NUM_BUCKETS - max_exact)).astype(I32)
    return jnp.where(d < max_exact, d, jnp.minimum(large, NUM_BUCKETS - 1))


def _bias_tables(rel_bias, tq):
    assert tq >= MAX_DISTANCE
    dist = tq + jnp.arange(tq)[:, None] - jnp.arange(2 * tq)[None, :]
    tb = jnp.where((dist >= 0)[..., None], rel_bias[_t5_bucket(dist)], 0.0)
    tb = jnp.moveaxis(tb, -1, 0)
    b31 = rel_bias[NUM_BUCKETS - 1].reshape(1, H_A)
    tbs = rel_bias[_t5_bucket(PAGE_SIZE - jnp.arange(PAGE_SIZE))].T
    return tb, jnp.pad(b31, ((0, 0), (0, LANE - H_A))), tbs


def kernel(x_prompt, x_sample, cache_k, cache_v, cache_idx_k, state_ret, state_conv, state_ssm, page_table, rel_bias, norm_mix, norm_mlp, norm_final, w_in_even, w_out_even, w_in_odd, conv_w, conv_b, dt_bias, a_log, d_skip, ssm_norm, w_out_odd, w_up, w_down):
    b, s, d = x_prompt.shape
    bs = x_sample.shape[0]
    n_pages = page_table.shape[1]
    past = n_pages * PAGE_SIZE
    tm = min(512, b * s)
    tq = 128
    tk = min(512, s)
    ch = 128
    pg = 8

    xp = x_prompt.reshape(b * s, d)
    xs = x_sample.reshape(bs, d)
    tb, b31, tbs = _bias_tables(rel_bias, tq)
    b31c = rel_bias[NUM_BUCKETS - 1].reshape(H_A, 1)
    b0c = rel_bias[0].reshape(H_A, 1)
    cos_p, sin_p = _rope_tables(jnp.arange(s))
    cos_s, sin_s = _rope_tables(jnp.full((1,), past))
    bf_slices = ((0, 4 * D_ATT), (E_IK, LANE))

    k_p, v_p, ik_p, k_s, v_s, ik_s, ret_p, ret_s = [], [], [], [], [], [], [], []
    conv_p, conv_s, ssm_p, ssm_s = [], [], [], []
    for layer in range(DEPTH):
        final = layer == DEPTH - 1
        wup, wdn = w_up[layer].astype(BF16), w_down[layer].astype(BF16)
        if layer % 2 == 0:
            e = layer // 2
            w_in = _even_weight(w_in_even[e])
            wo = w_out_even[e].astype(BF16)
            pf, pbq, pbi = _rms_matmul(xp, norm_mix[layer], w_in, tm, bf_slices)
            sf, sbq, _ = _rms_matmul(xs, norm_mix[layer], w_in, bs, bf_slices)
            k_p.append(pf[:, E_K:E_K + D_ATT].reshape(b, s, H_A, DH_A))
            v_p.append(pf[:, E_V:E_V + D_ATT].reshape(b, s, H_A, DH_A))
            ik_p.append(pf[:, E_IK:E_IK + D_IDX].reshape(b, s, D_IDX))
            k_s.append(sf[:, E_K:E_K + D_ATT].reshape(bs, 1, H_A, DH_A))
            v_s.append(sf[:, E_V:E_V + D_ATT].reshape(bs, 1, H_A, DH_A))
            ik_s.append(sf[:, E_IK:E_IK + D_IDX].reshape(bs, 1, D_IDX))
            att = _attn_prompt(pf, pbq, pbi, tb, b31, b, s, tq, tk)
            o, st = _ret_prompt(pf, cos_p, sin_p, b, s, ch)
            ret_p.append(st)
            xp = _mlp(_even_merge(xp, att, o, pf, wo, tm), norm_mlp[layer], wup, wdn, norm_final, tm, final)
            iq_bf = sbq[:, E_IQ:E_IQ + D_ATT].reshape(bs, H_IDX, D_IDX)
            iqf = sf[:, E_IQ:E_IQ + D_ATT].reshape(bs, H_IDX, D_IDX)
            iwf = sf[:, E_IK + E_IW:E_IK + E_IW + H_IDX].reshape(bs, H_IDX, 1)
            scores = _sample_scores(iq_bf, iwf, cache_idx_k[e], page_table, pg)
            n_pool = cache_k.shape[1]
            att_s = _sample_attn(scores, sf[:, E_Q:E_Q + D_ATT].reshape(bs, 1, D_ATT),
                                 sf[:, E_K:E_K + D_ATT].reshape(bs, 1, D_ATT),
                                 sf[:, E_V:E_V + D_ATT].reshape(bs, 1, D_ATT),
                                 iqf, sf[:, E_IK:E_IK + D_IDX].reshape(bs, 1, D_IDX), iwf,
                                 cache_k[e].reshape(n_pool, PAGE_SIZE, D_ATT), cache_v[e].reshape(n_pool, PAGE_SIZE, D_ATT),
                                 page_table, tbs, b31c, b0c, pg)
            o_s, st_s = _sample_ret(sf[:, E_RQ:E_RQ + D_RK].reshape(bs, 1, D_RK), sf[:, E_RK:E_RK + D_RK].reshape(bs, 1, D_RK),
                                    sf[:, E_RV:E_RV + D_RV].reshape(bs, 1, D_RV), cos_s, sin_s, state_ret[e])
            ret_s.append(st_s)
            xs = _mlp(_even_merge(xs, att_s.reshape(bs, D_ATT), o_s.reshape(bs, D_RV), sf, wo, bs),
                      norm_mlp[layer], wup, wdn, norm_final, bs, final)
        else:
            o_ = layer // 2
            w_in = _odd_weight(w_in_odd[o_])
            wo = w_out_odd[o_].astype(BF16)
            (pf,) = _rms_matmul(xp, norm_mix[layer], w_in, tm // 2)
            (sf,) = _rms_matmul(xs, norm_mix[layer], w_in, bs)
            y, st = _ssd_prompt(pf, conv_w[o_], conv_b[o_], dt_bias[o_], a_log[o_], d_skip[o_], b, s, ch)
            conv_p.append(pf.reshape(b, s, ODD_PAD)[:, s - (CONV_W - 1):, O_XS:O_XS + CONV_DIM])
            ssm_p.append(st)
            xp = _mlp(_odd_merge(xp, y, pf, ssm_norm[o_], wo, tm), norm_mlp[layer], wup, wdn, norm_final, tm, final)
            y_s, hn, st_s = _sample_ssd(sf[:, O_XS:O_XS + CONV_DIM].reshape(bs, 1, CONV_DIM),
                                        sf[:, O_DT:O_DT + LANE].reshape(bs, 1, LANE), state_conv[o_],
                                        conv_w[o_], conv_b[o_], dt_bias[o_], a_log[o_], d_skip[o_], state_ssm[o_])
            conv_s.append(hn)
            ssm_s.append(st_s)
            xs = _mlp(_odd_merge(xs, y_s.reshape(bs, D_INNER), sf, ssm_norm[o_], wo, bs),
                      norm_mlp[layer], wup, wdn, norm_final, bs, final)

    return (xp.reshape(b, s, d), xs.reshape(bs, 1, d), jnp.stack(k_p), jnp.stack(v_p), jnp.stack(ik_p),
            jnp.stack(k_s), jnp.stack(v_s), jnp.stack(ik_s), jnp.stack(ret_p), jnp.stack(ret_s),
            jnp.stack(conv_p), jnp.stack(conv_s), jnp.stack(ssm_p), jnp.stack(ssm_s))
```

```python
import functools
import math

import jax
import jax.numpy as jnp
import numpy as np
from jax import lax
from jax.experimental import pallas as pl
from jax.experimental.pallas import tpu as pltpu

F32 = jnp.float32
BF16 = jnp.bfloat16
I32 = jnp.int32

D_MODEL = 1024
DEPTH = 4
PAGE_SIZE = 128
DH_A = 64
H_A = (D_MODEL // 2) // DH_A
D_ATT = H_A * DH_A
H_IDX = 8
D_IDX = 64
TOPK_MAX = 256
NUM_BUCKETS = 32
MAX_DISTANCE = 128
H_R = 4
DV_R = (D_MODEL // 2) // H_R
DK_R = DV_R // 2
D_RK = H_R * DK_R
D_RV = H_R * DV_R
ROPE_BASE = 10000.0
D_INNER = 2 * D_MODEL
P_SSM = 64
NH_SSM = D_INNER // P_SSM
G_SSM = 8
HPG = NH_SSM // G_SSM
N_STATE = 128
CONV_W = 4
D_BC = G_SSM * N_STATE
CONV_DIM = D_INNER + 2 * D_BC
D_FF = 4 * D_MODEL
EPS = 1e-6

EVEN_SIZES = (D_ATT, D_ATT, D_ATT, H_IDX * D_IDX, D_IDX, H_IDX, D_RK, D_RK, D_RV, D_RV)
ODD_IN = D_INNER + CONV_DIM + NH_SSM

E_Q, E_K, E_V, E_IQ = 0, 512, 1024, 1536
E_RV, E_RG, E_RQ, E_RK, E_IK = 2048, 2560, 3072, 3328, 3584
E_IW = D_IDX
EVEN_PAD = 3712
O_Z, O_XS, O_BC, O_DT = 0, 2048, 4096, 6144
ODD_PAD = 6272

LANE = 128
VMEM_LIMIT = 56 * 1024 * 1024
NEG_BIG = -1e30
INT_MIN = -(2 ** 31)
KEY_NEG_INF = int(np.array(0xFF800000, np.uint32).view(np.int32)) ^ 0x7FFFFFFF

NT_DIMS = (((1,), (1,)), ((), ()))
TN_DIMS = (((0,), (0,)), ((), ()))


def _cparams(*sem):
    return pltpu.CompilerParams(dimension_semantics=sem, vmem_limit_bytes=VMEM_LIMIT)


def _const_spec(shape):
    nd = len(shape)
    return pl.BlockSpec(shape, lambda *_: (0,) * nd, pipeline_mode=pl.Buffered(1))


def _rms(x, g):
    return x * lax.rsqrt(jnp.mean(x * x, axis=-1, keepdims=True) + EPS) * g


def _silu(x):
    return x / (1.0 + jnp.exp(-x))


def _softplus(x):
    return jnp.maximum(x, 0.0) + jnp.log(1.0 + jnp.exp(-jnp.abs(x)))


def _rms_matmul_kernel(x_ref, g_ref, w_ref, of_ref, *ob_refs, bf_slices, n_chunk):
    h = _rms(x_ref[...], g_ref[...]).astype(BF16)
    n = w_ref.shape[1]
    for c0 in range(0, n, n_chunk):
        c1 = min(c0 + n_chunk, n)
        of_ref[:, c0:c1] = jnp.dot(h, w_ref[:, c0:c1], preferred_element_type=F32)
    for (start, width), r in zip(bf_slices, ob_refs):
        r[...] = of_ref[:, start:start + width].astype(BF16)


def _rms_matmul(x, g, w, tm, bf_slices=()):
    t, d = x.shape
    n = w.shape[1]
    out_shape = [jax.ShapeDtypeStruct((t, n), F32)]
    out_specs = [pl.BlockSpec((tm, n), lambda i: (i, 0))]
    for _, width in bf_slices:
        out_shape.append(jax.ShapeDtypeStruct((t, width), BF16))
        out_specs.append(pl.BlockSpec((tm, width), lambda i: (i, 0)))
    return pl.pallas_call(
        functools.partial(_rms_matmul_kernel, bf_slices=tuple(bf_slices), n_chunk=512),
        grid=(t // tm,),
        in_specs=[pl.BlockSpec((tm, d), lambda i: (i, 0)), _const_spec((1, d)), _const_spec((d, n))],
        out_specs=out_specs,
        out_shape=out_shape,
        compiler_params=_cparams("parallel"),
        name="rms_matmul",
    )(x, g.reshape(1, d), w)


def _mlp_tail(x, gm_ref, wup_ref, wdn_ref, f_chunk=1024):
    h = _rms(x, gm_ref[...]).astype(BF16)
    acc = x
    for f in range(0, D_FF, f_chunk):
        u = jnp.dot(h, wup_ref[:, f:f + f_chunk], preferred_element_type=F32)
        u = jnp.maximum(u, 0.0)
        u = (u * u).astype(BF16)
        acc = acc + jnp.dot(u, wdn_ref[f:f + f_chunk, :], preferred_element_type=F32)
    return acc


def _even_merge_kernel(x_ref, att_ref, o_ref, rg_ref, wo_ref, out_ref):
    o = o_ref[...]
    parts = []
    for h in range(H_R):
        oh = o[:, h * DV_R:(h + 1) * DV_R]
        d = oh - jnp.mean(oh, axis=-1, keepdims=True)
        parts.append(d * lax.rsqrt(jnp.mean(d * d, axis=-1, keepdims=True) + EPS))
    ret = _silu(rg_ref[...]) * jnp.concatenate(parts, axis=1)
    x = x_ref[...]
    x = x + jnp.dot(att_ref[...].astype(BF16), wo_ref[0:D_ATT, :], preferred_element_type=F32)
    out_ref[...] = x + jnp.dot(ret.astype(BF16), wo_ref[D_ATT:D_ATT + D_RV, :], preferred_element_type=F32)


def _odd_merge_kernel(x_ref, y_ref, z_ref, nw_ref, wo_ref, out_ref):
    g = y_ref[...] * _silu(z_ref[...])
    gw = D_INNER // G_SSM
    parts = []
    for k in range(G_SSM):
        gk = g[:, k * gw:(k + 1) * gw]
        parts.append(gk * lax.rsqrt(jnp.mean(gk * gk, axis=-1, keepdims=True) + EPS))
    gn = (jnp.concatenate(parts, axis=1) * nw_ref[...]).astype(BF16)
    out_ref[...] = x_ref[...] + jnp.dot(gn, wo_ref[...], preferred_element_type=F32)


def _mlp_kernel(x_ref, gm_ref, wup_ref, wdn_ref, gf_ref, out_ref, *, final):
    x = _mlp_tail(x_ref[...], gm_ref, wup_ref, wdn_ref)
    if final:
        x = _rms(x, gf_ref[...])
    out_ref[...] = x


def _mlp(x, gm, wup, wdn, gf, tm, final):
    t, d = x.shape
    row = lambda i: (i, 0)
    return pl.pallas_call(
        functools.partial(_mlp_kernel, final=final),
        grid=(t // tm,),
        in_specs=[pl.BlockSpec((tm, d), row), _const_spec((1, d)), _const_spec(wup.shape), _const_spec(wdn.shape),
                  _const_spec((1, d))],
        out_specs=pl.BlockSpec((tm, d), row),
        out_shape=jax.ShapeDtypeStruct((t, d), F32),
        compiler_params=_cparams("parallel"),
        name="mlp",
    )(x, gm.reshape(1, d), wup, wdn, gf.reshape(1, d))


def _even_merge(x, att, o, pf, wo, tm):
    t, d = x.shape
    row = lambda i: (i, 0)
    return pl.pallas_call(
        _even_merge_kernel,
        grid=(t // tm,),
        in_specs=[pl.BlockSpec((tm, d), row), pl.BlockSpec((tm, D_ATT), row), pl.BlockSpec((tm, D_RV), row),
                  pl.BlockSpec((tm, D_RV), lambda i: (i, E_RG // D_RV)), _const_spec(wo.shape)],
        out_specs=pl.BlockSpec((tm, d), row),
        out_shape=jax.ShapeDtypeStruct((t, d), F32),
        compiler_params=_cparams("parallel"),
        name="even_merge",
    )(x, att, o, pf, wo)


def _odd_merge(x, y, pf, nw, wo, tm):
    t, d = x.shape
    row = lambda i: (i, 0)
    return pl.pallas_call(
        _odd_merge_kernel,
        grid=(t // tm,),
        in_specs=[pl.BlockSpec((tm, d), row), pl.BlockSpec((tm, D_INNER), row),
                  pl.BlockSpec((tm, D_INNER), lambda i: (i, O_Z // D_INNER)),
                  _const_spec((1, D_INNER)), _const_spec(wo.shape)],
        out_specs=pl.BlockSpec((tm, d), row),
        out_shape=jax.ShapeDtypeStruct((t, d), F32),
        compiler_params=_cparams("parallel"),
        name="odd_merge",
    )(x, y, pf, nw.reshape(1, D_INNER), wo)


FOLD = 64


def _fold_rows(x, op):
    n, w = x.shape
    if n > FOLD:
        x = op(x.reshape(n // FOLD, FOLD, w), axis=0)
    if x.shape[0] > 8:
        x = op(x.reshape(x.shape[0] // 8, 8, w), axis=0)
    return op(x, axis=0, keepdims=True)


def _score_key(score):
    bits = lax.bitcast_convert_type(score, I32)
    return bits ^ ((bits >> 31) & 0x7FFFFFFF)


def _kth_largest(count_ge, rows, k_sel):
    def bit_step(b, t):
        cand = t + lax.shift_left(jnp.int32(1), 31 - b)
        return jnp.where(count_ge(cand) >= k_sel, cand, t)
    return lax.fori_loop(0, 32, bit_step, jnp.full((rows, 1), INT_MIN, I32))


def _tie_limit(count_tie_lt, need, rows, n_bits):
    def bit_step(b, x):
        cand = x + lax.shift_left(jnp.int32(1), n_bits - 1 - b)
        return jnp.where(count_tie_lt(cand) < need, cand, x)
    return lax.fori_loop(0, n_bits, bit_step, jnp.zeros((rows, 1), I32))


def _attn_prompt_kernel(q_ref, iq_ref, iw_ref, k_ref, vt_ref, ik_ref, tb_ref, b31_ref, o_ref,
                        sk_ref, x_ref, m_ref, l_ref, acc_ref, s_ref, p_ref, *, tq, tk, k_sel, idx_bits):
    i = pl.program_id(1)
    r = tk // tq
    nkc = (i + r) // r
    nfar = jnp.maximum(i - 1, 0) // r
    qidx = lax.broadcasted_iota(I32, (tk, tq), 1) + i * tq
    krow = lax.broadcasted_iota(I32, (tk, tq), 0)
    iw = iw_ref[...] * ((D_IDX ** -0.5) * (H_IDX ** -0.5))

    def key_fold(m):
        return jnp.sum(m.reshape(tk // FOLD, FOLD, tq), axis=0)

    def score_chunk(c, carry):
        ikc = ik_ref[c][:, 0:D_IDX]
        acc = jnp.zeros((tk, tq), F32)
        for h in range(H_IDX):
            s = lax.dot_general(ikc, iq_ref[:, h * D_IDX:(h + 1) * D_IDX], NT_DIMS, preferred_element_type=F32)
            acc = acc + iw[h:h + 1, :] * jnp.maximum(s, 0.0)
        sk_ref[c] = _score_key(jnp.where(krow + c * tk <= qidx, acc, -jnp.inf))
        return carry
    lax.fori_loop(0, nkc, score_chunk, 0)

    def count_ge(cand):
        def body(c, part):
            return part + key_fold(jnp.where(sk_ref[c] >= cand, 1.0, 0.0))
        part = lax.fori_loop(0, nkc, body, jnp.zeros((FOLD, tq), F32))
        return _fold_rows(part, jnp.sum)

    def kth_step(b, t):
        cand = t + lax.shift_left(jnp.int32(1), 31 - b)
        return jnp.where(count_ge(cand) >= float(k_sel), cand, t)
    thr = lax.fori_loop(0, 32, kth_step, jnp.full((1, tq), INT_MIN, I32))
    n_gt = count_ge(thr + 1)
    n_ge = count_ge(thr)
    need = float(k_sel) - n_gt
    excess = jnp.logical_and(n_ge - n_gt > need, thr > KEY_NEG_INF)
    x_ref[...] = jnp.full((1, tq), 2 ** idx_bits, I32)

    @pl.when(jnp.max(jnp.where(excess, 1.0, 0.0)) > 0.0)
    def _():
        def count_tie_lt(cand):
            def body(c, part):
                hit = jnp.logical_and(sk_ref[c] == thr, krow + c * tk < cand)
                return part + key_fold(jnp.where(hit, 1.0, 0.0))
            part = lax.fori_loop(0, nkc, body, jnp.zeros((FOLD, tq), F32))
            return _fold_rows(part, jnp.sum)

        def tie_step(b, x):
            cand = x + lax.shift_left(jnp.int32(1), idx_bits - 1 - b)
            return jnp.where(count_tie_lt(cand) < need, cand, x)
        lim = lax.fori_loop(0, idx_bits, tie_step, jnp.zeros((1, tq), I32))
        x_ref[...] = jnp.where(excess, lim, 2 ** idx_bits)

    xlim = x_ref[...]

    m_ref[...] = jnp.full(m_ref.shape, NEG_BIG, F32)
    l_ref[...] = jnp.zeros(l_ref.shape, F32)
    acc_ref[...] = jnp.zeros(acc_ref.shape, F32)

    def attend_chunk(c, near):
        key = sk_ref[c]
        kidx = krow + c * tk
        sel = jnp.logical_or(key > thr, jnp.logical_and(key == thr, kidx <= xlim))
        if near:
            sel = jnp.logical_and(sel, kidx <= qidx)
        amask = jnp.where(sel, 0.0, NEG_BIG)
        kc = k_ref[c]
        vtc = vt_ref[c]
        def logits(h):
            hs = slice(h * DH_A, (h + 1) * DH_A)
            qh = q_ref[:, hs] * (DH_A ** -0.5)
            return lax.dot_general(kc[:, hs], qh, NT_DIMS, preferred_element_type=F32)

        for h in range(H_A):
            s_ref[h] = logits(h)
        alphas = []
        for h in range(H_A):
            b_far = b31_ref[0:1, h:h + 1]
            s = s_ref[h] + amask
            m_old = m_ref[h:h + 1, :]
            if near:
                tiles = []
                for t in range(r):
                    jt = c * r + t
                    tiles.append(jnp.where(jt == i, tb_ref[h, tq:2 * tq, :],
                                           jnp.where(jt == i - 1, tb_ref[h, 0:tq, :], b_far)))
                s = s + (tiles[0] if r == 1 else jnp.concatenate(tiles, axis=0))
                m_new = jnp.maximum(m_old, _fold_rows(s, jnp.max))
                shift = -m_new
            else:
                m_new = jnp.maximum(m_old, _fold_rows(s, jnp.max) + b_far)
                shift = b_far - m_new
            alpha = jnp.exp(m_old - m_new)
            p = jnp.exp(s + shift)
            l_ref[h:h + 1, :] = alpha * l_ref[h:h + 1, :] + _fold_rows(p, jnp.sum)
            m_ref[h:h + 1, :] = m_new
            p_ref[h] = p.astype(BF16)
            alphas.append(alpha)
        for h in range(H_A):
            hs = slice(h * DH_A, (h + 1) * DH_A)
            pv = jnp.dot(vtc[hs, :], p_ref[h], preferred_element_type=F32)
            acc_ref[hs, :] = alphas[h] * acc_ref[hs, :] + pv

    def far_body(c, carry):
        attend_chunk(c, False)
        return carry

    def near_body(c, carry):
        attend_chunk(c, True)
        return carry

    lax.fori_loop(0, nfar, far_body, 0)
    lax.fori_loop(nfar, nkc, near_body, 0)

    for h in range(H_A):
        hs = slice(h * DH_A, (h + 1) * DH_A)
        acc_ref[hs, :] = acc_ref[hs, :] / l_ref[h:h + 1, :]
    o_ref[...] = jnp.transpose(acc_ref[...])


def _attn_prompt(pf, pbq, pbi, tb_t, b31, b, s, tq, tk):
    nq, nkc = s // tq, s // tk
    k_sel = min(TOPK_MAX, s // 4)
    idx_bits = int(math.log2(s))
    assert 2 ** idx_bits == s and tk % tq == 0 and tq % LANE == 0
    pbq4 = pbq.reshape(b, nkc, tk, pbq.shape[1])
    pbi4 = pbi.reshape(b, nkc, tk, LANE)
    vt4 = jnp.swapaxes(pbq4[..., E_V:E_V + D_ATT], 2, 3)
    iw_t = jnp.transpose(pf[:, E_IK + E_IW:E_IK + E_IW + H_IDX])
    return pl.pallas_call(
        functools.partial(_attn_prompt_kernel, tq=tq, tk=tk, k_sel=k_sel, idx_bits=idx_bits),
        grid=(b, nq),
        in_specs=[
            pl.BlockSpec((tq, D_ATT), lambda bb, i: (bb * nq + i, E_Q // D_ATT)),
            pl.BlockSpec((tq, D_ATT), lambda bb, i: (bb * nq + i, E_IQ // D_ATT)),
            pl.BlockSpec((H_IDX, tq), lambda bb, i: (0, bb * nq + i)),
            pl.BlockSpec((None, nkc, tk, D_ATT), lambda bb, i: (bb, 0, 0, E_K // D_ATT), pipeline_mode=pl.Buffered(1)),
            pl.BlockSpec((None, nkc, D_ATT, tk), lambda bb, i: (bb, 0, 0, 0), pipeline_mode=pl.Buffered(1)),
            pl.BlockSpec((None, nkc, tk, LANE), lambda bb, i: (bb, 0, 0, 0), pipeline_mode=pl.Buffered(1)),
            _const_spec(tb_t.shape), _const_spec(b31.shape),
        ],
        out_specs=pl.BlockSpec((tq, D_ATT), lambda bb, i: (bb * nq + i, 0)),
        out_shape=jax.ShapeDtypeStruct((b * s, D_ATT), F32),
        scratch_shapes=[pltpu.VMEM((nkc, tk, tq), I32), pltpu.VMEM((1, tq), I32),
                        pltpu.VMEM((H_A, tq), F32), pltpu.VMEM((H_A, tq), F32),
                        pltpu.VMEM((D_ATT, tq), F32),
                        pltpu.VMEM((H_A, tk, tq), F32), pltpu.VMEM((H_A, tk, tq), BF16)],
        compiler_params=_cparams("parallel", "arbitrary"),
        name="attn_prompt",
    )(pbq, pbq, iw_t, pbq4, vt4, pbi4, tb_t, b31)


def _rotate_half(x, cos_t, sin_t):
    half = DK_R // 2
    width = x.shape[1]
    lane = lax.broadcasted_iota(I32, x.shape, 1)
    partner = jnp.where(lane % DK_R < half, pltpu.roll(x, width - half, 1), pltpu.roll(x, half, 1))
    return x * cos_t + partner * sin_t


def _ret_prompt_kernel(rq_ref, rk_ref, rv_ref, cos_ref, sin_ref, dm_ref, cd_ref, kd_ref, gl_ref,
                       o_ref, st_ref, s_ref):
    c = pl.program_id(1)

    @pl.when(c == 0)
    def _():
        s_ref[...] = jnp.zeros(s_ref.shape, F32)

    cos_t, sin_t = cos_ref[...], sin_ref[...]
    rq = _rotate_half(rq_ref[...], cos_t, sin_t)
    rk = _rotate_half(rk_ref[...], cos_t, sin_t) * (DK_R ** -0.5)
    rv = rv_ref[...]
    for h in range(H_R):
        qh = rq[:, h * DK_R:(h + 1) * DK_R].astype(BF16)
        kh = rk[:, h * DK_R:(h + 1) * DK_R]
        vh = rv[:, h * DV_R:(h + 1) * DV_R].astype(BF16)
        st = s_ref[h]
        scores = lax.dot_general(qh, kh.astype(BF16), NT_DIMS, preferred_element_type=F32) * dm_ref[h]
        inner = jnp.dot(scores.astype(BF16), vh, preferred_element_type=F32)
        cross = jnp.dot(qh, st.astype(BF16), preferred_element_type=F32) * cd_ref[:, h:h + 1]
        o_ref[:, h * DV_R:(h + 1) * DV_R] = inner + cross
        kdec = (kh * kd_ref[:, h:h + 1]).astype(BF16)
        upd = lax.dot_general(kdec, vh, TN_DIMS, preferred_element_type=F32)
        s_ref[h] = st * gl_ref[0:1, h:h + 1] + upd

    @pl.when(c == pl.num_programs(1) - 1)
    def _():
        st_ref[...] = s_ref[...]


def _ret_tables(ch):
    lg = jnp.log(1.0 - jnp.exp2(-5.0 - jnp.arange(H_R, dtype=F32)))
    n = jnp.arange(ch, dtype=F32)
    diff = n[:, None] - n[None, :]
    causal = diff >= 0
    dm = jnp.where(causal[None], jnp.exp(jnp.where(causal, diff, 0.0)[None] * lg[:, None, None]), 0.0)
    cd = jnp.exp((n[:, None] + 1.0) * lg[None, :])
    kd = jnp.exp((ch - 1.0 - n)[:, None] * lg[None, :])
    gl = jnp.exp(ch * lg)[None, :]
    pad = lambda a: jnp.pad(a, ((0, 0), (0, LANE - H_R)))
    return dm, pad(cd), pad(kd), pad(gl)


def _rope_tables(pos):
    half = DK_R // 2
    inv = ROPE_BASE ** (-jnp.arange(half, dtype=F32) / half)
    ang = pos.astype(F32)[:, None] * inv[None, :]
    cos, sin = jnp.cos(ang), jnp.sin(ang)
    cos_t = jnp.tile(jnp.concatenate([cos, cos], axis=1), (1, H_R))
    sin_t = jnp.tile(jnp.concatenate([-sin, sin], axis=1), (1, H_R))
    return cos_t, sin_t


def _ret_prompt(pf, cos_t, sin_t, b, s, ch):
    nc = s // ch
    dm, cd, kd, gl = _ret_tables(ch)
    tok = lambda col: (lambda bb, c: (bb * nc + c, col))
    return pl.pallas_call(
        _ret_prompt_kernel,
        grid=(b, nc),
        in_specs=[pl.BlockSpec((ch, D_RK), tok(E_RQ // D_RK)), pl.BlockSpec((ch, D_RK), tok(E_RK // D_RK)),
                  pl.BlockSpec((ch, D_RV), tok(E_RV // D_RV)),
                  pl.BlockSpec((ch, D_RK), lambda bb, c: (c, 0)), pl.BlockSpec((ch, D_RK), lambda bb, c: (c, 0)),
                  _const_spec(dm.shape), _const_spec(cd.shape), _const_spec(kd.shape), _const_spec(gl.shape)],
        out_specs=[pl.BlockSpec((ch, D_RV), lambda bb, c: (bb * nc + c, 0)),
                   pl.BlockSpec((None, H_R, DK_R, DV_R), lambda bb, c: (bb, 0, 0, 0))],
        out_shape=[jax.ShapeDtypeStruct((b * s, D_RV), F32), jax.ShapeDtypeStruct((b, H_R, DK_R, DV_R), F32)],
        scratch_shapes=[pltpu.VMEM((H_R, DK_R, DV_R), F32)],
        compiler_params=_cparams("parallel", "arbitrary"),
        name="retention_prompt",
    )(pf, pf, pf, cos_t, sin_t, dm, cd, kd, gl)


def _shift_rows(x, prev8, j):
    rolled = pltpu.roll(x, j, 0)
    rid = lax.broadcasted_iota(I32, prev8.shape, 0)
    head = jnp.where(rid < j, pltpu.roll(prev8, j, 0), rolled[0:8])
    return jnp.concatenate([head, rolled[8:]], axis=0)


def _conv_silu(x, prev8, w_ref, b_ref, col0, width):
    out = b_ref[0:1, col0:col0 + width] + x * w_ref[CONV_W - 1:CONV_W, col0:col0 + width]
    for j in range(1, CONV_W):
        out = out + _shift_rows(x, prev8, j) * w_ref[CONV_W - 1 - j:CONV_W - j, col0:col0 + width]
    return _silu(out)


def _ssd_prompt_kernel(xs_ref, bc_ref, dt_ref, cw_ref, cb_ref, dtb_ref, alog_ref, dsk_ref, tri_ref,
                       y_ref, st_ref, s_ref, pxs_ref, pbc_ref, *, ch):
    c = pl.program_id(1)

    @pl.when(c == 0)
    def _():
        s_ref[...] = jnp.zeros(s_ref.shape, F32)
        pxs_ref[...] = jnp.zeros(pxs_ref.shape, F32)
        pbc_ref[...] = jnp.zeros(pbc_ref.shape, F32)

    xs_raw = xs_ref[...]
    bc_raw = bc_ref[...]
    xs = _conv_silu(xs_raw, pxs_ref[...], cw_ref, cb_ref, 0, D_INNER)
    bc = _conv_silu(bc_raw, pbc_ref[...], cw_ref, cb_ref, D_INNER, 2 * D_BC)
    pxs_ref[...] = xs_raw[ch - 8:ch]
    pbc_ref[...] = bc_raw[ch - 8:ch]

    dt = _softplus(dt_ref[...] + dtb_ref[...])
    a = -jnp.exp(alog_ref[...])
    acs = jnp.dot(tri_ref[...], dt * a, preferred_element_type=F32, precision=lax.Precision.HIGHEST)
    acs_t = jnp.transpose(acs)
    e_acs = jnp.exp(acs)
    dec_end = jnp.exp(acs[ch - 1:ch, :] - acs)
    e_last = jnp.exp(acs[ch - 1:ch, :])
    tri = tri_ref[...] > 0.5
    dsk = dsk_ref[...]

    for g in range(G_SSM):
        bm = bc[:, g * N_STATE:(g + 1) * N_STATE].astype(BF16)
        cm = bc[:, D_BC + g * N_STATE:D_BC + (g + 1) * N_STATE].astype(BF16)
        cb = lax.dot_general(cm, bm, NT_DIMS, preferred_element_type=F32)
        for hh in range(HPG):
            h = g * HPG + hh
            xh = xs[:, h * P_SSM:(h + 1) * P_SSM]
            xdt = xh * dt[:, h:h + 1]
            seg = acs[:, h:h + 1] - acs_t[h:h + 1, :]
            lmat = jnp.exp(jnp.where(tri, seg, -jnp.inf))
            y_diag = jnp.dot((cb * lmat).astype(BF16), xdt.astype(BF16), preferred_element_type=F32)
            st = s_ref[h]
            y_off = lax.dot_general(cm, st.astype(BF16), NT_DIMS, preferred_element_type=F32) * e_acs[:, h:h + 1]
            y_ref[:, h * P_SSM:(h + 1) * P_SSM] = y_diag + y_off + dsk[0:1, h:h + 1] * xh
            xd = (xdt * dec_end[:, h:h + 1]).astype(BF16)
            upd = lax.dot_general(xd, bm, TN_DIMS, preferred_element_type=F32)
            s_ref[h] = st * e_last[0:1, h:h + 1] + upd

    @pl.when(c == pl.num_programs(1) - 1)
    def _():
        st_ref[...] = s_ref[...]


def _pad_lanes(v):
    return jnp.pad(v.reshape(1, -1), ((0, 0), (0, LANE - v.shape[-1])))


def _ssd_prompt(pf, cw, cb, dtb, alog, dsk, b, s, ch):
    nc = s // ch
    tri = jnp.tril(jnp.ones((ch, ch), F32))
    tok = lambda col: (lambda bb, c: (bb * nc + c, col))
    return pl.pallas_call(
        functools.partial(_ssd_prompt_kernel, ch=ch),
        grid=(b, nc),
        in_specs=[pl.BlockSpec((ch, D_INNER), tok(O_XS // D_INNER)), pl.BlockSpec((ch, 2 * D_BC), tok(O_BC // (2 * D_BC))),
                  pl.BlockSpec((ch, LANE), tok(O_DT // LANE)),
                  _const_spec(cw.shape), _const_spec((1, CONV_DIM)), _const_spec((1, LANE)), _const_spec((1, LANE)),
                  _const_spec((1, LANE)), _const_spec((ch, ch))],
        out_specs=[pl.BlockSpec((ch, D_INNER), lambda bb, c: (bb * nc + c, 0)),
                   pl.BlockSpec((None, NH_SSM, P_SSM, N_STATE), lambda bb, c: (bb, 0, 0, 0))],
        out_shape=[jax.ShapeDtypeStruct((b * s, D_INNER), F32),
                   jax.ShapeDtypeStruct((b, NH_SSM, P_SSM, N_STATE), F32)],
        scratch_shapes=[pltpu.VMEM((NH_SSM, P_SSM, N_STATE), F32), pltpu.VMEM((8, D_INNER), F32),
                        pltpu.VMEM((8, 2 * D_BC), F32)],
        compiler_params=_cparams("parallel", "arbitrary"),
        name="ssd_prompt",
    )(pf, pf, pf, cw, cb.reshape(1, CONV_DIM), _pad_lanes(dtb), _pad_lanes(alog), _pad_lanes(dsk), tri)


def _row_to_col(row, n):
    eye = lax.broadcasted_iota(I32, (n, n), 0) == lax.broadcasted_iota(I32, (n, n), 1)
    return jnp.sum(jnp.where(eye, jnp.broadcast_to(row, (n, n)), 0.0), axis=1, keepdims=True)


def _col_to_row(col, n):
    eye = lax.broadcasted_iota(I32, (n, n), 0) == lax.broadcasted_iota(I32, (n, n), 1)
    return jnp.sum(jnp.where(eye, jnp.broadcast_to(col, (n, n)), 0.0), axis=0, keepdims=True)


def _sample_scores_kernel(pt_ref, iq_ref, iw_ref, *refs, pg):
    page_refs, o_ref = refs[:pg], refs[pg]
    iq = iq_ref[...]
    iw = iw_ref[...] * ((D_IDX ** -0.5) * (H_IDX ** -0.5))
    for u in range(pg):
        s = jnp.dot(iq, page_refs[u][...].astype(BF16), preferred_element_type=F32)
        o_ref[u:u + 1, :] = jnp.sum(iw * jnp.maximum(s, 0.0), axis=0, keepdims=True)


def _sample_scores(iq, iw, cik_t, e, page_table, pg):
    bs, n_pages = page_table.shape
    page_spec = lambda u: pl.BlockSpec((None, None, D_IDX, PAGE_SIZE),
                                       lambda b, j, pt: (e, pt[b, j * pg + u], 0, 0))
    grid_spec = pltpu.PrefetchScalarGridSpec(
        num_scalar_prefetch=1, grid=(bs, n_pages // pg),
        in_specs=[pl.BlockSpec((None, H_IDX, D_IDX), lambda b, j, pt: (b, 0, 0)),
                  pl.BlockSpec((None, H_IDX, 1), lambda b, j, pt: (b, 0, 0))] + [page_spec(u) for u in range(pg)],
        out_specs=pl.BlockSpec((None, pg, PAGE_SIZE), lambda b, j, pt: (b, j, 0)))
    return pl.pallas_call(
        functools.partial(_sample_scores_kernel, pg=pg),
        grid_spec=grid_spec,
        out_shape=jax.ShapeDtypeStruct((bs, n_pages, PAGE_SIZE), F32),
        compiler_params=_cparams("parallel", "arbitrary"),
        name="sample_scores",
    )(page_table, iq, iw, *([cik_t] * pg))


def _sample_attn_kernel(pt_ref, sc_ref, q_ref, kn_ref, vn_ref, iqf_ref, ikn_ref, iwf_ref, tbs_ref, b31_ref, b0_ref,
                        *refs, pg, n_pages, k_sel, idx_bits):
    k_refs, v_refs = refs[:pg], refs[pg:2 * pg]
    o_ref, sk_ref, sel_ref, m_ref, l_ref, acc_ref = refs[2 * pg:]
    j = pl.program_id(1)
    lane = lax.broadcasted_iota(I32, (H_A, D_ATT), 1)
    sub = lax.broadcasted_iota(I32, (H_A, D_ATT), 0)
    head_mask = lane // DH_A == sub
    qbd = jnp.where(head_mask, jnp.broadcast_to(q_ref[...], (H_A, D_ATT)), 0.0)

    @pl.when(j == 0)
    def _():
        iqf = iqf_ref[...]
        s_new = jnp.sum(iqf * ikn_ref[...], axis=1, keepdims=True)
        w = iwf_ref[...] * ((D_IDX ** -0.5) * (H_IDX ** -0.5))
        sc_new = jnp.sum(w * jnp.maximum(s_new, 0.0), axis=0, keepdims=True)
        key_new = _score_key(sc_new)
        keys = _score_key(sc_ref[...])
        sk_ref[...] = keys
        kidx = (lax.broadcasted_iota(I32, keys.shape, 0) * PAGE_SIZE + lax.broadcasted_iota(I32, keys.shape, 1))
        idx_new = n_pages * PAGE_SIZE

        def total(mask, extra):
            part = jnp.sum(jnp.where(mask, 1.0, 0.0), axis=0, keepdims=True)
            return jnp.sum(part, axis=1, keepdims=True) + jnp.where(extra, 1.0, 0.0)

        def count_ge(cand):
            return total(keys >= cand, key_new >= cand)

        thr = _kth_largest(count_ge, 1, float(k_sel))
        n_gt = count_ge(thr + 1)
        need = float(k_sel) - n_gt

        def count_tie_lt(cand):
            return total(jnp.logical_and(keys == thr, kidx < cand), jnp.logical_and(key_new == thr, idx_new < cand))

        lim = _tie_limit(count_tie_lt, need, 1, idx_bits)
        sel = jnp.logical_or(keys > thr, jnp.logical_and(keys == thr, kidx <= lim))
        sel_ref[...] = jnp.where(sel, 1.0, 0.0)
        sel_new = jnp.logical_or(key_new > thr, jnp.logical_and(key_new == thr, idx_new <= lim))
        lg_new = (jnp.sum(qbd * kn_ref[...], axis=1, keepdims=True) * (DH_A ** -0.5) + b0_ref[...])
        m_ref[...] = jnp.broadcast_to(jnp.where(sel_new, lg_new, NEG_BIG), m_ref.shape)
        l0 = jnp.broadcast_to(jnp.where(sel_new, 1.0, 0.0), (H_A, 1))
        l_ref[...] = jnp.broadcast_to(l0, l_ref.shape)
        acc_ref[...] = l0 * jnp.broadcast_to(vn_ref[...], (H_A, D_ATT))

    qb = qbd.astype(BF16)
    for u in range(pg):
        p_idx = j * pg + u
        kp = k_refs[u][...].astype(BF16)
        vp = v_refs[u][...].astype(BF16)
        s = jnp.dot(qb, kp, preferred_element_type=F32) * (DH_A ** -0.5)
        s = s + jnp.where(p_idx == n_pages - 1, tbs_ref[...], b31_ref[...])
        sel = sel_ref[pl.ds(p_idx, 1), :] > 0.5
        s = jnp.where(sel, s, NEG_BIG)
        m_old = m_ref[:, 0:1]
        m_new = jnp.maximum(m_old, jnp.max(s, axis=1, keepdims=True))
        alpha = jnp.exp(m_old - m_new)
        p = jnp.exp(s - m_new)
        l_new = alpha * l_ref[:, 0:1] + jnp.sum(p, axis=1, keepdims=True)
        acc_ref[...] = alpha * acc_ref[...] + lax.dot_general(p.astype(BF16), vp, NT_DIMS,
                                                              preferred_element_type=F32)
        m_ref[...] = jnp.broadcast_to(m_new, m_ref.shape)
        l_ref[...] = jnp.broadcast_to(l_new, l_ref.shape)

    @pl.when(j == pl.num_programs(1) - 1)
    def _():
        out = jnp.where(head_mask, acc_ref[...] / l_ref[:, 0:1], 0.0)
        o_ref[...] = jnp.sum(out, axis=0, keepdims=True)


def _sample_attn(scores, q, k_new, v_new, iqf, ik_new, iwf, ck_t, cv_t, e, page_table, tbs, b31c, b0c, pg):
    bs, n_pages = page_table.shape
    k_sel = min(TOPK_MAX, (n_pages * PAGE_SIZE + 1) // 4)
    idx_bits = int(math.log2(n_pages * PAGE_SIZE)) + 1
    per_b = lambda shape: pl.BlockSpec((None,) + shape, lambda b, j, pt: (b,) + (0,) * len(shape))
    page_spec = lambda u: pl.BlockSpec((None, None, D_ATT, PAGE_SIZE),
                                       lambda b, j, pt: (e, pt[b, j * pg + u], 0, 0))
    const = lambda shape: pl.BlockSpec(shape, lambda b, j, pt: (0,) * len(shape))
    grid_spec = pltpu.PrefetchScalarGridSpec(
        num_scalar_prefetch=1, grid=(bs, n_pages // pg),
        in_specs=[per_b((n_pages, PAGE_SIZE)), per_b((1, D_ATT)), per_b((1, D_ATT)), per_b((1, D_ATT)),
                  per_b((H_IDX, D_IDX)), per_b((1, D_IDX)), per_b((H_IDX, 1)),
                  const(tbs.shape), const(b31c.shape), const(b0c.shape)]
        + [page_spec(u) for u in range(pg)] + [page_spec(u) for u in range(pg)],
        out_specs=per_b((1, D_ATT)),
        scratch_shapes=[pltpu.VMEM((n_pages, PAGE_SIZE), I32), pltpu.VMEM((n_pages, PAGE_SIZE), F32),
                        pltpu.VMEM((H_A, LANE), F32), pltpu.VMEM((H_A, LANE), F32), pltpu.VMEM((H_A, D_ATT), F32)])
    return pl.pallas_call(
        functools.partial(_sample_attn_kernel, pg=pg, n_pages=n_pages, k_sel=k_sel, idx_bits=idx_bits),
        grid_spec=grid_spec,
        out_shape=jax.ShapeDtypeStruct((bs, 1, D_ATT), F32),
        compiler_params=_cparams("parallel", "arbitrary"),
        name="sample_attn",
    )(page_table, scores, q, k_new, v_new, iqf, ik_new, iwf, tbs, b31c, b0c, *([ck_t] * pg), *([cv_t] * pg))


def _sample_ret_kernel(rq_ref, rk_ref, rv_ref, cos_ref, sin_ref, g_ref, s_ref, o_ref, sn_ref):
    cos_t, sin_t = cos_ref[...], sin_ref[...]
    rq = _rotate_half(rq_ref[...], cos_t, sin_t)
    rk = _rotate_half(rk_ref[...], cos_t, sin_t) * (DK_R ** -0.5)
    rv = rv_ref[...]
    for h in range(H_R):
        qc = _row_to_col(rq[:, h * DK_R:(h + 1) * DK_R], DK_R)
        kc = _row_to_col(rk[:, h * DK_R:(h + 1) * DK_R], DK_R)
        vh = rv[:, h * DV_R:(h + 1) * DV_R]
        s_new = s_ref[h] * g_ref[0:1, h:h + 1] + kc * vh
        sn_ref[h] = s_new
        o_ref[:, h * DV_R:(h + 1) * DV_R] = jnp.sum(qc * s_new, axis=0, keepdims=True)


def _layer_b_spec(layer, shape):
    return pl.BlockSpec((None, None) + shape, lambda b: (layer, b) + (0,) * len(shape))


def _sample_ret(rq, rk, rv, cos_t, sin_t, state, e):
    bs = rq.shape[0]
    gamma = _pad_lanes(1.0 - jnp.exp2(-5.0 - jnp.arange(H_R, dtype=F32)))
    per_b = lambda shape: pl.BlockSpec((None,) + shape, lambda b: (b,) + (0,) * len(shape))
    return pl.pallas_call(
        _sample_ret_kernel,
        grid=(bs,),
        in_specs=[per_b((1, D_RK)), per_b((1, D_RK)), per_b((1, D_RV)), _const_spec((1, D_RK)), _const_spec((1, D_RK)),
                  _const_spec((1, LANE)), _layer_b_spec(e, (H_R, DK_R, DV_R))],
        out_specs=[per_b((1, D_RV)), per_b((H_R, DK_R, DV_R))],
        out_shape=[jax.ShapeDtypeStruct((bs, 1, D_RV), F32), jax.ShapeDtypeStruct(state.shape[1:], F32)],
        compiler_params=_cparams("parallel"),
        name="sample_retention",
    )(rq, rk, rv, cos_t, sin_t, gamma, state)


def _sample_ssd_kernel(xbc_ref, dt_ref, hist_ref, cw_ref, cb_ref, dtb_ref, alog_ref, dsk_ref, s_ref,
                       y_ref, hn_ref, sn_ref):
    x_new = xbc_ref[...]
    hist = hist_ref[...]
    conv = cb_ref[...] + x_new * cw_ref[CONV_W - 1:CONV_W, :]
    for j in range(CONV_W - 1):
        conv = conv + hist[j:j + 1, :] * cw_ref[j:j + 1, :]
    xbc = _silu(conv)
    hn_ref[0:CONV_W - 2, :] = hist[1:CONV_W - 1, :]
    hn_ref[CONV_W - 2:CONV_W - 1, :] = x_new
    dt = _softplus(dt_ref[...] + dtb_ref[...])
    d_a = jnp.exp(dt * (-jnp.exp(alog_ref[...])))
    dsk = dsk_ref[...]
    for g in range(G_SSM):
        bm = xbc[:, D_INNER + g * N_STATE:D_INNER + (g + 1) * N_STATE]
        cm = xbc[:, D_INNER + D_BC + g * N_STATE:D_INNER + D_BC + (g + 1) * N_STATE]
        for hh in range(HPG):
            h = g * HPG + hh
            xh = xbc[:, h * P_SSM:(h + 1) * P_SSM]
            xc = _row_to_col(xh * dt[:, h:h + 1], P_SSM)
            s_new = s_ref[h] * d_a[0:1, h:h + 1] + xc * bm
            sn_ref[h] = s_new
            yc = jnp.sum(s_new * cm, axis=1, keepdims=True)
            y_ref[:, h * P_SSM:(h + 1) * P_SSM] = _col_to_row(yc, P_SSM) + dsk[0:1, h:h + 1] * xh


def _sample_ssd(xbc, dt, hist, cw, cb, dtb, alog, dsk, state, layer):
    bs = xbc.shape[0]
    per_b = lambda shape: pl.BlockSpec((None,) + shape, lambda b: (b,) + (0,) * len(shape))
    return pl.pallas_call(
        _sample_ssd_kernel,
        grid=(bs,),
        in_specs=[per_b((1, CONV_DIM)), per_b((1, LANE)), per_b((CONV_W - 1, CONV_DIM)),
                  _const_spec(cw.shape), _const_spec((1, CONV_DIM)), _const_spec((1, LANE)), _const_spec((1, LANE)),
                  _const_spec((1, LANE)), _layer_b_spec(layer, (NH_SSM, P_SSM, N_STATE))],
        out_specs=[per_b((1, D_INNER)), per_b((CONV_W - 1, CONV_DIM)), per_b((NH_SSM, P_SSM, N_STATE))],
        out_shape=[jax.ShapeDtypeStruct((bs, 1, D_INNER), F32), jax.ShapeDtypeStruct(hist.shape, F32),
                   jax.ShapeDtypeStruct(state.shape[1:], F32)],
        compiler_params=_cparams("parallel"),
        name="sample_ssd",
    )(xbc, dt, hist, cw, cb.reshape(1, CONV_DIM), _pad_lanes(dtb), _pad_lanes(alog), _pad_lanes(dsk), state)


def _even_weight(w):
    q, k, v, iq, ik, iw, rq, rk, rv, rg = jnp.split(w, np.cumsum(EVEN_SIZES)[:-1].tolist(), axis=1)
    pad = jnp.zeros((w.shape[0], EVEN_PAD - sum(EVEN_SIZES)), w.dtype)
    return jnp.concatenate([q, k, v, iq, rv, rg, rq, rk, ik, iw, pad], axis=1).astype(BF16)


def _odd_weight(w):
    return jnp.pad(w, ((0, 0), (0, ODD_PAD - ODD_IN))).astype(BF16)


def _t5_bucket(dist):
    max_exact = NUM_BUCKETS // 2
    d = jnp.maximum(dist, 0)
    ratio = jnp.log(jnp.maximum(d, max_exact).astype(F32) / max_exact) / math.log(MAX_DISTANCE / max_exact)
    large = max_exact + (ratio * (NUM_BUCKETS - max_exact)).astype(I32)
    return jnp.where(d < max_exact, d, jnp.minimum(large, NUM_BUCKETS - 1))


def _bias_tables(rel_bias, tq):
    assert tq >= MAX_DISTANCE
    dist = tq + jnp.arange(tq)[None, :] - jnp.arange(2 * tq)[:, None]
    tb = jnp.where((dist >= 0)[..., None], rel_bias[_t5_bucket(dist)], 0.0)
    tb = jnp.moveaxis(tb, -1, 0)
    b31 = rel_bias[NUM_BUCKETS - 1].reshape(1, H_A)
    tbs = rel_bias[_t5_bucket(PAGE_SIZE - jnp.arange(PAGE_SIZE))].T
    return tb, jnp.pad(b31, ((0, 0), (0, LANE - H_A))), tbs


def kernel(x_prompt, x_sample, cache_k, cache_v, cache_idx_k, state_ret, state_conv, state_ssm, page_table, rel_bias, norm_mix, norm_mlp, norm_final, w_in_even, w_out_even, w_in_odd, conv_w, conv_b, dt_bias, a_log, d_skip, ssm_norm, w_out_odd, w_up, w_down):
    b, s, d = x_prompt.shape
    bs = x_sample.shape[0]
    n_pages = page_table.shape[1]
    past = n_pages * PAGE_SIZE
    tm = min(512, b * s)
    tq = 128
    tk = min(512, s)
    ch = 128
    pg = 8

    xp = x_prompt.reshape(b * s, d)
    xs = x_sample.reshape(bs, d)
    tb, b31, tbs = _bias_tables(rel_bias, tq)
    b31c = rel_bias[NUM_BUCKETS - 1].reshape(H_A, 1)
    b0c = rel_bias[0].reshape(H_A, 1)
    cos_p, sin_p = _rope_tables(jnp.arange(s))
    cos_s, sin_s = _rope_tables(jnp.full((1,), past))
    bf_slices = ((0, 4 * D_ATT), (E_IK, LANE))
    n_layers, n_pool = cache_k.shape[0], cache_k.shape[1]
    ck_t = jnp.transpose(cache_k, (0, 1, 3, 4, 2)).reshape(n_layers, n_pool, D_ATT, PAGE_SIZE)
    cv_t = jnp.transpose(cache_v, (0, 1, 3, 4, 2)).reshape(n_layers, n_pool, D_ATT, PAGE_SIZE)
    cik_t = jnp.transpose(cache_idx_k, (0, 1, 3, 2))

    k_p, v_p, ik_p, k_s, v_s, ik_s, ret_p, ret_s = [], [], [], [], [], [], [], []
    conv_p, conv_s, ssm_p, ssm_s = [], [], [], []
    for layer in range(DEPTH):
        final = layer == DEPTH - 1
        wup, wdn = w_up[layer].astype(BF16), w_down[layer].astype(BF16)
        if layer % 2 == 0:
            e = layer // 2
            w_in = _even_weight(w_in_even[e])
            wo = w_out_even[e].astype(BF16)
            pf, pbq, pbi = _rms_matmul(xp, norm_mix[layer], w_in, tm, bf_slices)
            sf, sbq, _ = _rms_matmul(xs, norm_mix[layer], w_in, bs, bf_slices)
            k_p.append(pf[:, E_K:E_K + D_ATT].reshape(b, s, H_A, DH_A))
            v_p.append(pf[:, E_V:E_V + D_ATT].reshape(b, s, H_A, DH_A))
            ik_p.append(pf[:, E_IK:E_IK + D_IDX].reshape(b, s, D_IDX))
            k_s.append(sf[:, E_K:E_K + D_ATT].reshape(bs, 1, H_A, DH_A))
            v_s.append(sf[:, E_V:E_V + D_ATT].reshape(bs, 1, H_A, DH_A))
            ik_s.append(sf[:, E_IK:E_IK + D_IDX].reshape(bs, 1, D_IDX))
            att = _attn_prompt(pf, pbq, pbi, tb, b31, b, s, tq, tk)
            o, st = _ret_prompt(pf, cos_p, sin_p, b, s, ch)
            ret_p.append(st)
            xp = _mlp(_even_merge(xp, att, o, pf, wo, tm), norm_mlp[layer], wup, wdn, norm_final, tm, final)
            iq_bf = sbq[:, E_IQ:E_IQ + D_ATT].reshape(bs, H_IDX, D_IDX)
            iqf = sf[:, E_IQ:E_IQ + D_ATT].reshape(bs, H_IDX, D_IDX)
            iwf = sf[:, E_IK + E_IW:E_IK + E_IW + H_IDX].reshape(bs, H_IDX, 1)
            scores = _sample_scores(iq_bf, iwf, cik_t, e, page_table, pg)
            att_s = _sample_attn(scores, sf[:, E_Q:E_Q + D_ATT].reshape(bs, 1, D_ATT),
                                 sf[:, E_K:E_K + D_ATT].reshape(bs, 1, D_ATT),
                                 sf[:, E_V:E_V + D_ATT].reshape(bs, 1, D_ATT),
                                 iqf, sf[:, E_IK:E_IK + D_IDX].reshape(bs, 1, D_IDX), iwf,
                                 ck_t, cv_t, e, page_table, tbs, b31c, b0c, pg)
            o_s, st_s = _sample_ret(sf[:, E_RQ:E_RQ + D_RK].reshape(bs, 1, D_RK), sf[:, E_RK:E_RK + D_RK].reshape(bs, 1, D_RK),
                                    sf[:, E_RV:E_RV + D_RV].reshape(bs, 1, D_RV), cos_s, sin_s, state_ret, e)
            ret_s.append(st_s)
            xs = _mlp(_even_merge(xs, att_s.reshape(bs, D_ATT), o_s.reshape(bs, D_RV), sf, wo, bs),
                      norm_mlp[layer], wup, wdn, norm_final, bs, final)
        else:
            o_ = layer // 2
            w_in = _odd_weight(w_in_odd[o_])
            wo = w_out_odd[o_].astype(BF16)
            (pf,) = _rms_matmul(xp, norm_mix[layer], w_in, tm // 2)
            (sf,) = _rms_matmul(xs, norm_mix[layer], w_in, bs)
            y, st = _ssd_prompt(pf, conv_w[o_], conv_b[o_], dt_bias[o_], a_log[o_], d_skip[o_], b, s, ch)
            conv_p.append(pf.reshape(b, s, ODD_PAD)[:, s - (CONV_W - 1):, O_XS:O_XS + CONV_DIM])
            ssm_p.append(st)
            xp = _mlp(_odd_merge(xp, y, pf, ssm_norm[o_], wo, tm), norm_mlp[layer], wup, wdn, norm_final, tm, final)
            y_s, hn, st_s = _sample_ssd(sf[:, O_XS:O_XS + CONV_DIM].reshape(bs, 1, CONV_DIM),
                                        sf[:, O_DT:O_DT + LANE].reshape(bs, 1, LANE), state_conv[o_],
                                        conv_w[o_], conv_b[o_], dt_bias[o_], a_log[o_], d_skip[o_], state_ssm, o_)
            conv_s.append(hn)
            ssm_s.append(st_s)
            xs = _mlp(_odd_merge(xs, y_s.reshape(bs, D_INNER), sf, ssm_norm[o_], wo, bs),
                      norm_mlp[layer], wup, wdn, norm_final, bs, final)

    return (xp.reshape(b, s, d), xs.reshape(bs, 1, d), jnp.stack(k_p), jnp.stack(v_p), jnp.stack(ik_p),
            jnp.stack(k_s), jnp.stack(v_s), jnp.stack(ik_s), jnp.stack(ret_p), jnp.stack(ret_s),
            jnp.stack(conv_p), jnp.stack(conv_s), jnp.stack(ssm_p), jnp.stack(ssm_s))
```

```python
import functools
import math

import jax
import jax.numpy as jnp
import numpy as np
from jax import lax
from jax.experimental import pallas as pl
from jax.experimental.pallas import tpu as pltpu

F32 = jnp.float32
BF16 = jnp.bfloat16
I32 = jnp.int32

D_MODEL = 1024
DEPTH = 4
PAGE_SIZE = 128
DH_A = 64
H_A = (D_MODEL // 2) // DH_A
D_ATT = H_A * DH_A
H_IDX = 8
D_IDX = 64
TOPK_MAX = 256
NUM_BUCKETS = 32
MAX_DISTANCE = 128
H_R = 4
DV_R = (D_MODEL // 2) // H_R
DK_R = DV_R // 2
D_RK = H_R * DK_R
D_RV = H_R * DV_R
ROPE_BASE = 10000.0
D_INNER = 2 * D_MODEL
P_SSM = 64
NH_SSM = D_INNER // P_SSM
G_SSM = 8
HPG = NH_SSM // G_SSM
N_STATE = 128
CONV_W = 4
D_BC = G_SSM * N_STATE
CONV_DIM = D_INNER + 2 * D_BC
D_FF = 4 * D_MODEL
EPS = 1e-6

EVEN_SIZES = (D_ATT, D_ATT, D_ATT, H_IDX * D_IDX, D_IDX, H_IDX, D_RK, D_RK, D_RV, D_RV)
ODD_IN = D_INNER + CONV_DIM + NH_SSM

E_Q, E_K, E_V, E_IQ = 0, 512, 1024, 1536
E_RV, E_RG, E_RQ, E_RK, E_IK = 2048, 2560, 3072, 3328, 3584
E_IW = D_IDX
EVEN_PAD = 3712
O_Z, O_XS, O_BC, O_DT = 0, 2048, 4096, 6144
ODD_PAD = 6272

LANE = 128
VMEM_LIMIT = 56 * 1024 * 1024
NEG_BIG = -1e30
INT_MIN = -(2 ** 31)
KEY_NEG_INF = int(np.array(0xFF800000, np.uint32).view(np.int32)) ^ 0x7FFFFFFF

NT_DIMS = (((1,), (1,)), ((), ()))
TN_DIMS = (((0,), (0,)), ((), ()))


def _cparams(*sem):
    return pltpu.CompilerParams(dimension_semantics=sem, vmem_limit_bytes=VMEM_LIMIT)


def _const_spec(shape):
    nd = len(shape)
    return pl.BlockSpec(shape, lambda *_: (0,) * nd, pipeline_mode=pl.Buffered(1))


def _rms(x, g):
    return x * lax.rsqrt(jnp.mean(x * x, axis=-1, keepdims=True) + EPS) * g


def _silu(x):
    return x / (1.0 + jnp.exp(-x))


def _softplus(x):
    return jnp.maximum(x, 0.0) + jnp.log(1.0 + jnp.exp(-jnp.abs(x)))


def _rms_matmul_kernel(x_ref, g_ref, w_ref, of_ref, *ob_refs, bf_slices, n_chunk):
    h = _rms(x_ref[...], g_ref[...]).astype(BF16)
    n = w_ref.shape[1]
    for c0 in range(0, n, n_chunk):
        c1 = min(c0 + n_chunk, n)
        of_ref[:, c0:c1] = jnp.dot(h, w_ref[:, c0:c1], preferred_element_type=F32)
    for (start, width), r in zip(bf_slices, ob_refs):
        r[...] = of_ref[:, start:start + width].astype(BF16)


def _rms_matmul(x, g, w, tm, bf_slices=()):
    t, d = x.shape
    n = w.shape[1]
    out_shape = [jax.ShapeDtypeStruct((t, n), F32)]
    out_specs = [pl.BlockSpec((tm, n), lambda i: (i, 0))]
    for _, width in bf_slices:
        out_shape.append(jax.ShapeDtypeStruct((t, width), BF16))
        out_specs.append(pl.BlockSpec((tm, width), lambda i: (i, 0)))
    return pl.pallas_call(
        functools.partial(_rms_matmul_kernel, bf_slices=tuple(bf_slices), n_chunk=512),
        grid=(t // tm,),
        in_specs=[pl.BlockSpec((tm, d), lambda i: (i, 0)), _const_spec((1, d)), _const_spec((d, n))],
        out_specs=out_specs,
        out_shape=out_shape,
        compiler_params=_cparams("parallel"),
        name="rms_matmul",
    )(x, g.reshape(1, d), w)


def _mlp_tail(x, gm_ref, wup_ref, wdn_ref, f_chunk=1024):
    h = _rms(x, gm_ref[...]).astype(BF16)
    acc = x
    for f in range(0, D_FF, f_chunk):
        u = jnp.dot(h, wup_ref[:, f:f + f_chunk], preferred_element_type=F32)
        u = jnp.maximum(u, 0.0)
        u = (u * u).astype(BF16)
        acc = acc + jnp.dot(u, wdn_ref[f:f + f_chunk, :], preferred_element_type=F32)
    return acc


def _even_merge_kernel(x_ref, att_ref, o_ref, rg_ref, wo_ref, out_ref):
    o = o_ref[...]
    parts = []
    for h in range(H_R):
        oh = o[:, h * DV_R:(h + 1) * DV_R]
        d = oh - jnp.mean(oh, axis=-1, keepdims=True)
        parts.append(d * lax.rsqrt(jnp.mean(d * d, axis=-1, keepdims=True) + EPS))
    ret = _silu(rg_ref[...]) * jnp.concatenate(parts, axis=1)
    x = x_ref[...]
    x = x + jnp.dot(att_ref[...].astype(BF16), wo_ref[0:D_ATT, :], preferred_element_type=F32)
    out_ref[...] = x + jnp.dot(ret.astype(BF16), wo_ref[D_ATT:D_ATT + D_RV, :], preferred_element_type=F32)


def _odd_merge_kernel(x_ref, y_ref, z_ref, nw_ref, wo_ref, out_ref):
    g = y_ref[...] * _silu(z_ref[...])
    gw = D_INNER // G_SSM
    parts = []
    for k in range(G_SSM):
        gk = g[:, k * gw:(k + 1) * gw]
        parts.append(gk * lax.rsqrt(jnp.mean(gk * gk, axis=-1, keepdims=True) + EPS))
    gn = (jnp.concatenate(parts, axis=1) * nw_ref[...]).astype(BF16)
    out_ref[...] = x_ref[...] + jnp.dot(gn, wo_ref[...], preferred_element_type=F32)


def _mlp_kernel(x_ref, gm_ref, wup_ref, wdn_ref, gf_ref, out_ref, *, final):
    x = _mlp_tail(x_ref[...], gm_ref, wup_ref, wdn_ref)
    if final:
        x = _rms(x, gf_ref[...])
    out_ref[...] = x


def _mlp(x, gm, wup, wdn, gf, tm, final):
    t, d = x.shape
    row = lambda i: (i, 0)
    return pl.pallas_call(
        functools.partial(_mlp_kernel, final=final),
        grid=(t // tm,),
        in_specs=[pl.BlockSpec((tm, d), row), _const_spec((1, d)), _const_spec(wup.shape), _const_spec(wdn.shape),
                  _const_spec((1, d))],
        out_specs=pl.BlockSpec((tm, d), row),
        out_shape=jax.ShapeDtypeStruct((t, d), F32),
        compiler_params=_cparams("parallel"),
        name="mlp",
    )(x, gm.reshape(1, d), wup, wdn, gf.reshape(1, d))


def _even_merge(x, att, o, pf, wo, tm):
    t, d = x.shape
    row = lambda i: (i, 0)
    return pl.pallas_call(
        _even_merge_kernel,
        grid=(t // tm,),
        in_specs=[pl.BlockSpec((tm, d), row), pl.BlockSpec((tm, D_ATT), row), pl.BlockSpec((tm, D_RV), row),
                  pl.BlockSpec((tm, D_RV), lambda i: (i, E_RG // D_RV)), _const_spec(wo.shape)],
        out_specs=pl.BlockSpec((tm, d), row),
        out_shape=jax.ShapeDtypeStruct((t, d), F32),
        compiler_params=_cparams("parallel"),
        name="even_merge",
    )(x, att, o, pf, wo)


def _odd_merge(x, y, pf, nw, wo, tm):
    t, d = x.shape
    row = lambda i: (i, 0)
    return pl.pallas_call(
        _odd_merge_kernel,
        grid=(t // tm,),
        in_specs=[pl.BlockSpec((tm, d), row), pl.BlockSpec((tm, D_INNER), row),
                  pl.BlockSpec((tm, D_INNER), lambda i: (i, O_Z // D_INNER)),
                  _const_spec((1, D_INNER)), _const_spec(wo.shape)],
        out_specs=pl.BlockSpec((tm, d), row),
        out_shape=jax.ShapeDtypeStruct((t, d), F32),
        compiler_params=_cparams("parallel"),
        name="odd_merge",
    )(x, y, pf, nw.reshape(1, D_INNER), wo)


FOLD = 64


def _fold_rows(x, op):
    n, w = x.shape
    if n > FOLD:
        x = op(x.reshape(n // FOLD, FOLD, w), axis=0)
    if x.shape[0] > 8:
        x = op(x.reshape(x.shape[0] // 8, 8, w), axis=0)
    return op(x, axis=0, keepdims=True)


def _score_key(score):
    bits = lax.bitcast_convert_type(score, I32)
    return bits ^ ((bits >> 31) & 0x7FFFFFFF)


def _kth_largest(count_ge, rows, k_sel):
    def bit_step(b, t):
        cand = t + lax.shift_left(jnp.int32(1), 31 - b)
        return jnp.where(count_ge(cand) >= k_sel, cand, t)
    return lax.fori_loop(0, 32, bit_step, jnp.full((rows, 1), INT_MIN, I32))


def _tie_limit(count_tie_lt, need, rows, n_bits):
    def bit_step(b, x):
        cand = x + lax.shift_left(jnp.int32(1), n_bits - 1 - b)
        return jnp.where(count_tie_lt(cand) < need, cand, x)
    return lax.fori_loop(0, n_bits, bit_step, jnp.zeros((rows, 1), I32))


def _attn_prompt_kernel(q_ref, iq_ref, iw_ref, k_ref, vt_ref, ik_ref, tb_ref, b31_ref, o_ref,
                        sk_ref, x_ref, m_ref, l_ref, acc_ref, s_ref, p_ref, *, tq, tk, k_sel, idx_bits):
    i = pl.program_id(1)
    r = tk // tq
    nkc = (i + r) // r
    nfar = jnp.maximum(i - 1, 0) // r
    qidx = lax.broadcasted_iota(I32, (tk, tq), 1) + i * tq
    krow = lax.broadcasted_iota(I32, (tk, tq), 0)
    iw = iw_ref[...] * ((D_IDX ** -0.5) * (H_IDX ** -0.5))

    def key_fold(m):
        return jnp.sum(m.reshape(tk // FOLD, FOLD, tq), axis=0)

    def score_chunk(c, carry):
        ikc = ik_ref[c][:, 0:D_IDX]
        acc = jnp.zeros((tk, tq), F32)
        for h in range(H_IDX):
            s = lax.dot_general(ikc, iq_ref[:, h * D_IDX:(h + 1) * D_IDX], NT_DIMS, preferred_element_type=F32)
            acc = acc + iw[h:h + 1, :] * jnp.maximum(s, 0.0)
        sk_ref[c] = _score_key(jnp.where(krow + c * tk <= qidx, acc, -jnp.inf))
        return carry
    lax.fori_loop(0, nkc, score_chunk, 0)

    def count_ge(cand):
        def body(c, part):
            return part + key_fold(jnp.where(sk_ref[c] >= cand, 1.0, 0.0))
        part = lax.fori_loop(0, nkc, body, jnp.zeros((FOLD, tq), F32))
        return _fold_rows(part, jnp.sum)

    def kth_step(b, t):
        cand = t + lax.shift_left(jnp.int32(1), 31 - b)
        return jnp.where(count_ge(cand) >= float(k_sel), cand, t)
    thr = lax.fori_loop(0, 32, kth_step, jnp.full((1, tq), INT_MIN, I32))
    n_gt = count_ge(thr + 1)
    n_ge = count_ge(thr)
    need = float(k_sel) - n_gt
    excess = jnp.logical_and(n_ge - n_gt > need, thr > KEY_NEG_INF)
    x_ref[...] = jnp.full((1, tq), 2 ** idx_bits, I32)

    @pl.when(jnp.max(jnp.where(excess, 1.0, 0.0)) > 0.0)
    def _():
        def count_tie_lt(cand):
            def body(c, part):
                hit = jnp.logical_and(sk_ref[c] == thr, krow + c * tk < cand)
                return part + key_fold(jnp.where(hit, 1.0, 0.0))
            part = lax.fori_loop(0, nkc, body, jnp.zeros((FOLD, tq), F32))
            return _fold_rows(part, jnp.sum)

        def tie_step(b, x):
            cand = x + lax.shift_left(jnp.int32(1), idx_bits - 1 - b)
            return jnp.where(count_tie_lt(cand) < need, cand, x)
        lim = lax.fori_loop(0, idx_bits, tie_step, jnp.zeros((1, tq), I32))
        x_ref[...] = jnp.where(excess, lim, 2 ** idx_bits)

    xlim = x_ref[...]

    m_ref[...] = jnp.full(m_ref.shape, NEG_BIG, F32)
    l_ref[...] = jnp.zeros(l_ref.shape, F32)
    acc_ref[...] = jnp.zeros(acc_ref.shape, F32)

    def attend_chunk(c, near):
        key = sk_ref[c]
        kidx = krow + c * tk
        sel = jnp.logical_or(key > thr, jnp.logical_and(key == thr, kidx <= xlim))
        if near:
            sel = jnp.logical_and(sel, kidx <= qidx)
        amask = jnp.where(sel, 0.0, NEG_BIG)
        kc = k_ref[c]
        vtc = vt_ref[c]
        def logits(h):
            hs = slice(h * DH_A, (h + 1) * DH_A)
            qh = q_ref[:, hs] * (DH_A ** -0.5)
            return lax.dot_general(kc[:, hs], qh, NT_DIMS, preferred_element_type=F32)

        for h in range(H_A):
            s_ref[h] = logits(h)
        alphas = []
        for h in range(H_A):
            b_far = b31_ref[0:1, h:h + 1]
            s = s_ref[h] + amask
            m_old = m_ref[h:h + 1, :]
            if near:
                tiles = []
                for t in range(r):
                    jt = c * r + t
                    tiles.append(jnp.where(jt == i, tb_ref[h, tq:2 * tq, :],
                                           jnp.where(jt == i - 1, tb_ref[h, 0:tq, :], b_far)))
                s = s + (tiles[0] if r == 1 else jnp.concatenate(tiles, axis=0))
                m_new = jnp.maximum(m_old, _fold_rows(s, jnp.max))
                shift = -m_new
            else:
                m_new = jnp.maximum(m_old, _fold_rows(s, jnp.max) + b_far)
                shift = b_far - m_new
            alpha = jnp.exp(m_old - m_new)
            p = jnp.exp(s + shift)
            l_ref[h:h + 1, :] = alpha * l_ref[h:h + 1, :] + _fold_rows(p, jnp.sum)
            m_ref[h:h + 1, :] = m_new
            p_ref[h] = p.astype(BF16)
            alphas.append(alpha)
        for h in range(H_A):
            hs = slice(h * DH_A, (h + 1) * DH_A)
            pv = jnp.dot(vtc[hs, :], p_ref[h], preferred_element_type=F32)
            acc_ref[hs, :] = alphas[h] * acc_ref[hs, :] + pv

    def far_body(c, carry):
        attend_chunk(c, False)
        return carry

    def near_body(c, carry):
        attend_chunk(c, True)
        return carry

    lax.fori_loop(0, nfar, far_body, 0)
    lax.fori_loop(nfar, nkc, near_body, 0)

    for h in range(H_A):
        hs = slice(h * DH_A, (h + 1) * DH_A)
        acc_ref[hs, :] = acc_ref[hs, :] / l_ref[h:h + 1, :]
    o_ref[...] = jnp.transpose(acc_ref[...])


def _attn_prompt(pf, pbq, pbi, tb_t, b31, b, s, tq, tk):
    nq, nkc = s // tq, s // tk
    k_sel = min(TOPK_MAX, s // 4)
    idx_bits = int(math.log2(s))
    assert 2 ** idx_bits == s and tk % tq == 0 and tq % LANE == 0
    pbq4 = pbq.reshape(b, nkc, tk, pbq.shape[1])
    pbi4 = pbi.reshape(b, nkc, tk, LANE)
    vt4 = jnp.swapaxes(pbq4[..., E_V:E_V + D_ATT], 2, 3)
    iw_t = jnp.transpose(pf[:, E_IK + E_IW:E_IK + E_IW + H_IDX])
    return pl.pallas_call(
        functools.partial(_attn_prompt_kernel, tq=tq, tk=tk, k_sel=k_sel, idx_bits=idx_bits),
        grid=(b, nq),
        in_specs=[
            pl.BlockSpec((tq, D_ATT), lambda bb, i: (bb * nq + i, E_Q // D_ATT)),
            pl.BlockSpec((tq, D_ATT), lambda bb, i: (bb * nq + i, E_IQ // D_ATT)),
            pl.BlockSpec((H_IDX, tq), lambda bb, i: (0, bb * nq + i)),
            pl.BlockSpec((None, nkc, tk, D_ATT), lambda bb, i: (bb, 0, 0, E_K // D_ATT), pipeline_mode=pl.Buffered(1)),
            pl.BlockSpec((None, nkc, D_ATT, tk), lambda bb, i: (bb, 0, 0, 0), pipeline_mode=pl.Buffered(1)),
            pl.BlockSpec((None, nkc, tk, LANE), lambda bb, i: (bb, 0, 0, 0), pipeline_mode=pl.Buffered(1)),
            _const_spec(tb_t.shape), _const_spec(b31.shape),
        ],
        out_specs=pl.BlockSpec((tq, D_ATT), lambda bb, i: (bb * nq + i, 0)),
        out_shape=jax.ShapeDtypeStruct((b * s, D_ATT), F32),
        scratch_shapes=[pltpu.VMEM((nkc, tk, tq), I32), pltpu.VMEM((1, tq), I32),
                        pltpu.VMEM((H_A, tq), F32), pltpu.VMEM((H_A, tq), F32),
                        pltpu.VMEM((D_ATT, tq), F32),
                        pltpu.VMEM((H_A, tk, tq), F32), pltpu.VMEM((H_A, tk, tq), BF16)],
        compiler_params=_cparams("parallel", "arbitrary"),
        name="attn_prompt",
    )(pbq, pbq, iw_t, pbq4, vt4, pbi4, tb_t, b31)


def _rotate_half(x, cos_t, sin_t):
    half = DK_R // 2
    width = x.shape[1]
    lane = lax.broadcasted_iota(I32, x.shape, 1)
    partner = jnp.where(lane % DK_R < half, pltpu.roll(x, width - half, 1), pltpu.roll(x, half, 1))
    return x * cos_t + partner * sin_t


def _ret_prompt_kernel(rq_ref, rk_ref, rv_ref, cos_ref, sin_ref, dm_ref, cd_ref, kd_ref, gl_ref,
                       o_ref, st_ref, s_ref):
    c = pl.program_id(1)

    @pl.when(c == 0)
    def _():
        s_ref[...] = jnp.zeros(s_ref.shape, F32)

    cos_t, sin_t = cos_ref[...], sin_ref[...]
    rq = _rotate_half(rq_ref[...], cos_t, sin_t)
    rk = _rotate_half(rk_ref[...], cos_t, sin_t) * (DK_R ** -0.5)
    rv = rv_ref[...]
    for h in range(H_R):
        qh = rq[:, h * DK_R:(h + 1) * DK_R].astype(BF16)
        kh = rk[:, h * DK_R:(h + 1) * DK_R]
        vh = rv[:, h * DV_R:(h + 1) * DV_R].astype(BF16)
        st = s_ref[h]
        scores = lax.dot_general(qh, kh.astype(BF16), NT_DIMS, preferred_element_type=F32) * dm_ref[h]
        inner = jnp.dot(scores.astype(BF16), vh, preferred_element_type=F32)
        cross = jnp.dot(qh, st.astype(BF16), preferred_element_type=F32) * cd_ref[:, h:h + 1]
        o_ref[:, h * DV_R:(h + 1) * DV_R] = inner + cross
        kdec = (kh * kd_ref[:, h:h + 1]).astype(BF16)
        upd = lax.dot_general(kdec, vh, TN_DIMS, preferred_element_type=F32)
        s_ref[h] = st * gl_ref[0:1, h:h + 1] + upd

    @pl.when(c == pl.num_programs(1) - 1)
    def _():
        st_ref[...] = s_ref[...]


def _ret_tables(ch):
    lg = jnp.log(1.0 - jnp.exp2(-5.0 - jnp.arange(H_R, dtype=F32)))
    n = jnp.arange(ch, dtype=F32)
    diff = n[:, None] - n[None, :]
    causal = diff >= 0
    dm = jnp.where(causal[None], jnp.exp(jnp.where(causal, diff, 0.0)[None] * lg[:, None, None]), 0.0)
    cd = jnp.exp((n[:, None] + 1.0) * lg[None, :])
    kd = jnp.exp((ch - 1.0 - n)[:, None] * lg[None, :])
    gl = jnp.exp(ch * lg)[None, :]
    pad = lambda a: jnp.pad(a, ((0, 0), (0, LANE - H_R)))
    return dm, pad(cd), pad(kd), pad(gl)


def _rope_tables(pos):
    half = DK_R // 2
    inv = ROPE_BASE ** (-jnp.arange(half, dtype=F32) / half)
    ang = pos.astype(F32)[:, None] * inv[None, :]
    cos, sin = jnp.cos(ang), jnp.sin(ang)
    cos_t = jnp.tile(jnp.concatenate([cos, cos], axis=1), (1, H_R))
    sin_t = jnp.tile(jnp.concatenate([-sin, sin], axis=1), (1, H_R))
    return cos_t, sin_t


def _ret_prompt(pf, cos_t, sin_t, b, s, ch):
    nc = s // ch
    dm, cd, kd, gl = _ret_tables(ch)
    tok = lambda col: (lambda bb, c: (bb * nc + c, col))
    return pl.pallas_call(
        _ret_prompt_kernel,
        grid=(b, nc),
        in_specs=[pl.BlockSpec((ch, D_RK), tok(E_RQ // D_RK)), pl.BlockSpec((ch, D_RK), tok(E_RK // D_RK)),
                  pl.BlockSpec((ch, D_RV), tok(E_RV // D_RV)),
                  pl.BlockSpec((ch, D_RK), lambda bb, c: (c, 0)), pl.BlockSpec((ch, D_RK), lambda bb, c: (c, 0)),
                  _const_spec(dm.shape), _const_spec(cd.shape), _const_spec(kd.shape), _const_spec(gl.shape)],
        out_specs=[pl.BlockSpec((ch, D_RV), lambda bb, c: (bb * nc + c, 0)),
                   pl.BlockSpec((None, H_R, DK_R, DV_R), lambda bb, c: (bb, 0, 0, 0))],
        out_shape=[jax.ShapeDtypeStruct((b * s, D_RV), F32), jax.ShapeDtypeStruct((b, H_R, DK_R, DV_R), F32)],
        scratch_shapes=[pltpu.VMEM((H_R, DK_R, DV_R), F32)],
        compiler_params=_cparams("parallel", "arbitrary"),
        name="retention_prompt",
    )(pf, pf, pf, cos_t, sin_t, dm, cd, kd, gl)


def _shift_rows(x, prev8, j):
    rolled = pltpu.roll(x, j, 0)
    rid = lax.broadcasted_iota(I32, prev8.shape, 0)
    head = jnp.where(rid < j, pltpu.roll(prev8, j, 0), rolled[0:8])
    return jnp.concatenate([head, rolled[8:]], axis=0)


def _conv_silu(x, prev8, w_ref, b_ref, col0, width):
    out = b_ref[0:1, col0:col0 + width] + x * w_ref[CONV_W - 1:CONV_W, col0:col0 + width]
    for j in range(1, CONV_W):
        out = out + _shift_rows(x, prev8, j) * w_ref[CONV_W - 1 - j:CONV_W - j, col0:col0 + width]
    return _silu(out)


def _split_pack(x, pieces):
    lane = lax.broadcasted_iota(I32, x.shape, 1)
    rest = jnp.where(lane < NH_SSM, x, 0.0)
    packed = jnp.zeros(x.shape, F32)
    for k in range(pieces):
        piece = rest.astype(BF16).astype(F32)
        rest = rest - piece
        packed = packed + (piece if k == 0 else pltpu.roll(piece, k * NH_SSM, 1))
    return packed.astype(BF16)


def _ssd_prompt_kernel(xs_ref, bc_ref, dt_ref, cw_ref, cb_ref, dtb_ref, alog_ref, dskx_ref, tri_ref, ex_ref, sw_ref,
                       y_ref, st_ref, s_ref, pxs_ref, pbc_ref, w_ref, *, ch):
    c = pl.program_id(1)

    @pl.when(c == 0)
    def _():
        s_ref[...] = jnp.zeros(s_ref.shape, F32)
        pxs_ref[...] = jnp.zeros(pxs_ref.shape, F32)
        pbc_ref[...] = jnp.zeros(pbc_ref.shape, F32)

    xs_raw = xs_ref[...]
    bc_raw = bc_ref[...]
    xs = _conv_silu(xs_raw, pxs_ref[...], cw_ref, cb_ref, 0, D_INNER)
    bc = _conv_silu(bc_raw, pbc_ref[...], cw_ref, cb_ref, D_INNER, 2 * D_BC).astype(BF16)
    pxs_ref[...] = xs_raw[ch - 8:ch]
    pbc_ref[...] = bc_raw[ch - 8:ch]

    dt = _softplus(dt_ref[...] + dtb_ref[...])
    a = -jnp.exp(alog_ref[...])
    acs = jnp.dot(tri_ref[...], dt * a, preferred_element_type=F32, precision=lax.Precision.HIGHEST)
    acs_t = jnp.transpose(acs)
    dec_end = jnp.exp(acs[ch - 1:ch, :] - acs)

    def expand(x):
        return jnp.dot(_split_pack(x, 2), ex_ref[...], preferred_element_type=F32)
    dt_x = expand(dt)
    e_x = expand(jnp.exp(acs))
    xdt = (xs * dt_x).astype(BF16)
    xd = (xs * expand(dt * dec_end)).astype(BF16)

    seg_t = jnp.dot(_split_pack(acs, 3), sw_ref[...], preferred_element_type=F32)
    tri = tri_ref[...] > 0.5
    for g in range(G_SSM):
        bm = bc[:, g * N_STATE:(g + 1) * N_STATE]
        cm = bc[:, D_BC + g * N_STATE:D_BC + (g + 1) * N_STATE]
        cb = lax.dot_general(cm, bm, NT_DIMS, preferred_element_type=F32)
        for hh in range(HPG):
            h = g * HPG + hh
            seg = seg_t[:, h * ch:(h + 1) * ch] - acs_t[h:h + 1, :]
            w_ref[h] = (cb * jnp.exp(jnp.where(tri, seg, -jnp.inf))).astype(BF16)

    e_last_x = e_x[ch - 1:ch, :]
    gw = HPG * P_SSM
    for g in range(G_SSM):
        gs = slice(g * gw, (g + 1) * gw)
        bm = bc[:, g * N_STATE:(g + 1) * N_STATE]
        cm = bc[:, D_BC + g * N_STATE:D_BC + (g + 1) * N_STATE]
        st = s_ref[g]
        y_off = jnp.dot(cm, st.astype(BF16), preferred_element_type=F32)
        y_diag = jnp.concatenate(
            [jnp.dot(w_ref[g * HPG + hh], xdt[:, (g * HPG + hh) * P_SSM:(g * HPG + hh + 1) * P_SSM],
                     preferred_element_type=F32) for hh in range(HPG)], axis=1)
        y_ref[:, gs] = y_diag + y_off * e_x[:, gs] + dskx_ref[:, gs] * xs[:, gs]
        upd = lax.dot_general(bm, xd[:, gs], TN_DIMS, preferred_element_type=F32)
        s_ref[g] = st * e_last_x[:, gs] + upd

    @pl.when(c == pl.num_programs(1) - 1)
    def _():
        for g in range(G_SSM):
            st_ref[g * HPG:(g + 1) * HPG] = jnp.transpose(s_ref[g]).reshape(HPG, P_SSM, N_STATE)


def _pad_lanes(v):
    return jnp.pad(v.reshape(1, -1), ((0, 0), (0, LANE - v.shape[-1])))


def _ssd_prompt(pf, cw, cb, dtb, alog, dsk, b, s, ch):
    nc = s // ch
    tri = jnp.tril(jnp.ones((ch, ch), F32))
    lane_head = jnp.arange(LANE) % NH_SSM
    ex = ((lane_head[:, None] == jnp.arange(D_INNER)[None, :] // P_SSM)
          & (jnp.arange(LANE)[:, None] < 2 * NH_SSM)).astype(BF16)
    sw = ((lane_head[:, None] == jnp.arange(NH_SSM * ch)[None, :] // ch)
          & (jnp.arange(LANE)[:, None] < 3 * NH_SSM)).astype(BF16)
    dskx = jnp.repeat(dsk, P_SSM).reshape(1, D_INNER)
    tok = lambda col: (lambda bb, c: (bb * nc + c, col))
    return pl.pallas_call(
        functools.partial(_ssd_prompt_kernel, ch=ch),
        grid=(b, nc),
        in_specs=[pl.BlockSpec((ch, D_INNER), tok(O_XS // D_INNER)), pl.BlockSpec((ch, 2 * D_BC), tok(O_BC // (2 * D_BC))),
                  pl.BlockSpec((ch, LANE), tok(O_DT // LANE)),
                  _const_spec(cw.shape), _const_spec((1, CONV_DIM)), _const_spec((1, LANE)), _const_spec((1, LANE)),
                  _const_spec((1, D_INNER)), _const_spec((ch, ch)), _const_spec(ex.shape), _const_spec(sw.shape)],
        out_specs=[pl.BlockSpec((ch, D_INNER), lambda bb, c: (bb * nc + c, 0)),
                   pl.BlockSpec((None, NH_SSM, P_SSM, N_STATE), lambda bb, c: (bb, 0, 0, 0))],
        out_shape=[jax.ShapeDtypeStruct((b * s, D_INNER), F32),
                   jax.ShapeDtypeStruct((b, NH_SSM, P_SSM, N_STATE), F32)],
        scratch_shapes=[pltpu.VMEM((G_SSM, N_STATE, HPG * P_SSM), F32), pltpu.VMEM((8, D_INNER), F32),
                        pltpu.VMEM((8, 2 * D_BC), F32), pltpu.VMEM((NH_SSM, ch, ch), BF16)],
        compiler_params=_cparams("parallel", "arbitrary"),
        name="ssd_prompt",
    )(pf, pf, pf, cw, cb.reshape(1, CONV_DIM), _pad_lanes(dtb), _pad_lanes(alog), dskx, tri, ex, sw)


def _row_to_col(row, n):
    eye = lax.broadcasted_iota(I32, (n, n), 0) == lax.broadcasted_iota(I32, (n, n), 1)
    return jnp.sum(jnp.where(eye, jnp.broadcast_to(row, (n, n)), 0.0), axis=1, keepdims=True)


def _col_to_row(col, n):
    eye = lax.broadcasted_iota(I32, (n, n), 0) == lax.broadcasted_iota(I32, (n, n), 1)
    return jnp.sum(jnp.where(eye, jnp.broadcast_to(col, (n, n)), 0.0), axis=0, keepdims=True)


def _sample_scores_kernel(pt_ref, iq_ref, iw_ref, *refs, pg):
    page_refs, o_ref = refs[:pg], refs[pg]
    iq = iq_ref[...]
    iw = iw_ref[...] * ((D_IDX ** -0.5) * (H_IDX ** -0.5))
    for u in range(pg):
        s = jnp.dot(iq, page_refs[u][...].astype(BF16), preferred_element_type=F32)
        o_ref[u:u + 1, :] = jnp.sum(iw * jnp.maximum(s, 0.0), axis=0, keepdims=True)


def _sample_scores(iq, iw, cik_t, e, page_table, pg):
    bs, n_pages = page_table.shape
    page_spec = lambda u: pl.BlockSpec((None, None, D_IDX, PAGE_SIZE),
                                       lambda b, j, pt: (e, pt[b, j * pg + u], 0, 0))
    grid_spec = pltpu.PrefetchScalarGridSpec(
        num_scalar_prefetch=1, grid=(bs, n_pages // pg),
        in_specs=[pl.BlockSpec((None, H_IDX, D_IDX), lambda b, j, pt: (b, 0, 0)),
                  pl.BlockSpec((None, H_IDX, 1), lambda b, j, pt: (b, 0, 0))] + [page_spec(u) for u in range(pg)],
        out_specs=pl.BlockSpec((None, pg, PAGE_SIZE), lambda b, j, pt: (b, j, 0)))
    return pl.pallas_call(
        functools.partial(_sample_scores_kernel, pg=pg),
        grid_spec=grid_spec,
        out_shape=jax.ShapeDtypeStruct((bs, n_pages, PAGE_SIZE), F32),
        compiler_params=_cparams("parallel", "arbitrary"),
        name="sample_scores",
    )(page_table, iq, iw, *([cik_t] * pg))


def _sample_attn_kernel(pt_ref, sc_ref, q_ref, kn_ref, vn_ref, iqf_ref, ikn_ref, iwf_ref, tbs_ref, b31_ref, b0_ref,
                        *refs, pg, n_pages, k_sel, idx_bits):
    k_refs, v_refs = refs[:pg], refs[pg:2 * pg]
    o_ref, sel_ref, m_ref, l_ref, acc_ref = refs[2 * pg:]
    j = pl.program_id(1)
    qcol = q_ref[...] * (DH_A ** -0.5)

    def head_sum(x):
        return jnp.concatenate([jnp.sum(x[h * DH_A:(h + 1) * DH_A], axis=0, keepdims=True) for h in range(H_A)],
                               axis=0)

    def head_expand(x, w):
        return jnp.concatenate([jnp.broadcast_to(x[h:h + 1, :], (DH_A, w)) for h in range(H_A)], axis=0)

    @pl.when(j == 0)
    def _():
        iqf = iqf_ref[...]
        s_new = jnp.sum(iqf * ikn_ref[...], axis=1, keepdims=True)
        w = iwf_ref[...] * ((D_IDX ** -0.5) * (H_IDX ** -0.5))
        sc_new = jnp.sum(w * jnp.maximum(s_new, 0.0), axis=0, keepdims=True)
        key_new = _score_key(sc_new)
        keys = _score_key(sc_ref[...])
        kidx = (lax.broadcasted_iota(I32, keys.shape, 0) * PAGE_SIZE + lax.broadcasted_iota(I32, keys.shape, 1))
        idx_new = n_pages * PAGE_SIZE

        def total(mask, extra):
            part = jnp.sum(jnp.where(mask, 1.0, 0.0), axis=0, keepdims=True)
            return jnp.sum(part, axis=1, keepdims=True) + jnp.where(extra, 1.0, 0.0)

        def count_ge(cand):
            return total(keys >= cand, key_new >= cand)

        thr = _kth_largest(count_ge, 1, float(k_sel))
        n_gt = count_ge(thr + 1)
        need = float(k_sel) - n_gt

        def count_tie_lt(cand):
            return total(jnp.logical_and(keys == thr, kidx < cand), jnp.logical_and(key_new == thr, idx_new < cand))

        lim = _tie_limit(count_tie_lt, need, 1, idx_bits)
        sel = jnp.logical_or(keys > thr, jnp.logical_and(keys == thr, kidx <= lim))
        sel_ref[...] = jnp.where(sel, 1.0, 0.0)
        sel_new = jnp.logical_or(key_new > thr, jnp.logical_and(key_new == thr, idx_new <= lim))
        lg_new = head_sum(qcol * kn_ref[...]) + b0_ref[...]
        m_ref[...] = jnp.broadcast_to(jnp.where(sel_new, lg_new, NEG_BIG), m_ref.shape)
        l0 = jnp.broadcast_to(jnp.where(sel_new, 1.0, 0.0), (H_A, 1))
        l_ref[...] = jnp.broadcast_to(l0, l_ref.shape)
        first_lane = lax.broadcasted_iota(I32, (D_ATT, PAGE_SIZE), 1) == 0
        acc_ref[...] = jnp.where(first_lane, head_expand(l0, PAGE_SIZE) * vn_ref[...], 0.0)

    qb = jnp.broadcast_to(qcol, (D_ATT, PAGE_SIZE))
    logits = []
    for u in range(pg):
        p_idx = j * pg + u
        s = head_sum(k_refs[u][...] * qb)
        s = s + jnp.where(p_idx == n_pages - 1, tbs_ref[...], b31_ref[...])
        logits.append(jnp.where(sel_ref[pl.ds(p_idx, 1), :] > 0.5, s, NEG_BIG))
    m_old = m_ref[:, 0:1]
    m_new = m_old
    for s in logits:
        m_new = jnp.maximum(m_new, jnp.max(s, axis=1, keepdims=True))
    alpha = jnp.exp(m_old - m_new)
    l_new = alpha * l_ref[:, 0:1]
    acc = head_expand(alpha, PAGE_SIZE) * acc_ref[...]
    for u, s in enumerate(logits):
        p = jnp.exp(s - m_new)
        l_new = l_new + jnp.sum(p, axis=1, keepdims=True)
        acc = acc + v_refs[u][...] * head_expand(p, PAGE_SIZE)
    acc_ref[...] = acc
    m_ref[...] = jnp.broadcast_to(m_new, m_ref.shape)
    l_ref[...] = jnp.broadcast_to(l_new, l_ref.shape)

    @pl.when(j == pl.num_programs(1) - 1)
    def _():
        o_ref[...] = jnp.sum(acc_ref[...], axis=1, keepdims=True) / head_expand(l_ref[:, 0:1], 1)


def _sample_attn(scores, q, k_new, v_new, iqf, ik_new, iwf, ck_t, cv_t, e, page_table, tbs, b31c, b0c, pg):
    bs, n_pages = page_table.shape
    k_sel = min(TOPK_MAX, (n_pages * PAGE_SIZE + 1) // 4)
    idx_bits = int(math.log2(n_pages * PAGE_SIZE)) + 1
    per_b = lambda shape: pl.BlockSpec((None,) + shape, lambda b, j, pt: (b,) + (0,) * len(shape))
    page_spec = lambda u: pl.BlockSpec((None, None, D_ATT, PAGE_SIZE),
                                       lambda b, j, pt: (e, pt[b, j * pg + u], 0, 0))
    const = lambda shape: pl.BlockSpec(shape, lambda b, j, pt: (0,) * len(shape))
    grid_spec = pltpu.PrefetchScalarGridSpec(
        num_scalar_prefetch=1, grid=(bs, n_pages // pg),
        in_specs=[per_b((n_pages, PAGE_SIZE)), per_b((D_ATT, 1)), per_b((D_ATT, 1)), per_b((D_ATT, 1)),
                  per_b((H_IDX, D_IDX)), per_b((1, D_IDX)), per_b((H_IDX, 1)),
                  const(tbs.shape), const(b31c.shape), const(b0c.shape)]
        + [page_spec(u) for u in range(pg)] + [page_spec(u) for u in range(pg)],
        out_specs=per_b((D_ATT, 1)),
        scratch_shapes=[pltpu.VMEM((n_pages, PAGE_SIZE), F32), pltpu.VMEM((H_A, LANE), F32),
                        pltpu.VMEM((H_A, LANE), F32), pltpu.VMEM((D_ATT, PAGE_SIZE), F32)])
    return pl.pallas_call(
        functools.partial(_sample_attn_kernel, pg=pg, n_pages=n_pages, k_sel=k_sel, idx_bits=idx_bits),
        grid_spec=grid_spec,
        out_shape=jax.ShapeDtypeStruct((bs, D_ATT, 1), F32),
        compiler_params=_cparams("parallel", "arbitrary"),
        name="sample_attn",
    )(page_table, scores, q, k_new, v_new, iqf, ik_new, iwf, tbs, b31c, b0c, *([ck_t] * pg), *([cv_t] * pg))


def _sample_ret_kernel(rq_ref, rk_ref, rv_ref, cos_ref, sin_ref, g_ref, s_ref, o_ref, sn_ref):
    cos_t, sin_t = cos_ref[...], sin_ref[...]
    rq = _rotate_half(rq_ref[...], cos_t, sin_t)
    rk = _rotate_half(rk_ref[...], cos_t, sin_t) * (DK_R ** -0.5)
    rv = rv_ref[...]
    for h in range(H_R):
        qc = _row_to_col(rq[:, h * DK_R:(h + 1) * DK_R], DK_R)
        kc = _row_to_col(rk[:, h * DK_R:(h + 1) * DK_R], DK_R)
        vh = rv[:, h * DV_R:(h + 1) * DV_R]
        s_new = s_ref[h] * g_ref[0:1, h:h + 1] + kc * vh
        sn_ref[h] = s_new
        o_ref[:, h * DV_R:(h + 1) * DV_R] = jnp.sum(qc * s_new, axis=0, keepdims=True)


def _layer_b_spec(layer, shape):
    return pl.BlockSpec((None, None) + shape, lambda b: (layer, b) + (0,) * len(shape))


def _sample_ret(rq, rk, rv, cos_t, sin_t, state, e):
    bs = rq.shape[0]
    gamma = _pad_lanes(1.0 - jnp.exp2(-5.0 - jnp.arange(H_R, dtype=F32)))
    per_b = lambda shape: pl.BlockSpec((None,) + shape, lambda b: (b,) + (0,) * len(shape))
    return pl.pallas_call(
        _sample_ret_kernel,
        grid=(bs,),
        in_specs=[per_b((1, D_RK)), per_b((1, D_RK)), per_b((1, D_RV)), _const_spec((1, D_RK)), _const_spec((1, D_RK)),
                  _const_spec((1, LANE)), _layer_b_spec(e, (H_R, DK_R, DV_R))],
        out_specs=[per_b((1, D_RV)), per_b((H_R, DK_R, DV_R))],
        out_shape=[jax.ShapeDtypeStruct((bs, 1, D_RV), F32), jax.ShapeDtypeStruct(state.shape[1:], F32)],
        compiler_params=_cparams("parallel"),
        name="sample_retention",
    )(rq, rk, rv, cos_t, sin_t, gamma, state)


def _sample_ssd_kernel(xbc_ref, dt_ref, hist_ref, cw_ref, cb_ref, dtb_ref, alog_ref, dsk_ref, s_ref,
                       y_ref, hn_ref, sn_ref):
    x_new = xbc_ref[...]
    hist = hist_ref[...]
    conv = cb_ref[...] + x_new * cw_ref[CONV_W - 1:CONV_W, :]
    for j in range(CONV_W - 1):
        conv = conv + hist[j:j + 1, :] * cw_ref[j:j + 1, :]
    xbc = _silu(conv)
    hn_ref[0:CONV_W - 2, :] = hist[1:CONV_W - 1, :]
    hn_ref[CONV_W - 2:CONV_W - 1, :] = x_new
    dt = _softplus(dt_ref[...] + dtb_ref[...])
    d_a = jnp.exp(dt * (-jnp.exp(alog_ref[...])))
    dsk = dsk_ref[...]
    for g in range(G_SSM):
        bm = xbc[:, D_INNER + g * N_STATE:D_INNER + (g + 1) * N_STATE]
        cm = xbc[:, D_INNER + D_BC + g * N_STATE:D_INNER + D_BC + (g + 1) * N_STATE]
        for hh in range(HPG):
            h = g * HPG + hh
            xh = xbc[:, h * P_SSM:(h + 1) * P_SSM]
            xc = _row_to_col(xh * dt[:, h:h + 1], P_SSM)
            s_new = s_ref[h] * d_a[0:1, h:h + 1] + xc * bm
            sn_ref[h] = s_new
            yc = jnp.sum(s_new * cm, axis=1, keepdims=True)
            y_ref[:, h * P_SSM:(h + 1) * P_SSM] = _col_to_row(yc, P_SSM) + dsk[0:1, h:h + 1] * xh


def _sample_ssd(xbc, dt, hist, cw, cb, dtb, alog, dsk, state, layer):
    bs = xbc.shape[0]
    per_b = lambda shape: pl.BlockSpec((None,) + shape, lambda b: (b,) + (0,) * len(shape))
    return pl.pallas_call(
        _sample_ssd_kernel,
        grid=(bs,),
        in_specs=[per_b((1, CONV_DIM)), per_b((1, LANE)), per_b((CONV_W - 1, CONV_DIM)),
                  _const_spec(cw.shape), _const_spec((1, CONV_DIM)), _const_spec((1, LANE)), _const_spec((1, LANE)),
                  _const_spec((1, LANE)), _layer_b_spec(layer, (NH_SSM, P_SSM, N_STATE))],
        out_specs=[per_b((1, D_INNER)), per_b((CONV_W - 1, CONV_DIM)), per_b((NH_SSM, P_SSM, N_STATE))],
        out_shape=[jax.ShapeDtypeStruct((bs, 1, D_INNER), F32), jax.ShapeDtypeStruct(hist.shape, F32),
                   jax.ShapeDtypeStruct(state.shape[1:], F32)],
        compiler_params=_cparams("parallel"),
        name="sample_ssd",
    )(xbc, dt, hist, cw, cb.reshape(1, CONV_DIM), _pad_lanes(dtb), _pad_lanes(alog), _pad_lanes(dsk), state)


def _even_weight(w):
    q, k, v, iq, ik, iw, rq, rk, rv, rg = jnp.split(w, np.cumsum(EVEN_SIZES)[:-1].tolist(), axis=1)
    pad = jnp.zeros((w.shape[0], EVEN_PAD - sum(EVEN_SIZES)), w.dtype)
    return jnp.concatenate([q, k, v, iq, rv, rg, rq, rk, ik, iw, pad], axis=1).astype(BF16)


def _odd_weight(w):
    return jnp.pad(w, ((0, 0), (0, ODD_PAD - ODD_IN))).astype(BF16)


def _t5_bucket(dist):
    max_exact = NUM_BUCKETS // 2
    d = jnp.maximum(dist, 0)
    ratio = jnp.log(jnp.maximum(d, max_exact).astype(F32) / max_exact) / math.log(MAX_DISTANCE / max_exact)
    large = max_exact + (ratio * (NUM_BUCKETS - max_exact)).astype(I32)
    return jnp.where(d < max_exact, d, jnp.minimum(large, NUM_BUCKETS - 1))


def _bias_tables(rel_bias, tq):
    assert tq >= MAX_DISTANCE
    dist = tq + jnp.arange(tq)[None, :] - jnp.arange(2 * tq)[:, None]
    tb = jnp.where((dist >= 0)[..., None], rel_bias[_t5_bucket(dist)], 0.0)
    tb = jnp.moveaxis(tb, -1, 0)
    b31 = rel_bias[NUM_BUCKETS - 1].reshape(1, H_A)
    tbs = rel_bias[_t5_bucket(PAGE_SIZE - jnp.arange(PAGE_SIZE))].T
    return tb, jnp.pad(b31, ((0, 0), (0, LANE - H_A))), tbs


def kernel(x_prompt, x_sample, cache_k, cache_v, cache_idx_k, state_ret, state_conv, state_ssm, page_table, rel_bias, norm_mix, norm_mlp, norm_final, w_in_even, w_out_even, w_in_odd, conv_w, conv_b, dt_bias, a_log, d_skip, ssm_norm, w_out_odd, w_up, w_down):
    b, s, d = x_prompt.shape
    bs = x_sample.shape[0]
    n_pages = page_table.shape[1]
    past = n_pages * PAGE_SIZE
    tm = min(512, b * s)
    tq = 128
    tk = min(512, s)
    ch = 128
    pg = 8

    xp = x_prompt.reshape(b * s, d)
    xs = x_sample.reshape(bs, d)
    tb, b31, tbs = _bias_tables(rel_bias, tq)
    b31c = rel_bias[NUM_BUCKETS - 1].reshape(H_A, 1)
    b0c = rel_bias[0].reshape(H_A, 1)
    cos_p, sin_p = _rope_tables(jnp.arange(s))
    cos_s, sin_s = _rope_tables(jnp.full((1,), past))
    bf_slices = ((0, 4 * D_ATT), (E_IK, LANE))
    n_layers, n_pool = cache_k.shape[0], cache_k.shape[1]
    ck_t = jnp.transpose(cache_k, (0, 1, 3, 4, 2)).reshape(n_layers, n_pool, D_ATT, PAGE_SIZE)
    cv_t = jnp.transpose(cache_v, (0, 1, 3, 4, 2)).reshape(n_layers, n_pool, D_ATT, PAGE_SIZE)
    cik_t = jnp.transpose(cache_idx_k, (0, 1, 3, 2))

    k_p, v_p, ik_p, k_s, v_s, ik_s, ret_p, ret_s = [], [], [], [], [], [], [], []
    conv_p, conv_s, ssm_p, ssm_s = [], [], [], []
    for layer in range(DEPTH):
        final = layer == DEPTH - 1
        wup, wdn = w_up[layer].astype(BF16), w_down[layer].astype(BF16)
        if layer % 2 == 0:
            e = layer // 2
            w_in = _even_weight(w_in_even[e])
            wo = w_out_even[e].astype(BF16)
            pf, pbq, pbi = _rms_matmul(xp, norm_mix[layer], w_in, tm, bf_slices)
            sf, sbq, _ = _rms_matmul(xs, norm_mix[layer], w_in, bs, bf_slices)
            k_p.append(pf[:, E_K:E_K + D_ATT].reshape(b, s, H_A, DH_A))
            v_p.append(pf[:, E_V:E_V + D_ATT].reshape(b, s, H_A, DH_A))
            ik_p.append(pf[:, E_IK:E_IK + D_IDX].reshape(b, s, D_IDX))
            k_s.append(sf[:, E_K:E_K + D_ATT].reshape(bs, 1, H_A, DH_A))
            v_s.append(sf[:, E_V:E_V + D_ATT].reshape(bs, 1, H_A, DH_A))
            ik_s.append(sf[:, E_IK:E_IK + D_IDX].reshape(bs, 1, D_IDX))
            att = _attn_prompt(pf, pbq, pbi, tb, b31, b, s, tq, tk)
            o, st = _ret_prompt(pf, cos_p, sin_p, b, s, ch)
            ret_p.append(st)
            xp = _mlp(_even_merge(xp, att, o, pf, wo, tm), norm_mlp[layer], wup, wdn, norm_final, tm, final)
            iq_bf = sbq[:, E_IQ:E_IQ + D_ATT].reshape(bs, H_IDX, D_IDX)
            iqf = sf[:, E_IQ:E_IQ + D_ATT].reshape(bs, H_IDX, D_IDX)
            iwf = sf[:, E_IK + E_IW:E_IK + E_IW + H_IDX].reshape(bs, H_IDX, 1)
            scores = _sample_scores(iq_bf, iwf, cik_t, e, page_table, pg)
            att_s = _sample_attn(scores, sf[:, E_Q:E_Q + D_ATT].reshape(bs, D_ATT, 1),
                                 sf[:, E_K:E_K + D_ATT].reshape(bs, D_ATT, 1),
                                 sf[:, E_V:E_V + D_ATT].reshape(bs, D_ATT, 1),
                                 iqf, sf[:, E_IK:E_IK + D_IDX].reshape(bs, 1, D_IDX), iwf,
                                 ck_t, cv_t, e, page_table, tbs, b31c, b0c, pg)
            o_s, st_s = _sample_ret(sf[:, E_RQ:E_RQ + D_RK].reshape(bs, 1, D_RK), sf[:, E_RK:E_RK + D_RK].reshape(bs, 1, D_RK),
                                    sf[:, E_RV:E_RV + D_RV].reshape(bs, 1, D_RV), cos_s, sin_s, state_ret, e)
            ret_s.append(st_s)
            xs = _mlp(_even_merge(xs, att_s.reshape(bs, D_ATT), o_s.reshape(bs, D_RV), sf, wo, bs),
                      norm_mlp[layer], wup, wdn, norm_final, bs, final)
        else:
            o_ = layer // 2
            w_in = _odd_weight(w_in_odd[o_])
            wo = w_out_odd[o_].astype(BF16)
            (pf,) = _rms_matmul(xp, norm_mix[layer], w_in, tm // 2)
            (sf,) = _rms_matmul(xs, norm_mix[layer], w_in, bs)
            y, st = _ssd_prompt(pf, conv_w[o_], conv_b[o_], dt_bias[o_], a_log[o_], d_skip[o_], b, s, ch)
            conv_p.append(pf.reshape(b, s, ODD_PAD)[:, s - (CONV_W - 1):, O_XS:O_XS + CONV_DIM])
            ssm_p.append(st)
            xp = _mlp(_odd_merge(xp, y, pf, ssm_norm[o_], wo, tm), norm_mlp[layer], wup, wdn, norm_final, tm, final)
            y_s, hn, st_s = _sample_ssd(sf[:, O_XS:O_XS + CONV_DIM].reshape(bs, 1, CONV_DIM),
                                        sf[:, O_DT:O_DT + LANE].reshape(bs, 1, LANE), state_conv[o_],
                                        conv_w[o_], conv_b[o_], dt_bias[o_], a_log[o_], d_skip[o_], state_ssm, o_)
            conv_s.append(hn)
            ssm_s.append(st_s)
            xs = _mlp(_odd_merge(xs, y_s.reshape(bs, D_INNER), sf, ssm_norm[o_], wo, bs),
                      norm_mlp[layer], wup, wdn, norm_final, bs, final)

    return (xp.reshape(b, s, d), xs.reshape(bs, 1, d), jnp.stack(k_p), jnp.stack(v_p), jnp.stack(ik_p),
            jnp.stack(k_s), jnp.stack(v_s), jnp.stack(ik_s), jnp.stack(ret_p), jnp.stack(ret_s),
            jnp.stack(conv_p), jnp.stack(conv_s), jnp.stack(ssm_p), jnp.stack(ssm_s))
```

```python
import functools
import math

import jax
import jax.numpy as jnp
import numpy as np
from jax import lax
from jax.experimental import pallas as pl
from jax.experimental.pallas import tpu as pltpu

F32 = jnp.float32
BF16 = jnp.bfloat16
I32 = jnp.int32
I16 = jnp.int16

D_MODEL = 1024
DEPTH = 4
PAGE_SIZE = 128
DH_A = 64
H_A = (D_MODEL // 2) // DH_A
D_ATT = H_A * DH_A
H_IDX = 8
D_IDX = 64
TOPK_MAX = 256
NUM_BUCKETS = 32
MAX_DISTANCE = 128
H_R = 4
DV_R = (D_MODEL // 2) // H_R
DK_R = DV_R // 2
D_RK = H_R * DK_R
D_RV = H_R * DV_R
ROPE_BASE = 10000.0
D_INNER = 2 * D_MODEL
P_SSM = 64
NH_SSM = D_INNER // P_SSM
G_SSM = 8
HPG = NH_SSM // G_SSM
N_STATE = 128
CONV_W = 4
D_BC = G_SSM * N_STATE
CONV_DIM = D_INNER + 2 * D_BC
D_FF = 4 * D_MODEL
EPS = 1e-6

EVEN_SIZES = (D_ATT, D_ATT, D_ATT, H_IDX * D_IDX, D_IDX, H_IDX, D_RK, D_RK, D_RV, D_RV)
ODD_IN = D_INNER + CONV_DIM + NH_SSM

E_Q, E_K, E_V, E_IQ = 0, 512, 1024, 1536
E_RV, E_RG, E_RQ, E_RK, E_IK = 2048, 2560, 3072, 3328, 3584
E_IW = D_IDX
EVEN_PAD = 3712
O_Z, O_XS, O_BC, O_DT = 0, 2048, 4096, 6144
ODD_PAD = 6272

LANE = 128
VMEM_LIMIT = 56 * 1024 * 1024
NEG_BIG = -1e30
LOG2E = math.log2(math.e)
INT_MIN = -(2 ** 31)
KEY_NEG_INF = int(np.array(0xFF800000, np.uint32).view(np.int32)) ^ 0x7FFFFFFF

NT_DIMS = (((1,), (1,)), ((), ()))
TN_DIMS = (((0,), (0,)), ((), ()))


def _cparams(*sem):
    return pltpu.CompilerParams(dimension_semantics=sem, vmem_limit_bytes=VMEM_LIMIT)


def _const_spec(shape):
    nd = len(shape)
    return pl.BlockSpec(shape, lambda *_: (0,) * nd, pipeline_mode=pl.Buffered(1))


def _rms(x, g):
    return x * lax.rsqrt(jnp.mean(x * x, axis=-1, keepdims=True) + EPS) * g


def _silu(x):
    return x / (1.0 + jnp.exp(-x))


def _softplus(x):
    return jnp.maximum(x, 0.0) + jnp.log(1.0 + jnp.exp(-jnp.abs(x)))


def _rms_matmul_kernel(x_ref, g_ref, w_ref, of_ref, *ob_refs, bf_slices, n_chunk):
    h = _rms(x_ref[...], g_ref[...]).astype(BF16)
    n = w_ref.shape[1]
    for c0 in range(0, n, n_chunk):
        c1 = min(c0 + n_chunk, n)
        of_ref[:, c0:c1] = jnp.dot(h, w_ref[:, c0:c1], preferred_element_type=F32)
    for (start, width), r in zip(bf_slices, ob_refs):
        r[...] = of_ref[:, start:start + width].astype(BF16)


def _rms_matmul(x, g, w, tm, bf_slices=()):
    t, d = x.shape
    n = w.shape[1]
    out_shape = [jax.ShapeDtypeStruct((t, n), F32)]
    out_specs = [pl.BlockSpec((tm, n), lambda i: (i, 0))]
    for _, width in bf_slices:
        out_shape.append(jax.ShapeDtypeStruct((t, width), BF16))
        out_specs.append(pl.BlockSpec((tm, width), lambda i: (i, 0)))
    return pl.pallas_call(
        functools.partial(_rms_matmul_kernel, bf_slices=tuple(bf_slices), n_chunk=512),
        grid=(t // tm,),
        in_specs=[pl.BlockSpec((tm, d), lambda i: (i, 0)), _const_spec((1, d)), _const_spec((d, n))],
        out_specs=out_specs,
        out_shape=out_shape,
        compiler_params=_cparams("parallel"),
        name="rms_matmul",
    )(x, g.reshape(1, d), w)


def _mlp_tail(x, gm_ref, wup_ref, wdn_ref, f_chunk=1024):
    h = _rms(x, gm_ref[...]).astype(BF16)
    acc = x
    for f in range(0, D_FF, f_chunk):
        u = jnp.dot(h, wup_ref[:, f:f + f_chunk], preferred_element_type=F32)
        u = jnp.maximum(u, 0.0)
        u = (u * u).astype(BF16)
        acc = acc + jnp.dot(u, wdn_ref[f:f + f_chunk, :], preferred_element_type=F32)
    return acc


def _even_merge_kernel(x_ref, att_ref, o_ref, rg_ref, wo_ref, out_ref):
    o = o_ref[...]
    parts = []
    for h in range(H_R):
        oh = o[:, h * DV_R:(h + 1) * DV_R]
        d = oh - jnp.mean(oh, axis=-1, keepdims=True)
        parts.append(d * lax.rsqrt(jnp.mean(d * d, axis=-1, keepdims=True) + EPS))
    ret = _silu(rg_ref[...]) * jnp.concatenate(parts, axis=1)
    x = x_ref[...]
    x = x + jnp.dot(att_ref[...].astype(BF16), wo_ref[0:D_ATT, :], preferred_element_type=F32)
    out_ref[...] = x + jnp.dot(ret.astype(BF16), wo_ref[D_ATT:D_ATT + D_RV, :], preferred_element_type=F32)


def _odd_merge_kernel(x_ref, y_ref, z_ref, nw_ref, wo_ref, out_ref):
    g = y_ref[...] * _silu(z_ref[...])
    gw = D_INNER // G_SSM
    parts = []
    for k in range(G_SSM):
        gk = g[:, k * gw:(k + 1) * gw]
        parts.append(gk * lax.rsqrt(jnp.mean(gk * gk, axis=-1, keepdims=True) + EPS))
    gn = (jnp.concatenate(parts, axis=1) * nw_ref[...]).astype(BF16)
    out_ref[...] = x_ref[...] + jnp.dot(gn, wo_ref[...], preferred_element_type=F32)


def _mlp_kernel(x_ref, gm_ref, wup_ref, wdn_ref, gf_ref, out_ref, *, final):
    x = _mlp_tail(x_ref[...], gm_ref, wup_ref, wdn_ref)
    if final:
        x = _rms(x, gf_ref[...])
    out_ref[...] = x


def _mlp(x, gm, wup, wdn, gf, tm, final):
    t, d = x.shape
    row = lambda i: (i, 0)
    return pl.pallas_call(
        functools.partial(_mlp_kernel, final=final),
        grid=(t // tm,),
        in_specs=[pl.BlockSpec((tm, d), row), _const_spec((1, d)), _const_spec(wup.shape), _const_spec(wdn.shape),
                  _const_spec((1, d))],
        out_specs=pl.BlockSpec((tm, d), row),
        out_shape=jax.ShapeDtypeStruct((t, d), F32),
        compiler_params=_cparams("parallel"),
        name="mlp",
    )(x, gm.reshape(1, d), wup, wdn, gf.reshape(1, d))


def _even_merge(x, att, o, pf, wo, tm):
    t, d = x.shape
    row = lambda i: (i, 0)
    return pl.pallas_call(
        _even_merge_kernel,
        grid=(t // tm,),
        in_specs=[pl.BlockSpec((tm, d), row), pl.BlockSpec((tm, D_ATT), row), pl.BlockSpec((tm, D_RV), row),
                  pl.BlockSpec((tm, D_RV), lambda i: (i, E_RG // D_RV)), _const_spec(wo.shape)],
        out_specs=pl.BlockSpec((tm, d), row),
        out_shape=jax.ShapeDtypeStruct((t, d), F32),
        compiler_params=_cparams("parallel"),
        name="even_merge",
    )(x, att, o, pf, wo)


def _odd_merge(x, y, pf, nw, wo, tm):
    t, d = x.shape
    row = lambda i: (i, 0)
    return pl.pallas_call(
        _odd_merge_kernel,
        grid=(t // tm,),
        in_specs=[pl.BlockSpec((tm, d), row), pl.BlockSpec((tm, D_INNER), row),
                  pl.BlockSpec((tm, D_INNER), lambda i: (i, O_Z // D_INNER)),
                  _const_spec((1, D_INNER)), _const_spec(wo.shape)],
        out_specs=pl.BlockSpec((tm, d), row),
        out_shape=jax.ShapeDtypeStruct((t, d), F32),
        compiler_params=_cparams("parallel"),
        name="odd_merge",
    )(x, y, pf, nw.reshape(1, D_INNER), wo)


FOLD = 64


def _fold_rows(x, op):
    n, w = x.shape
    if n > FOLD:
        x = op(x.reshape(n // FOLD, FOLD, w), axis=0)
    if x.shape[0] > 8:
        x = op(x.reshape(x.shape[0] // 8, 8, w), axis=0)
    return op(x, axis=0, keepdims=True)


def _score_key(score):
    bits = lax.bitcast_convert_type(score, I32)
    return bits ^ ((bits >> 31) & 0x7FFFFFFF)


def _kth_largest(count_ge, rows, k_sel):
    def bit_step(b, t):
        cand = t + lax.shift_left(jnp.int32(1), 31 - b)
        return jnp.where(count_ge(cand) >= k_sel, cand, t)
    return lax.fori_loop(0, 32, bit_step, jnp.full((rows, 1), INT_MIN, I32))


def _tie_limit(count_tie_lt, need, rows, n_bits):
    def bit_step(b, x):
        cand = x + lax.shift_left(jnp.int32(1), n_bits - 1 - b)
        return jnp.where(count_tie_lt(cand) < need, cand, x)
    return lax.fori_loop(0, n_bits, bit_step, jnp.zeros((rows, 1), I32))


def _attn_prompt_kernel(q_ref, iq_ref, iw_ref, k_ref, vt_ref, ik_ref, tb_ref, b31_ref, o_ref,
                        sk_ref, x_ref, m_ref, l_ref, acc_ref, s_ref, p_ref, kh_ref, kl_ref, *, tq, tk, k_sel, idx_bits):
    i = pl.program_id(1)
    r = tk // tq
    nkc = (i + r) // r
    nfar = jnp.maximum(i - 1, 0) // r
    qidx = lax.broadcasted_iota(I32, (tk, tq), 1) + i * tq
    krow = lax.broadcasted_iota(I32, (tk, tq), 0)
    iw = iw_ref[...] * ((D_IDX ** -0.5) * (H_IDX ** -0.5))

    def key_fold(m):
        return jnp.sum(m.reshape(tk // FOLD, FOLD, tq), axis=0)

    def score_chunk(c, carry):
        ikc = ik_ref[c][:, 0:D_IDX]
        acc = jnp.zeros((tk, tq), F32)
        for h in range(H_IDX):
            s = lax.dot_general(ikc, iq_ref[:, h * D_IDX:(h + 1) * D_IDX], NT_DIMS, preferred_element_type=F32)
            acc = acc + iw[h:h + 1, :] * jnp.maximum(s, 0.0)
        key = _score_key(jnp.where(krow + c * tk <= qidx, acc, -jnp.inf))
        sk_ref[c] = key
        kh_ref[c] = (key >> 16).astype(I16)
        kl_ref[c] = ((key & 0xFFFF) - 2 ** 15).astype(I16)
        return carry
    lax.fori_loop(0, nkc, score_chunk, 0)

    def count16(pred):
        def body(c, part):
            m = jnp.where(pred(c), jnp.int16(1), jnp.int16(0))
            for t in range(tk // FOLD):
                part = part + m[t * FOLD:(t + 1) * FOLD]
            return part
        part = lax.fori_loop(0, nkc, body, jnp.zeros((FOLD, tq), I16))
        return _fold_rows(part.astype(I32).astype(F32), jnp.sum)

    def search16(ref, k_need):
        def step(b, t):
            cand = t + lax.shift_left(jnp.int32(1), 15 - b)
            c16 = cand.astype(I16)
            return jnp.where(count16(lambda c: ref[c] >= c16) >= k_need, cand, t)
        return lax.fori_loop(0, 16, step, jnp.full((1, tq), -2 ** 15, I32))

    t_hi = search16(kh_ref, float(k_sel))
    hi16 = t_hi.astype(I16)
    n_gt_hi = count16(lambda c: kh_ref[c] > hi16)

    def mask_low(c, carry):
        kl_ref[c] = jnp.where(kh_ref[c] == hi16, kl_ref[c], jnp.int16(-2 ** 15))
        return carry
    lax.fori_loop(0, nkc, mask_low, 0)
    t_lo = search16(kl_ref, float(k_sel) - n_gt_hi)
    lo16 = t_lo.astype(I16)
    thr = lax.shift_left(t_hi, 16) + (t_lo + 2 ** 15)
    n_gt = n_gt_hi + count16(lambda c: kl_ref[c] > lo16)
    n_ge = n_gt_hi + count16(lambda c: jnp.logical_and(kh_ref[c] == hi16, kl_ref[c] >= lo16))
    need = float(k_sel) - n_gt
    excess = jnp.logical_and(n_ge - n_gt > need, thr > KEY_NEG_INF)
    x_ref[...] = jnp.full((1, tq), 2 ** idx_bits, I32)

    @pl.when(jnp.max(jnp.where(excess, 1.0, 0.0)) > 0.0)
    def _():
        def count_tie_lt(cand):
            def body(c, part):
                hit = jnp.logical_and(sk_ref[c] == thr, krow + c * tk < cand)
                return part + key_fold(jnp.where(hit, 1.0, 0.0))
            part = lax.fori_loop(0, nkc, body, jnp.zeros((FOLD, tq), F32))
            return _fold_rows(part, jnp.sum)

        def tie_step(b, x):
            cand = x + lax.shift_left(jnp.int32(1), idx_bits - 1 - b)
            return jnp.where(count_tie_lt(cand) < need, cand, x)
        lim = lax.fori_loop(0, idx_bits, tie_step, jnp.zeros((1, tq), I32))
        x_ref[...] = jnp.where(excess, lim, 2 ** idx_bits)

    xlim = x_ref[...]

    m_ref[...] = jnp.full(m_ref.shape, NEG_BIG, F32)
    l_ref[...] = jnp.zeros(l_ref.shape, F32)
    acc_ref[...] = jnp.zeros(acc_ref.shape, F32)

    def attend_chunk(c, near):
        key = sk_ref[c]
        kidx = krow + c * tk
        sel = jnp.logical_or(key > thr, jnp.logical_and(key == thr, kidx <= xlim))
        if near:
            sel = jnp.logical_and(sel, kidx <= qidx)
        amask = jnp.where(sel, 0.0, NEG_BIG)
        kc = k_ref[c]
        vtc = vt_ref[c]
        def logits(h):
            hs = slice(h * DH_A, (h + 1) * DH_A)
            qh = (q_ref[:, hs] * (DH_A ** -0.5 * LOG2E)).astype(BF16)
            return lax.dot_general(kc[:, hs], qh, NT_DIMS, preferred_element_type=F32)

        for h in range(H_A):
            s_ref[h] = logits(h)
        alphas = []
        for h in range(H_A):
            b_far = b31_ref[0:1, h:h + 1]
            s = s_ref[h] + amask
            m_old = m_ref[h:h + 1, :]
            if near:
                start = pl.multiple_of(tk + c * tk - (i - 1) * tq, tq)
                s = s + tb_ref[h, pl.ds(start, tk), :]
                m_new = jnp.maximum(m_old, _fold_rows(s, jnp.max))
                shift = -m_new
            else:
                m_new = jnp.maximum(m_old, _fold_rows(s, jnp.max) + b_far)
                shift = b_far - m_new
            alpha = jnp.exp2(m_old - m_new)
            p = jnp.exp2(s + shift)
            l_ref[h:h + 1, :] = alpha * l_ref[h:h + 1, :] + _fold_rows(p, jnp.sum)
            m_ref[h:h + 1, :] = m_new
            p_ref[h] = p.astype(BF16)
            alphas.append(alpha)
        for h in range(H_A):
            hs = slice(h * DH_A, (h + 1) * DH_A)
            pv = jnp.dot(vtc[hs, :], p_ref[h], preferred_element_type=F32)
            acc_ref[hs, :] = alphas[h] * acc_ref[hs, :] + pv

    def far_body(c, carry):
        attend_chunk(c, False)
        return carry

    def near_body(c, carry):
        attend_chunk(c, True)
        return carry

    lax.fori_loop(0, nfar, far_body, 0)
    lax.fori_loop(nfar, nkc, near_body, 0)

    for h in range(H_A):
        hs = slice(h * DH_A, (h + 1) * DH_A)
        acc_ref[hs, :] = acc_ref[hs, :] / l_ref[h:h + 1, :]
    o_ref[...] = jnp.transpose(acc_ref[...])


def _attn_prompt(pf, pbq, pbi, tb_t, b31, b, s, tq, tk):
    nq, nkc = s // tq, s // tk
    k_sel = min(TOPK_MAX, s // 4)
    idx_bits = int(math.log2(s))
    assert 2 ** idx_bits == s and tk % tq == 0 and tq % LANE == 0
    pbq4 = pbq.reshape(b, nkc, tk, pbq.shape[1])
    pbi4 = pbi.reshape(b, nkc, tk, LANE)
    tb_t = jnp.concatenate([jnp.broadcast_to(b31[0, :H_A, None, None], (H_A, tk, tq)), tb_t,
                            jnp.zeros((H_A, tk - tq, tq), F32)], axis=1)
    vt4 = jnp.swapaxes(pbq4[..., E_V:E_V + D_ATT], 2, 3)
    iw_t = jnp.transpose(pf[:, E_IK + E_IW:E_IK + E_IW + H_IDX])
    return pl.pallas_call(
        functools.partial(_attn_prompt_kernel, tq=tq, tk=tk, k_sel=k_sel, idx_bits=idx_bits),
        grid=(b, nq),
        in_specs=[
            pl.BlockSpec((tq, D_ATT), lambda bb, i: (bb * nq + i, E_Q // D_ATT)),
            pl.BlockSpec((tq, D_ATT), lambda bb, i: (bb * nq + i, E_IQ // D_ATT)),
            pl.BlockSpec((H_IDX, tq), lambda bb, i: (0, bb * nq + i)),
            pl.BlockSpec((None, nkc, tk, D_ATT), lambda bb, i: (bb, 0, 0, E_K // D_ATT), pipeline_mode=pl.Buffered(1)),
            pl.BlockSpec((None, nkc, D_ATT, tk), lambda bb, i: (bb, 0, 0, 0), pipeline_mode=pl.Buffered(1)),
            pl.BlockSpec((None, nkc, tk, LANE), lambda bb, i: (bb, 0, 0, 0), pipeline_mode=pl.Buffered(1)),
            _const_spec(tb_t.shape), _const_spec(b31.shape),
        ],
        out_specs=pl.BlockSpec((tq, D_ATT), lambda bb, i: (bb * nq + i, 0)),
        out_shape=jax.ShapeDtypeStruct((b * s, D_ATT), F32),
        scratch_shapes=[pltpu.VMEM((nkc, tk, tq), I32), pltpu.VMEM((1, tq), I32),
                        pltpu.VMEM((H_A, tq), F32), pltpu.VMEM((H_A, tq), F32),
                        pltpu.VMEM((D_ATT, tq), F32),
                        pltpu.VMEM((H_A, tk, tq), F32), pltpu.VMEM((H_A, tk, tq), BF16),
                        pltpu.VMEM((nkc, tk, tq), I16), pltpu.VMEM((nkc, tk, tq), I16)],
        compiler_params=_cparams("parallel", "arbitrary"),
        name="attn_prompt",
    )(pf, pbq, iw_t, pbq4, vt4, pbi4, tb_t * LOG2E, b31 * LOG2E)


def _rotate_half(x, cos_t, sin_t):
    half = DK_R // 2
    width = x.shape[1]
    lane = lax.broadcasted_iota(I32, x.shape, 1)
    partner = jnp.where(lane % DK_R < half, pltpu.roll(x, width - half, 1), pltpu.roll(x, half, 1))
    return x * cos_t + partner * sin_t


def _ret_prompt_kernel(rq_ref, rk_ref, rv_ref, cos_ref, sin_ref, dm_ref, cd_ref, kd_ref, gl_ref,
                       o_ref, st_ref, s_ref):
    c = pl.program_id(1)

    @pl.when(c == 0)
    def _():
        s_ref[...] = jnp.zeros(s_ref.shape, F32)

    cos_t, sin_t = cos_ref[...], sin_ref[...]
    rq = _rotate_half(rq_ref[...], cos_t, sin_t)
    rk = _rotate_half(rk_ref[...], cos_t, sin_t) * (DK_R ** -0.5)
    rv = rv_ref[...]
    for h in range(H_R):
        qh = rq[:, h * DK_R:(h + 1) * DK_R].astype(BF16)
        kh = rk[:, h * DK_R:(h + 1) * DK_R]
        vh = rv[:, h * DV_R:(h + 1) * DV_R].astype(BF16)
        st = s_ref[h]
        scores = lax.dot_general(qh, kh.astype(BF16), NT_DIMS, preferred_element_type=F32) * dm_ref[h]
        inner = jnp.dot(scores.astype(BF16), vh, preferred_element_type=F32)
        cross = jnp.dot(qh, st.astype(BF16), preferred_element_type=F32) * cd_ref[:, h:h + 1]
        o_ref[:, h * DV_R:(h + 1) * DV_R] = inner + cross
        kdec = (kh * kd_ref[:, h:h + 1]).astype(BF16)
        upd = lax.dot_general(kdec, vh, TN_DIMS, preferred_element_type=F32)
        s_ref[h] = st * gl_ref[0:1, h:h + 1] + upd

    @pl.when(c == pl.num_programs(1) - 1)
    def _():
        st_ref[...] = s_ref[...]


def _ret_tables(ch):
    lg = jnp.log(1.0 - jnp.exp2(-5.0 - jnp.arange(H_R, dtype=F32)))
    n = jnp.arange(ch, dtype=F32)
    diff = n[:, None] - n[None, :]
    causal = diff >= 0
    dm = jnp.where(causal[None], jnp.exp(jnp.where(causal, diff, 0.0)[None] * lg[:, None, None]), 0.0)
    cd = jnp.exp((n[:, None] + 1.0) * lg[None, :])
    kd = jnp.exp((ch - 1.0 - n)[:, None] * lg[None, :])
    gl = jnp.exp(ch * lg)[None, :]
    pad = lambda a: jnp.pad(a, ((0, 0), (0, LANE - H_R)))
    return dm, pad(cd), pad(kd), pad(gl)


def _rope_tables(pos):
    half = DK_R // 2
    inv = ROPE_BASE ** (-jnp.arange(half, dtype=F32) / half)
    ang = pos.astype(F32)[:, None] * inv[None, :]
    cos, sin = jnp.cos(ang), jnp.sin(ang)
    cos_t = jnp.tile(jnp.concatenate([cos, cos], axis=1), (1, H_R))
    sin_t = jnp.tile(jnp.concatenate([-sin, sin], axis=1), (1, H_R))
    return cos_t, sin_t


def _ret_prompt(pf, cos_t, sin_t, b, s, ch):
    nc = s // ch
    dm, cd, kd, gl = _ret_tables(ch)
    tok = lambda col: (lambda bb, c: (bb * nc + c, col))
    return pl.pallas_call(
        _ret_prompt_kernel,
        grid=(b, nc),
        in_specs=[pl.BlockSpec((ch, D_RK), tok(E_RQ // D_RK)), pl.BlockSpec((ch, D_RK), tok(E_RK // D_RK)),
                  pl.BlockSpec((ch, D_RV), tok(E_RV // D_RV)),
                  pl.BlockSpec((ch, D_RK), lambda bb, c: (c, 0)), pl.BlockSpec((ch, D_RK), lambda bb, c: (c, 0)),
                  _const_spec(dm.shape), _const_spec(cd.shape), _const_spec(kd.shape), _const_spec(gl.shape)],
        out_specs=[pl.BlockSpec((ch, D_RV), lambda bb, c: (bb * nc + c, 0)),
                   pl.BlockSpec((None, H_R, DK_R, DV_R), lambda bb, c: (bb, 0, 0, 0))],
        out_shape=[jax.ShapeDtypeStruct((b * s, D_RV), F32), jax.ShapeDtypeStruct((b, H_R, DK_R, DV_R), F32)],
        scratch_shapes=[pltpu.VMEM((H_R, DK_R, DV_R), F32)],
        compiler_params=_cparams("parallel", "arbitrary"),
        name="retention_prompt",
    )(pf, pf, pf, cos_t, sin_t, dm, cd, kd, gl)


def _shift_rows(x, prev8, j):
    rolled = pltpu.roll(x, j, 0)
    rid = lax.broadcasted_iota(I32, prev8.shape, 0)
    head = jnp.where(rid < j, pltpu.roll(prev8, j, 0), rolled[0:8])
    return jnp.concatenate([head, rolled[8:]], axis=0)


def _conv_silu(x, prev8, w_ref, b_ref, col0, width):
    out = b_ref[0:1, col0:col0 + width] + x * w_ref[CONV_W - 1:CONV_W, col0:col0 + width]
    for j in range(1, CONV_W):
        out = out + _shift_rows(x, prev8, j) * w_ref[CONV_W - 1 - j:CONV_W - j, col0:col0 + width]
    return _silu(out)


def _split_pack(x, pieces):
    lane = lax.broadcasted_iota(I32, x.shape, 1)
    rest = jnp.where(lane < NH_SSM, x, 0.0)
    packed = jnp.zeros(x.shape, F32)
    for k in range(pieces):
        piece = rest.astype(BF16).astype(F32)
        rest = rest - piece
        packed = packed + (piece if k == 0 else pltpu.roll(piece, k * NH_SSM, 1))
    return packed.astype(BF16)


def _ssd_prompt_kernel(xs_ref, bc_ref, dt_ref, cw_ref, cb_ref, dtb_ref, alog_ref, dskx_ref, tri_ref, ex_ref, sw_ref,
                       y_ref, st_ref, s_ref, pxs_ref, pbc_ref, w_ref, *, ch):
    c = pl.program_id(1)

    @pl.when(c == 0)
    def _():
        s_ref[...] = jnp.zeros(s_ref.shape, F32)
        pxs_ref[...] = jnp.zeros(pxs_ref.shape, F32)
        pbc_ref[...] = jnp.zeros(pbc_ref.shape, F32)

    xs_raw = xs_ref[...]
    bc_raw = bc_ref[...]
    xs = _conv_silu(xs_raw, pxs_ref[...], cw_ref, cb_ref, 0, D_INNER)
    bc = _conv_silu(bc_raw, pbc_ref[...], cw_ref, cb_ref, D_INNER, 2 * D_BC).astype(BF16)
    pxs_ref[...] = xs_raw[ch - 8:ch]
    pbc_ref[...] = bc_raw[ch - 8:ch]

    dt = _softplus(dt_ref[...] + dtb_ref[...])
    a = -jnp.exp(alog_ref[...])
    acs = jnp.dot(tri_ref[...], dt * a, preferred_element_type=F32, precision=lax.Precision.HIGHEST)
    acs_t = jnp.transpose(acs)
    dec_end = jnp.exp(acs[ch - 1:ch, :] - acs)

    def expand(x):
        return jnp.dot(_split_pack(x, 2), ex_ref[...], preferred_element_type=F32)
    dt_x = expand(dt)
    e_x = expand(jnp.exp(acs))
    xdt = (xs * dt_x).astype(BF16)
    xd = (xs * expand(dt * dec_end)).astype(BF16)

    seg_t = jnp.dot(_split_pack(acs, 3), sw_ref[...], preferred_element_type=F32)
    tri = tri_ref[...] > 0.5
    for g in range(G_SSM):
        bm = bc[:, g * N_STATE:(g + 1) * N_STATE]
        cm = bc[:, D_BC + g * N_STATE:D_BC + (g + 1) * N_STATE]
        cb = lax.dot_general(cm, bm, NT_DIMS, preferred_element_type=F32)
        for hh in range(HPG):
            h = g * HPG + hh
            seg = seg_t[:, h * ch:(h + 1) * ch] - acs_t[h:h + 1, :]
            w_ref[h] = (cb * jnp.exp(jnp.where(tri, seg, -jnp.inf))).astype(BF16)

    e_last_x = e_x[ch - 1:ch, :]
    gw = HPG * P_SSM
    for g in range(G_SSM):
        gs = slice(g * gw, (g + 1) * gw)
        bm = bc[:, g * N_STATE:(g + 1) * N_STATE]
        cm = bc[:, D_BC + g * N_STATE:D_BC + (g + 1) * N_STATE]
        st = s_ref[g]
        y_off = jnp.dot(cm, st.astype(BF16), preferred_element_type=F32)
        y_diag = jnp.concatenate(
            [jnp.dot(w_ref[g * HPG + hh], xdt[:, (g * HPG + hh) * P_SSM:(g * HPG + hh + 1) * P_SSM],
                     preferred_element_type=F32) for hh in range(HPG)], axis=1)
        y_ref[:, gs] = y_diag + y_off * e_x[:, gs] + dskx_ref[:, gs] * xs[:, gs]
        upd = lax.dot_general(bm, xd[:, gs], TN_DIMS, preferred_element_type=F32)
        s_ref[g] = st * e_last_x[:, gs] + upd

    @pl.when(c == pl.num_programs(1) - 1)
    def _():
        for g in range(G_SSM):
            st_ref[g * HPG:(g + 1) * HPG] = jnp.transpose(s_ref[g]).reshape(HPG, P_SSM, N_STATE)


def _pad_lanes(v):
    return jnp.pad(v.reshape(1, -1), ((0, 0), (0, LANE - v.shape[-1])))


def _ssd_prompt(pf, cw, cb, dtb, alog, dsk, b, s, ch):
    nc = s // ch
    tri = jnp.tril(jnp.ones((ch, ch), F32))
    lane_head = jnp.arange(LANE) % NH_SSM
    ex = ((lane_head[:, None] == jnp.arange(D_INNER)[None, :] // P_SSM)
          & (jnp.arange(LANE)[:, None] < 2 * NH_SSM)).astype(BF16)
    sw = ((lane_head[:, None] == jnp.arange(NH_SSM * ch)[None, :] // ch)
          & (jnp.arange(LANE)[:, None] < 3 * NH_SSM)).astype(BF16)
    dskx = jnp.repeat(dsk, P_SSM).reshape(1, D_INNER)
    tok = lambda col: (lambda bb, c: (bb * nc + c, col))
    return pl.pallas_call(
        functools.partial(_ssd_prompt_kernel, ch=ch),
        grid=(b, nc),
        in_specs=[pl.BlockSpec((ch, D_INNER), tok(O_XS // D_INNER)), pl.BlockSpec((ch, 2 * D_BC), tok(O_BC // (2 * D_BC))),
                  pl.BlockSpec((ch, LANE), tok(O_DT // LANE)),
                  _const_spec(cw.shape), _const_spec((1, CONV_DIM)), _const_spec((1, LANE)), _const_spec((1, LANE)),
                  _const_spec((1, D_INNER)), _const_spec((ch, ch)), _const_spec(ex.shape), _const_spec(sw.shape)],
        out_specs=[pl.BlockSpec((ch, D_INNER), lambda bb, c: (bb * nc + c, 0)),
                   pl.BlockSpec((None, NH_SSM, P_SSM, N_STATE), lambda bb, c: (bb, 0, 0, 0))],
        out_shape=[jax.ShapeDtypeStruct((b * s, D_INNER), F32),
                   jax.ShapeDtypeStruct((b, NH_SSM, P_SSM, N_STATE), F32)],
        scratch_shapes=[pltpu.VMEM((G_SSM, N_STATE, HPG * P_SSM), F32), pltpu.VMEM((8, D_INNER), F32),
                        pltpu.VMEM((8, 2 * D_BC), F32), pltpu.VMEM((NH_SSM, ch, ch), BF16)],
        compiler_params=_cparams("parallel", "arbitrary"),
        name="ssd_prompt",
    )(pf, pf, pf, cw, cb.reshape(1, CONV_DIM), _pad_lanes(dtb), _pad_lanes(alog), dskx, tri, ex, sw)


def _row_to_col(row, n):
    eye = lax.broadcasted_iota(I32, (n, n), 0) == lax.broadcasted_iota(I32, (n, n), 1)
    return jnp.sum(jnp.where(eye, jnp.broadcast_to(row, (n, n)), 0.0), axis=1, keepdims=True)


def _col_to_row(col, n):
    eye = lax.broadcasted_iota(I32, (n, n), 0) == lax.broadcasted_iota(I32, (n, n), 1)
    return jnp.sum(jnp.where(eye, jnp.broadcast_to(col, (n, n)), 0.0), axis=0, keepdims=True)


def _sample_scores_kernel(pt_ref, iq_ref, iw_ref, *refs, pg):
    page_refs, o_ref = refs[:pg], refs[pg]
    iq = iq_ref[...]
    iw = iw_ref[...] * ((D_IDX ** -0.5) * (H_IDX ** -0.5))
    for u in range(pg):
        s = jnp.dot(iq, page_refs[u][...].astype(BF16), preferred_element_type=F32)
        o_ref[u:u + 1, :] = jnp.sum(iw * jnp.maximum(s, 0.0), axis=0, keepdims=True)


def _sample_scores(iq, iw, cik_t, e, page_table, pg):
    bs, n_pages = page_table.shape
    page_spec = lambda u: pl.BlockSpec((None, None, D_IDX, PAGE_SIZE),
                                       lambda b, j, pt: (e, pt[b, j * pg + u], 0, 0))
    grid_spec = pltpu.PrefetchScalarGridSpec(
        num_scalar_prefetch=1, grid=(bs, n_pages // pg),
        in_specs=[pl.BlockSpec((None, H_IDX, D_IDX), lambda b, j, pt: (b, 0, 0)),
                  pl.BlockSpec((None, H_IDX, 1), lambda b, j, pt: (b, 0, 0))] + [page_spec(u) for u in range(pg)],
        out_specs=pl.BlockSpec((None, pg, PAGE_SIZE), lambda b, j, pt: (b, j, 0)))
    return pl.pallas_call(
        functools.partial(_sample_scores_kernel, pg=pg),
        grid_spec=grid_spec,
        out_shape=jax.ShapeDtypeStruct((bs, n_pages, PAGE_SIZE), F32),
        compiler_params=_cparams("parallel", "arbitrary"),
        name="sample_scores",
    )(page_table, iq, iw, *([cik_t] * pg))


def _sample_attn_kernel(pt_ref, sc_ref, q_ref, kn_ref, vn_ref, iqf_ref, ikn_ref, iwf_ref, tbs_ref, b31_ref, b0_ref,
                        *refs, pg, n_pages, k_sel, idx_bits):
    k_refs, v_refs = refs[:pg], refs[pg:2 * pg]
    o_ref, sel_ref, m_ref, l_ref, acc_ref = refs[2 * pg:]
    j = pl.program_id(1)
    qcol = q_ref[...] * (DH_A ** -0.5)

    def head_sum(x):
        return jnp.concatenate([jnp.sum(x[h * DH_A:(h + 1) * DH_A], axis=0, keepdims=True) for h in range(H_A)],
                               axis=0)

    def head_expand(x, w):
        return jnp.concatenate([jnp.broadcast_to(x[h:h + 1, :], (DH_A, w)) for h in range(H_A)], axis=0)

    @pl.when(j == 0)
    def _():
        iqf = iqf_ref[...]
        s_new = jnp.sum(iqf * ikn_ref[...], axis=1, keepdims=True)
        w = iwf_ref[...] * ((D_IDX ** -0.5) * (H_IDX ** -0.5))
        sc_new = jnp.sum(w * jnp.maximum(s_new, 0.0), axis=0, keepdims=True)
        key_new = _score_key(sc_new)
        keys = _score_key(sc_ref[...])
        kidx = (lax.broadcasted_iota(I32, keys.shape, 0) * PAGE_SIZE + lax.broadcasted_iota(I32, keys.shape, 1))
        idx_new = n_pages * PAGE_SIZE

        def total(mask, extra):
            part = jnp.sum(jnp.where(mask, 1.0, 0.0), axis=0, keepdims=True)
            return jnp.sum(part, axis=1, keepdims=True) + jnp.where(extra, 1.0, 0.0)

        def count_ge(cand):
            return total(keys >= cand, key_new >= cand)

        thr = _kth_largest(count_ge, 1, float(k_sel))
        n_gt = count_ge(thr + 1)
        need = float(k_sel) - n_gt

        def count_tie_lt(cand):
            return total(jnp.logical_and(keys == thr, kidx < cand), jnp.logical_and(key_new == thr, idx_new < cand))

        lim = _tie_limit(count_tie_lt, need, 1, idx_bits)
        sel = jnp.logical_or(keys > thr, jnp.logical_and(keys == thr, kidx <= lim))
        sel_ref[...] = jnp.where(sel, 1.0, 0.0)
        sel_new = jnp.logical_or(key_new > thr, jnp.logical_and(key_new == thr, idx_new <= lim))
        lg_new = head_sum(qcol * kn_ref[...]) + b0_ref[...]
        m_ref[...] = jnp.broadcast_to(jnp.where(sel_new, lg_new, NEG_BIG), m_ref.shape)
        l0 = jnp.broadcast_to(jnp.where(sel_new, 1.0, 0.0), (H_A, 1))
        l_ref[...] = jnp.broadcast_to(l0, l_ref.shape)
        first_lane = lax.broadcasted_iota(I32, (D_ATT, PAGE_SIZE), 1) == 0
        acc_ref[...] = jnp.where(first_lane, head_expand(l0, PAGE_SIZE) * vn_ref[...], 0.0)

    qb = jnp.broadcast_to(qcol, (D_ATT, PAGE_SIZE))
    logits = []
    for u in range(pg):
        p_idx = j * pg + u
        s = head_sum(k_refs[u][...] * qb)
        s = s + jnp.where(p_idx == n_pages - 1, tbs_ref[...], b31_ref[...])
        logits.append(jnp.where(sel_ref[pl.ds(p_idx, 1), :] > 0.5, s, NEG_BIG))
    m_old = m_ref[:, 0:1]
    m_new = m_old
    for s in logits:
        m_new = jnp.maximum(m_new, jnp.max(s, axis=1, keepdims=True))
    alpha = jnp.exp(m_old - m_new)
    l_new = alpha * l_ref[:, 0:1]
    acc = head_expand(alpha, PAGE_SIZE) * acc_ref[...]
    for u, s in enumerate(logits):
        p = jnp.exp(s - m_new)
        l_new = l_new + jnp.sum(p, axis=1, keepdims=True)
        acc = acc + v_refs[u][...] * head_expand(p, PAGE_SIZE)
    acc_ref[...] = acc
    m_ref[...] = jnp.broadcast_to(m_new, m_ref.shape)
    l_ref[...] = jnp.broadcast_to(l_new, l_ref.shape)

    @pl.when(j == pl.num_programs(1) - 1)
    def _():
        o_ref[...] = jnp.sum(acc_ref[...], axis=1, keepdims=True) / head_expand(l_ref[:, 0:1], 1)


def _sample_attn(scores, q, k_new, v_new, iqf, ik_new, iwf, ck_t, cv_t, e, page_table, tbs, b31c, b0c, pg):
    bs, n_pages = page_table.shape
    k_sel = min(TOPK_MAX, (n_pages * PAGE_SIZE + 1) // 4)
    idx_bits = int(math.log2(n_pages * PAGE_SIZE)) + 1
    per_b = lambda shape: pl.BlockSpec((None,) + shape, lambda b, j, pt: (b,) + (0,) * len(shape))
    page_spec = lambda u: pl.BlockSpec((None, None, D_ATT, PAGE_SIZE),
                                       lambda b, j, pt: (e, pt[b, j * pg + u], 0, 0))
    const = lambda shape: pl.BlockSpec(shape, lambda b, j, pt: (0,) * len(shape))
    grid_spec = pltpu.PrefetchScalarGridSpec(
        num_scalar_prefetch=1, grid=(bs, n_pages // pg),
        in_specs=[per_b((n_pages, PAGE_SIZE)), per_b((D_ATT, 1)), per_b((D_ATT, 1)), per_b((D_ATT, 1)),
                  per_b((H_IDX, D_IDX)), per_b((1, D_IDX)), per_b((H_IDX, 1)),
                  const(tbs.shape), const(b31c.shape), const(b0c.shape)]
        + [page_spec(u) for u in range(pg)] + [page_spec(u) for u in range(pg)],
        out_specs=per_b((D_ATT, 1)),
        scratch_shapes=[pltpu.VMEM((n_pages, PAGE_SIZE), F32), pltpu.VMEM((H_A, LANE), F32),
                        pltpu.VMEM((H_A, LANE), F32), pltpu.VMEM((D_ATT, PAGE_SIZE), F32)])
    return pl.pallas_call(
        functools.partial(_sample_attn_kernel, pg=pg, n_pages=n_pages, k_sel=k_sel, idx_bits=idx_bits),
        grid_spec=grid_spec,
        out_shape=jax.ShapeDtypeStruct((bs, D_ATT, 1), F32),
        compiler_params=_cparams("parallel", "arbitrary"),
        name="sample_attn",
    )(page_table, scores, q, k_new, v_new, iqf, ik_new, iwf, tbs, b31c, b0c, *([ck_t] * pg), *([cv_t] * pg))


def _sample_ret_kernel(rq_ref, rk_ref, rv_ref, cos_ref, sin_ref, g_ref, s_ref, o_ref, sn_ref):
    cos_t, sin_t = cos_ref[...], sin_ref[...]
    rq = _rotate_half(rq_ref[...], cos_t, sin_t)
    rk = _rotate_half(rk_ref[...], cos_t, sin_t) * (DK_R ** -0.5)
    rv = rv_ref[...]
    for h in range(H_R):
        qc = _row_to_col(rq[:, h * DK_R:(h + 1) * DK_R], DK_R)
        kc = _row_to_col(rk[:, h * DK_R:(h + 1) * DK_R], DK_R)
        vh = rv[:, h * DV_R:(h + 1) * DV_R]
        s_new = s_ref[h] * g_ref[0:1, h:h + 1] + kc * vh
        sn_ref[h] = s_new
        o_ref[:, h * DV_R:(h + 1) * DV_R] = jnp.sum(qc * s_new, axis=0, keepdims=True)


def _layer_b_spec(layer, shape):
    return pl.BlockSpec((None, None) + shape, lambda b: (layer, b) + (0,) * len(shape))


def _sample_ret(rq, rk, rv, cos_t, sin_t, state, e):
    bs = rq.shape[0]
    gamma = _pad_lanes(1.0 - jnp.exp2(-5.0 - jnp.arange(H_R, dtype=F32)))
    per_b = lambda shape: pl.BlockSpec((None,) + shape, lambda b: (b,) + (0,) * len(shape))
    return pl.pallas_call(
        _sample_ret_kernel,
        grid=(bs,),
        in_specs=[per_b((1, D_RK)), per_b((1, D_RK)), per_b((1, D_RV)), _const_spec((1, D_RK)), _const_spec((1, D_RK)),
                  _const_spec((1, LANE)), _layer_b_spec(e, (H_R, DK_R, DV_R))],
        out_specs=[per_b((1, D_RV)), per_b((H_R, DK_R, DV_R))],
        out_shape=[jax.ShapeDtypeStruct((bs, 1, D_RV), F32), jax.ShapeDtypeStruct(state.shape[1:], F32)],
        compiler_params=_cparams("parallel"),
        name="sample_retention",
    )(rq, rk, rv, cos_t, sin_t, gamma, state)


def _sample_ssd_kernel(xbc_ref, dt_ref, hist_ref, cw_ref, cb_ref, dtb_ref, alog_ref, dsk_ref, s_ref,
                       y_ref, hn_ref, sn_ref):
    x_new = xbc_ref[...]
    hist = hist_ref[...]
    conv = cb_ref[...] + x_new * cw_ref[CONV_W - 1:CONV_W, :]
    for j in range(CONV_W - 1):
        conv = conv + hist[j:j + 1, :] * cw_ref[j:j + 1, :]
    xbc = _silu(conv)
    hn_ref[0:CONV_W - 2, :] = hist[1:CONV_W - 1, :]
    hn_ref[CONV_W - 2:CONV_W - 1, :] = x_new
    dt = _softplus(dt_ref[...] + dtb_ref[...])
    d_a = jnp.exp(dt * (-jnp.exp(alog_ref[...])))
    dsk = dsk_ref[...]
    for g in range(G_SSM):
        bm = xbc[:, D_INNER + g * N_STATE:D_INNER + (g + 1) * N_STATE]
        cm = xbc[:, D_INNER + D_BC + g * N_STATE:D_INNER + D_BC + (g + 1) * N_STATE]
        for hh in range(HPG):
            h = g * HPG + hh
            xh = xbc[:, h * P_SSM:(h + 1) * P_SSM]
            xc = _row_to_col(xh * dt[:, h:h + 1], P_SSM)
            s_new = s_ref[h] * d_a[0:1, h:h + 1] + xc * bm
            sn_ref[h] = s_new
            yc = jnp.sum(s_new * cm, axis=1, keepdims=True)
            y_ref[:, h * P_SSM:(h + 1) * P_SSM] = _col_to_row(yc, P_SSM) + dsk[0:1, h:h + 1] * xh


def _sample_ssd(xbc, dt, hist, cw, cb, dtb, alog, dsk, state, layer):
    bs = xbc.shape[0]
    per_b = lambda shape: pl.BlockSpec((None,) + shape, lambda b: (b,) + (0,) * len(shape))
    return pl.pallas_call(
        _sample_ssd_kernel,
        grid=(bs,),
        in_specs=[per_b((1, CONV_DIM)), per_b((1, LANE)), per_b((CONV_W - 1, CONV_DIM)),
                  _const_spec(cw.shape), _const_spec((1, CONV_DIM)), _const_spec((1, LANE)), _const_spec((1, LANE)),
                  _const_spec((1, LANE)), _layer_b_spec(layer, (NH_SSM, P_SSM, N_STATE))],
        out_specs=[per_b((1, D_INNER)), per_b((CONV_W - 1, CONV_DIM)), per_b((NH_SSM, P_SSM, N_STATE))],
        out_shape=[jax.ShapeDtypeStruct((bs, 1, D_INNER), F32), jax.ShapeDtypeStruct(hist.shape, F32),
                   jax.ShapeDtypeStruct(state.shape[1:], F32)],
        compiler_params=_cparams("parallel"),
        name="sample_ssd",
    )(xbc, dt, hist, cw, cb.reshape(1, CONV_DIM), _pad_lanes(dtb), _pad_lanes(alog), _pad_lanes(dsk), state)


def _even_weight(w):
    q, k, v, iq, ik, iw, rq, rk, rv, rg = jnp.split(w, np.cumsum(EVEN_SIZES)[:-1].tolist(), axis=1)
    pad = jnp.zeros((w.shape[0], EVEN_PAD - sum(EVEN_SIZES)), w.dtype)
    return jnp.concatenate([q, k, v, iq, rv, rg, rq, rk, ik, iw, pad], axis=1).astype(BF16)


def _odd_weight(w):
    return jnp.pad(w, ((0, 0), (0, ODD_PAD - ODD_IN))).astype(BF16)


def _t5_bucket(dist):
    max_exact = NUM_BUCKETS // 2
    d = jnp.maximum(dist, 0)
    ratio = jnp.log(jnp.maximum(d, max_exact).astype(F32) / max_exact) / math.log(MAX_DISTANCE / max_exact)
    large = max_exact + (ratio * (NUM_BUCKETS - max_exact)).astype(I32)
    return jnp.where(d < max_exact, d, jnp.minimum(large, NUM_BUCKETS - 1))


def _bias_tables(rel_bias, tq):
    assert tq >= MAX_DISTANCE
    f = rel_bias[_t5_bucket(jnp.arange(2 * tq))]
    w = jnp.concatenate([f[tq:], jnp.zeros((tq, H_A), f.dtype), f[:tq]], axis=0)
    n, period = 2 * tq, 3 * tq
    tb = jnp.tile(w, (n, 1))[:n * (period - 1)].reshape(n, period - 1, H_A)[:, :tq, :]
    tb = jnp.moveaxis(tb, -1, 0)
    b31 = rel_bias[NUM_BUCKETS - 1].reshape(1, H_A)
    tbs = rel_bias[_t5_bucket(PAGE_SIZE - jnp.arange(PAGE_SIZE))].T
    return tb, jnp.pad(b31, ((0, 0), (0, LANE - H_A))), tbs


def kernel(x_prompt, x_sample, cache_k, cache_v, cache_idx_k, state_ret, state_conv, state_ssm, page_table, rel_bias, norm_mix, norm_mlp, norm_final, w_in_even, w_out_even, w_in_odd, conv_w, conv_b, dt_bias, a_log, d_skip, ssm_norm, w_out_odd, w_up, w_down):
    b, s, d = x_prompt.shape
    bs = x_sample.shape[0]
    n_pages = page_table.shape[1]
    past = n_pages * PAGE_SIZE
    tm = min(512, b * s)
    tq = 128
    tk = min(512, s)
    ch = 128
    pg = min(16, n_pages)

    xp = x_prompt.reshape(b * s, d)
    xs = x_sample.reshape(bs, d)
    tb, b31, tbs = _bias_tables(rel_bias, tq)
    b31c = rel_bias[NUM_BUCKETS - 1].reshape(H_A, 1)
    b0c = rel_bias[0].reshape(H_A, 1)
    cos_p, sin_p = _rope_tables(jnp.arange(s))
    cos_s, sin_s = _rope_tables(jnp.full((1,), past))
    bf_slices = ((0, 4 * D_ATT), (E_IK, LANE))
    n_layers, n_pool = cache_k.shape[0], cache_k.shape[1]
    ck_t = jnp.transpose(cache_k, (0, 1, 3, 4, 2)).reshape(n_layers, n_pool, D_ATT, PAGE_SIZE)
    cv_t = jnp.transpose(cache_v, (0, 1, 3, 4, 2)).reshape(n_layers, n_pool, D_ATT, PAGE_SIZE)
    cik_t = jnp.transpose(cache_idx_k, (0, 1, 3, 2))

    k_p, v_p, ik_p, k_s, v_s, ik_s, ret_p, ret_s = [], [], [], [], [], [], [], []
    conv_p, conv_s, ssm_p, ssm_s = [], [], [], []
    for layer in range(DEPTH):
        final = layer == DEPTH - 1
        wup, wdn = w_up[layer].astype(BF16), w_down[layer].astype(BF16)
        if layer % 2 == 0:
            e = layer // 2
            w_in = _even_weight(w_in_even[e])
            wo = w_out_even[e].astype(BF16)
            pf, pbq, pbi = _rms_matmul(xp, norm_mix[layer], w_in, tm, bf_slices)
            sf, sbq, _ = _rms_matmul(xs, norm_mix[layer], w_in, bs, bf_slices)
            k_p.append(pf[:, E_K:E_K + D_ATT].reshape(b, s, H_A, DH_A))
            v_p.append(pf[:, E_V:E_V + D_ATT].reshape(b, s, H_A, DH_A))
            ik_p.append(pf[:, E_IK:E_IK + D_IDX].reshape(b, s, D_IDX))
            k_s.append(sf[:, E_K:E_K + D_ATT].reshape(bs, 1, H_A, DH_A))
            v_s.append(sf[:, E_V:E_V + D_ATT].reshape(bs, 1, H_A, DH_A))
            ik_s.append(sf[:, E_IK:E_IK + D_IDX].reshape(bs, 1, D_IDX))
            att = _attn_prompt(pf, pbq, pbi, tb, b31, b, s, tq, tk)
            o, st = _ret_prompt(pf, cos_p, sin_p, b, s, ch)
            ret_p.append(st)
            xp = _mlp(_even_merge(xp, att, o, pf, wo, tm), norm_mlp[layer], wup, wdn, norm_final, tm, final)
            iq_bf = sbq[:, E_IQ:E_IQ + D_ATT].reshape(bs, H_IDX, D_IDX)
            iqf = sf[:, E_IQ:E_IQ + D_ATT].reshape(bs, H_IDX, D_IDX)
            iwf = sf[:, E_IK + E_IW:E_IK + E_IW + H_IDX].reshape(bs, H_IDX, 1)
            scores = _sample_scores(iq_bf, iwf, cik_t, e, page_table, pg)
            att_s = _sample_attn(scores, sf[:, E_Q:E_Q + D_ATT].reshape(bs, D_ATT, 1),
                                 sf[:, E_K:E_K + D_ATT].reshape(bs, D_ATT, 1),
                                 sf[:, E_V:E_V + D_ATT].reshape(bs, D_ATT, 1),
                                 iqf, sf[:, E_IK:E_IK + D_IDX].reshape(bs, 1, D_IDX), iwf,
                                 ck_t, cv_t, e, page_table, tbs, b31c, b0c, pg)
            o_s, st_s = _sample_ret(sf[:, E_RQ:E_RQ + D_RK].reshape(bs, 1, D_RK), sf[:, E_RK:E_RK + D_RK].reshape(bs, 1, D_RK),
                                    sf[:, E_RV:E_RV + D_RV].reshape(bs, 1, D_RV), cos_s, sin_s, state_ret, e)
            ret_s.append(st_s)
            xs = _mlp(_even_merge(xs, att_s.reshape(bs, D_ATT), o_s.reshape(bs, D_RV), sf, wo, bs),
                      norm_mlp[layer], wup, wdn, norm_final, bs, final)
        else:
            o_ = layer // 2
            w_in = _odd_weight(w_in_odd[o_])
            wo = w_out_odd[o_].astype(BF16)
            (pf,) = _rms_matmul(xp, norm_mix[layer], w_in, tm // 2)
            (sf,) = _rms_matmul(xs, norm_mix[layer], w_in, bs)
            y, st = _ssd_prompt(pf, conv_w[o_], conv_b[o_], dt_bias[o_], a_log[o_], d_skip[o_], b, s, ch)
            conv_p.append(pf.reshape(b, s, ODD_PAD)[:, s - (CONV_W - 1):, O_XS:O_XS + CONV_DIM])
            ssm_p.append(st)
            xp = _mlp(_odd_merge(xp, y, pf, ssm_norm[o_], wo, tm), norm_mlp[layer], wup, wdn, norm_final, tm, final)
            y_s, hn, st_s = _sample_ssd(sf[:, O_XS:O_XS + CONV_DIM].reshape(bs, 1, CONV_DIM),
                                        sf[:, O_DT:O_DT + LANE].reshape(bs, 1, LANE), state_conv[o_],
                                        conv_w[o_], conv_b[o_], dt_bias[o_], a_log[o_], d_skip[o_], state_ssm, o_)
            conv_s.append(hn)
            ssm_s.append(st_s)
            xs = _mlp(_odd_merge(xs, y_s.reshape(bs, D_INNER), sf, ssm_norm[o_], wo, bs),
                      norm_mlp[layer], wup, wdn, norm_final, bs, final)

    return (xp.reshape(b, s, d), xs.reshape(bs, 1, d), jnp.stack(k_p), jnp.stack(v_p), jnp.stack(ik_p),
            jnp.stack(k_s), jnp.stack(v_s), jnp.stack(ik_s), jnp.stack(ret_p), jnp.stack(ret_s),
            jnp.stack(conv_p), jnp.stack(conv_s), jnp.stack(ssm_p), jnp.stack(ssm_s))
```

```python
import functools
import math

import jax
import jax.numpy as jnp
import numpy as np
from jax import lax
from jax.experimental import pallas as pl
from jax.experimental.pallas import tpu as pltpu

F32 = jnp.float32
BF16 = jnp.bfloat16
I32 = jnp.int32

D_MODEL = 1024
DEPTH = 4
PAGE_SIZE = 128
DH_A = 64
H_A = (D_MODEL // 2) // DH_A
D_ATT = H_A * DH_A
H_IDX = 8
D_IDX = 64
TOPK_MAX = 256
NUM_BUCKETS = 32
MAX_DISTANCE = 128
H_R = 4
DV_R = (D_MODEL // 2) // H_R
DK_R = DV_R // 2
D_RK = H_R * DK_R
D_RV = H_R * DV_R
ROPE_BASE = 10000.0
D_INNER = 2 * D_MODEL
P_SSM = 64
NH_SSM = D_INNER // P_SSM
G_SSM = 8
HPG = NH_SSM // G_SSM
N_STATE = 128
CONV_W = 4
D_BC = G_SSM * N_STATE
CONV_DIM = D_INNER + 2 * D_BC
D_FF = 4 * D_MODEL
EPS = 1e-6

EVEN_SIZES = (D_ATT, D_ATT, D_ATT, H_IDX * D_IDX, D_IDX, H_IDX, D_RK, D_RK, D_RV, D_RV)
ODD_IN = D_INNER + CONV_DIM + NH_SSM

E_Q, E_K, E_V, E_IQ = 0, 512, 1024, 1536
E_RV, E_RG, E_RQ, E_RK, E_IK = 2048, 2560, 3072, 3328, 3584
E_IW = D_IDX
EVEN_PAD = 3712
O_Z, O_XS, O_BC, O_DT = 0, 2048, 4096, 6144
ODD_PAD = 6272

LANE = 128
VMEM_LIMIT = 56 * 1024 * 1024
NEG_BIG = -1e30
LOG2E = math.log2(math.e)
INT_MIN = -(2 ** 31)
KEY_NEG_INF = int(np.array(0xFF800000, np.uint32).view(np.int32)) ^ 0x7FFFFFFF

NT_DIMS = (((1,), (1,)), ((), ()))
TN_DIMS = (((0,), (0,)), ((), ()))


def _cparams(*sem):
    return pltpu.CompilerParams(dimension_semantics=sem, vmem_limit_bytes=VMEM_LIMIT)


def _const_spec(shape):
    nd = len(shape)
    return pl.BlockSpec(shape, lambda *_: (0,) * nd, pipeline_mode=pl.Buffered(1))


def _rms(x, g):
    return x * lax.rsqrt(jnp.mean(x * x, axis=-1, keepdims=True) + EPS) * g


def _silu(x):
    return x / (1.0 + jnp.exp(-x))


def _softplus(x):
    return jnp.maximum(x, 0.0) + jnp.log(1.0 + jnp.exp(-jnp.abs(x)))


def _rms_matmul_kernel(x_ref, g_ref, w_ref, of_ref, *ob_refs, bf_slices, n_chunk):
    h = _rms(x_ref[...], g_ref[...]).astype(BF16)
    n = w_ref.shape[1]
    for c0 in range(0, n, n_chunk):
        c1 = min(c0 + n_chunk, n)
        of_ref[:, c0:c1] = jnp.dot(h, w_ref[:, c0:c1], preferred_element_type=F32)
    for (start, width), r in zip(bf_slices, ob_refs):
        r[...] = of_ref[:, start:start + width].astype(BF16)


def _rms_matmul(x, g, w, tm, bf_slices=()):
    t, d = x.shape
    n = w.shape[1]
    out_shape = [jax.ShapeDtypeStruct((t, n), F32)]
    out_specs = [pl.BlockSpec((tm, n), lambda i: (i, 0))]
    for _, width in bf_slices:
        out_shape.append(jax.ShapeDtypeStruct((t, width), BF16))
        out_specs.append(pl.BlockSpec((tm, width), lambda i: (i, 0)))
    return pl.pallas_call(
        functools.partial(_rms_matmul_kernel, bf_slices=tuple(bf_slices), n_chunk=512),
        grid=(t // tm,),
        in_specs=[pl.BlockSpec((tm, d), lambda i: (i, 0)), _const_spec((1, d)), _const_spec((d, n))],
        out_specs=out_specs,
        out_shape=out_shape,
        compiler_params=_cparams("parallel"),
        name="rms_matmul",
    )(x, g.reshape(1, d), w)


def _even_proj_kernel(x_ref, g_ref, w_ref, wt_ref, of_ref, obq_ref, obi_ref, kt_ref, vt_ref, ikt_ref, vtb_ref, *,
                      n_chunk):
    h = _rms(x_ref[...], g_ref[...]).astype(BF16)
    n = w_ref.shape[1]
    for c0 in range(0, n, n_chunk):
        c1 = min(c0 + n_chunk, n)
        of_ref[:, c0:c1] = jnp.dot(h, w_ref[:, c0:c1], preferred_element_type=F32)
    obq_ref[...] = of_ref[:, 0:4 * D_ATT].astype(BF16)
    obi_ref[...] = of_ref[:, E_IK:E_IK + LANE].astype(BF16)
    kt_ref[...] = lax.dot_general(wt_ref[0:D_ATT, :], h, NT_DIMS, preferred_element_type=F32)
    vt = lax.dot_general(wt_ref[D_ATT:2 * D_ATT, :], h, NT_DIMS, preferred_element_type=F32)
    vt_ref[...] = vt
    vtb_ref[...] = vt.astype(BF16)
    ikt_ref[...] = lax.dot_general(wt_ref[2 * D_ATT:2 * D_ATT + D_IDX, :], h, NT_DIMS, preferred_element_type=F32)


def _even_proj_prompt(x, g, w, b, s, tm):
    t, d = x.shape
    n = w.shape[1]
    nb = s // tm
    wt = jnp.transpose(jnp.concatenate([w[:, E_K:E_K + D_ATT], w[:, E_V:E_V + D_ATT], w[:, E_IK:E_IK + D_IDX]],
                                       axis=1))
    row = lambda i: (i, 0)
    feat = lambda i: (i // nb, 0, i % nb)
    return pl.pallas_call(
        functools.partial(_even_proj_kernel, n_chunk=512),
        grid=(t // tm,),
        in_specs=[pl.BlockSpec((tm, d), row), _const_spec((1, d)), _const_spec((d, n)), _const_spec(wt.shape)],
        out_specs=[pl.BlockSpec((tm, n), row), pl.BlockSpec((tm, 4 * D_ATT), row), pl.BlockSpec((tm, LANE), row),
                   pl.BlockSpec((None, D_ATT, tm), feat), pl.BlockSpec((None, D_ATT, tm), feat),
                   pl.BlockSpec((None, D_IDX, tm), feat),
                   pl.BlockSpec((None, None, D_ATT, tm), lambda i: (i // nb, i % nb, 0, 0))],
        out_shape=[jax.ShapeDtypeStruct((t, n), F32), jax.ShapeDtypeStruct((t, 4 * D_ATT), BF16),
                   jax.ShapeDtypeStruct((t, LANE), BF16),
                   jax.ShapeDtypeStruct((b, D_ATT, s), F32), jax.ShapeDtypeStruct((b, D_ATT, s), F32),
                   jax.ShapeDtypeStruct((b, D_IDX, s), F32), jax.ShapeDtypeStruct((b, nb, D_ATT, tm), BF16)],
        compiler_params=_cparams("parallel"),
        name="even_proj_prompt",
    )(x, g.reshape(1, d), w, wt)


def _mlp_tail(x, gm_ref, wup_ref, wdn_ref, f_chunk=1024):
    h = _rms(x, gm_ref[...]).astype(BF16)
    acc = x
    for f in range(0, D_FF, f_chunk):
        u = jnp.dot(h, wup_ref[:, f:f + f_chunk], preferred_element_type=F32)
        u = jnp.maximum(u, 0.0)
        u = (u * u).astype(BF16)
        acc = acc + jnp.dot(u, wdn_ref[f:f + f_chunk, :], preferred_element_type=F32)
    return acc


def _even_merge_kernel(x_ref, att_ref, o_ref, rg_ref, wo_ref, out_ref):
    o = o_ref[...]
    parts = []
    for h in range(H_R):
        oh = o[:, h * DV_R:(h + 1) * DV_R]
        d = oh - jnp.mean(oh, axis=-1, keepdims=True)
        parts.append(d * lax.rsqrt(jnp.mean(d * d, axis=-1, keepdims=True) + EPS))
    ret = _silu(rg_ref[...]) * jnp.concatenate(parts, axis=1)
    x = x_ref[...]
    x = x + jnp.dot(att_ref[...].astype(BF16), wo_ref[0:D_ATT, :], preferred_element_type=F32)
    out_ref[...] = x + jnp.dot(ret.astype(BF16), wo_ref[D_ATT:D_ATT + D_RV, :], preferred_element_type=F32)


def _odd_merge_kernel(x_ref, y_ref, z_ref, nw_ref, wo_ref, out_ref):
    g = y_ref[...] * _silu(z_ref[...])
    gw = D_INNER // G_SSM
    parts = []
    for k in range(G_SSM):
        gk = g[:, k * gw:(k + 1) * gw]
        parts.append(gk * lax.rsqrt(jnp.mean(gk * gk, axis=-1, keepdims=True) + EPS))
    gn = (jnp.concatenate(parts, axis=1) * nw_ref[...]).astype(BF16)
    out_ref[...] = x_ref[...] + jnp.dot(gn, wo_ref[...], preferred_element_type=F32)


def _mlp_kernel(x_ref, gm_ref, wup_ref, wdn_ref, gf_ref, out_ref, *, final):
    x = _mlp_tail(x_ref[...], gm_ref, wup_ref, wdn_ref)
    if final:
        x = _rms(x, gf_ref[...])
    out_ref[...] = x


def _mlp(x, gm, wup, wdn, gf, tm, final):
    t, d = x.shape
    row = lambda i: (i, 0)
    return pl.pallas_call(
        functools.partial(_mlp_kernel, final=final),
        grid=(t // tm,),
        in_specs=[pl.BlockSpec((tm, d), row), _const_spec((1, d)), _const_spec(wup.shape), _const_spec(wdn.shape),
                  _const_spec((1, d))],
        out_specs=pl.BlockSpec((tm, d), row),
        out_shape=jax.ShapeDtypeStruct((t, d), F32),
        compiler_params=_cparams("parallel"),
        name="mlp",
    )(x, gm.reshape(1, d), wup, wdn, gf.reshape(1, d))


def _even_merge(x, att, o, pf, wo, tm):
    t, d = x.shape
    row = lambda i: (i, 0)
    return pl.pallas_call(
        _even_merge_kernel,
        grid=(t // tm,),
        in_specs=[pl.BlockSpec((tm, d), row), pl.BlockSpec((tm, D_ATT), row), pl.BlockSpec((tm, D_RV), row),
                  pl.BlockSpec((tm, D_RV), lambda i: (i, E_RG // D_RV)), _const_spec(wo.shape)],
        out_specs=pl.BlockSpec((tm, d), row),
        out_shape=jax.ShapeDtypeStruct((t, d), F32),
        compiler_params=_cparams("parallel"),
        name="even_merge",
    )(x, att, o, pf, wo)


def _odd_merge(x, y, pf, nw, wo, tm):
    t, d = x.shape
    row = lambda i: (i, 0)
    return pl.pallas_call(
        _odd_merge_kernel,
        grid=(t // tm,),
        in_specs=[pl.BlockSpec((tm, d), row), pl.BlockSpec((tm, D_INNER), row),
                  pl.BlockSpec((tm, D_INNER), lambda i: (i, O_Z // D_INNER)),
                  _const_spec((1, D_INNER)), _const_spec(wo.shape)],
        out_specs=pl.BlockSpec((tm, d), row),
        out_shape=jax.ShapeDtypeStruct((t, d), F32),
        compiler_params=_cparams("parallel"),
        name="odd_merge",
    )(x, y, pf, nw.reshape(1, D_INNER), wo)


FOLD = 64


def _fold_rows(x, op):
    n, w = x.shape
    if n > FOLD:
        x = op(x.reshape(n // FOLD, FOLD, w), axis=0)
    if x.shape[0] > 8:
        x = op(x.reshape(x.shape[0] // 8, 8, w), axis=0)
    return op(x, axis=0, keepdims=True)


def _score_key(score):
    bits = lax.bitcast_convert_type(score, I32)
    return bits ^ ((bits >> 31) & 0x7FFFFFFF)


def _kth_largest(count_ge, rows, k_sel):
    def bit_step(b, t):
        cand = t + lax.shift_left(jnp.int32(1), 31 - b)
        return jnp.where(count_ge(cand) >= k_sel, cand, t)
    return lax.fori_loop(0, 32, bit_step, jnp.full((rows, 1), INT_MIN, I32))


def _tie_limit(count_tie_lt, need, rows, n_bits):
    def bit_step(b, x):
        cand = x + lax.shift_left(jnp.int32(1), n_bits - 1 - b)
        return jnp.where(count_tie_lt(cand) < need, cand, x)
    return lax.fori_loop(0, n_bits, bit_step, jnp.zeros((rows, 1), I32))


def _attn_prompt_kernel(q_ref, iq_ref, iw_ref, k_ref, vt_ref, ik_ref, tb_ref, b31_ref, o_ref,
                        sk_ref, x_ref, m_ref, l_ref, acc_ref, s_ref, p_ref, *, tq, tk, k_sel, idx_bits):
    i = pl.program_id(1)
    r = tk // tq
    nkc = (i + r) // r
    nfar = jnp.maximum(i - 1, 0) // r
    qidx = lax.broadcasted_iota(I32, (tk, tq), 1) + i * tq
    krow = lax.broadcasted_iota(I32, (tk, tq), 0)
    iw = iw_ref[...] * ((D_IDX ** -0.5) * (H_IDX ** -0.5))

    def key_fold(m):
        return jnp.sum(m.reshape(tk // FOLD, FOLD, tq), axis=0)

    def score_chunk(c, carry):
        ikc = ik_ref[c][:, 0:D_IDX]
        acc = jnp.zeros((tk, tq), F32)
        for h in range(H_IDX):
            s = lax.dot_general(ikc, iq_ref[:, h * D_IDX:(h + 1) * D_IDX], NT_DIMS, preferred_element_type=F32)
            acc = acc + iw[h:h + 1, :] * jnp.maximum(s, 0.0)
        sk_ref[c] = _score_key(jnp.where(krow + c * tk <= qidx, acc, -jnp.inf))
        return carry
    lax.fori_loop(0, nkc, score_chunk, 0)

    def count_ge(cand):
        def body(c, part):
            return part + key_fold(jnp.where(sk_ref[c] >= cand, 1.0, 0.0))
        part = lax.fori_loop(0, nkc, body, jnp.zeros((FOLD, tq), F32))
        return _fold_rows(part, jnp.sum)

    def kth_step(b, t):
        cand = t + lax.shift_left(jnp.int32(1), 31 - b)
        return jnp.where(count_ge(cand) >= float(k_sel), cand, t)
    thr = lax.fori_loop(0, 32, kth_step, jnp.full((1, tq), INT_MIN, I32))
    n_gt = count_ge(thr + 1)
    n_ge = count_ge(thr)
    need = float(k_sel) - n_gt
    excess = jnp.logical_and(n_ge - n_gt > need, thr > KEY_NEG_INF)
    x_ref[...] = jnp.full((1, tq), 2 ** idx_bits, I32)

    @pl.when(jnp.max(jnp.where(excess, 1.0, 0.0)) > 0.0)
    def _():
        def count_tie_lt(cand):
            def body(c, part):
                hit = jnp.logical_and(sk_ref[c] == thr, krow + c * tk < cand)
                return part + key_fold(jnp.where(hit, 1.0, 0.0))
            part = lax.fori_loop(0, nkc, body, jnp.zeros((FOLD, tq), F32))
            return _fold_rows(part, jnp.sum)

        def tie_step(b, x):
            cand = x + lax.shift_left(jnp.int32(1), idx_bits - 1 - b)
            return jnp.where(count_tie_lt(cand) < need, cand, x)
        lim = lax.fori_loop(0, idx_bits, tie_step, jnp.zeros((1, tq), I32))
        x_ref[...] = jnp.where(excess, lim, 2 ** idx_bits)

    xlim = x_ref[...]

    m_ref[...] = jnp.full(m_ref.shape, NEG_BIG, F32)
    l_ref[...] = jnp.zeros(l_ref.shape, F32)
    acc_ref[...] = jnp.zeros(acc_ref.shape, F32)

    def attend_chunk(c, near):
        key = sk_ref[c]
        kidx = krow + c * tk
        sel = jnp.logical_or(key > thr, jnp.logical_and(key == thr, kidx <= xlim))
        if near:
            sel = jnp.logical_and(sel, kidx <= qidx)
        amask = jnp.where(sel, 0.0, NEG_BIG)
        kc = k_ref[c]
        vtc = vt_ref[c]
        def logits(h):
            hs = slice(h * DH_A, (h + 1) * DH_A)
            qh = (q_ref[:, hs] * (DH_A ** -0.5 * LOG2E)).astype(BF16)
            return lax.dot_general(kc[:, hs], qh, NT_DIMS, preferred_element_type=F32)

        for h in range(H_A):
            s_ref[h] = logits(h)
        alphas = []
        for h in range(H_A):
            b_far = b31_ref[0:1, h:h + 1]
            s = s_ref[h] + amask
            m_old = m_ref[h:h + 1, :]
            if near:
                start = pl.multiple_of(tk + c * tk - (i - 1) * tq, tq)
                s = s + tb_ref[h, pl.ds(start, tk), :]
                m_new = jnp.maximum(m_old, _fold_rows(s, jnp.max))
                shift = -m_new
            else:
                m_new = jnp.maximum(m_old, _fold_rows(s, jnp.max) + b_far)
                shift = b_far - m_new
            alpha = jnp.exp2(m_old - m_new)
            p = jnp.exp2(s + shift)
            l_ref[h:h + 1, :] = alpha * l_ref[h:h + 1, :] + _fold_rows(p, jnp.sum)
            m_ref[h:h + 1, :] = m_new
            p_ref[h] = p.astype(BF16)
            alphas.append(alpha)
        for h in range(H_A):
            hs = slice(h * DH_A, (h + 1) * DH_A)
            pv = jnp.dot(vtc[hs, :], p_ref[h], preferred_element_type=F32)
            acc_ref[hs, :] = alphas[h] * acc_ref[hs, :] + pv

    def far_body(c, carry):
        attend_chunk(c, False)
        return carry

    def near_body(c, carry):
        attend_chunk(c, True)
        return carry

    lax.fori_loop(0, nfar, far_body, 0)
    lax.fori_loop(nfar, nkc, near_body, 0)

    for h in range(H_A):
        hs = slice(h * DH_A, (h + 1) * DH_A)
        acc_ref[hs, :] = acc_ref[hs, :] / l_ref[h:h + 1, :]
    o_ref[...] = jnp.transpose(acc_ref[...])


def _attn_prompt(pf, pbq, pbi, vt4, tb_t, b31, b, s, tq, tk):
    nq, nkc = s // tq, s // tk
    k_sel = min(TOPK_MAX, s // 4)
    idx_bits = int(math.log2(s))
    assert 2 ** idx_bits == s and tk % tq == 0 and tq % LANE == 0
    pbq4 = pbq.reshape(b, nkc, tk, pbq.shape[1])
    pbi4 = pbi.reshape(b, nkc, tk, LANE)
    tb_t = jnp.concatenate([jnp.broadcast_to(b31[0, :H_A, None, None], (H_A, tk, tq)), tb_t,
                            jnp.zeros((H_A, tk - tq, tq), F32)], axis=1)
    assert vt4.shape == (b, nkc, D_ATT, tk)
    iw_t = jnp.transpose(pf[:, E_IK + E_IW:E_IK + E_IW + H_IDX])
    return pl.pallas_call(
        functools.partial(_attn_prompt_kernel, tq=tq, tk=tk, k_sel=k_sel, idx_bits=idx_bits),
        grid=(b, nq),
        in_specs=[
            pl.BlockSpec((tq, D_ATT), lambda bb, i: (bb * nq + i, E_Q // D_ATT)),
            pl.BlockSpec((tq, D_ATT), lambda bb, i: (bb * nq + i, E_IQ // D_ATT)),
            pl.BlockSpec((H_IDX, tq), lambda bb, i: (0, bb * nq + i)),
            pl.BlockSpec((None, nkc, tk, D_ATT), lambda bb, i: (bb, 0, 0, E_K // D_ATT), pipeline_mode=pl.Buffered(1)),
            pl.BlockSpec((None, nkc, D_ATT, tk), lambda bb, i: (bb, 0, 0, 0), pipeline_mode=pl.Buffered(1)),
            pl.BlockSpec((None, nkc, tk, LANE), lambda bb, i: (bb, 0, 0, 0), pipeline_mode=pl.Buffered(1)),
            _const_spec(tb_t.shape), _const_spec(b31.shape),
        ],
        out_specs=pl.BlockSpec((tq, D_ATT), lambda bb, i: (bb * nq + i, 0)),
        out_shape=jax.ShapeDtypeStruct((b * s, D_ATT), F32),
        scratch_shapes=[pltpu.VMEM((nkc, tk, tq), I32), pltpu.VMEM((1, tq), I32),
                        pltpu.VMEM((H_A, tq), F32), pltpu.VMEM((H_A, tq), F32),
                        pltpu.VMEM((D_ATT, tq), F32),
                        pltpu.VMEM((H_A, tk, tq), F32), pltpu.VMEM((H_A, tk, tq), BF16)],
        compiler_params=_cparams("parallel", "arbitrary"),
        name="attn_prompt",
    )(pf, pbq, iw_t, pbq4, vt4, pbi4, tb_t * LOG2E, b31 * LOG2E)


def _rotate_half(x, cos_t, sin_t):
    half = DK_R // 2
    width = x.shape[1]
    lane = lax.broadcasted_iota(I32, x.shape, 1)
    partner = jnp.where(lane % DK_R < half, pltpu.roll(x, width - half, 1), pltpu.roll(x, half, 1))
    return x * cos_t + partner * sin_t


def _ret_prompt_kernel(rq_ref, rk_ref, rv_ref, cos_ref, sin_ref, dm_ref, cd_ref, kd_ref, gl_ref,
                       o_ref, st_ref, s_ref):
    c = pl.program_id(1)

    @pl.when(c == 0)
    def _():
        s_ref[...] = jnp.zeros(s_ref.shape, F32)

    cos_t, sin_t = cos_ref[...], sin_ref[...]
    rq = _rotate_half(rq_ref[...], cos_t, sin_t)
    rk = _rotate_half(rk_ref[...], cos_t, sin_t) * (DK_R ** -0.5)
    rv = rv_ref[...]
    for h in range(H_R):
        qh = rq[:, h * DK_R:(h + 1) * DK_R].astype(BF16)
        kh = rk[:, h * DK_R:(h + 1) * DK_R]
        vh = rv[:, h * DV_R:(h + 1) * DV_R].astype(BF16)
        st = s_ref[h]
        scores = lax.dot_general(qh, kh.astype(BF16), NT_DIMS, preferred_element_type=F32) * dm_ref[h]
        inner = jnp.dot(scores.astype(BF16), vh, preferred_element_type=F32)
        cross = jnp.dot(qh, st.astype(BF16), preferred_element_type=F32) * cd_ref[:, h:h + 1]
        o_ref[:, h * DV_R:(h + 1) * DV_R] = inner + cross
        kdec = (kh * kd_ref[:, h:h + 1]).astype(BF16)
        upd = lax.dot_general(kdec, vh, TN_DIMS, preferred_element_type=F32)
        s_ref[h] = st * gl_ref[0:1, h:h + 1] + upd

    @pl.when(c == pl.num_programs(1) - 1)
    def _():
        st_ref[...] = s_ref[...]


def _ret_tables(ch):
    lg = jnp.log(1.0 - jnp.exp2(-5.0 - jnp.arange(H_R, dtype=F32)))
    n = jnp.arange(ch, dtype=F32)
    diff = n[:, None] - n[None, :]
    causal = diff >= 0
    dm = jnp.where(causal[None], jnp.exp(jnp.where(causal, diff, 0.0)[None] * lg[:, None, None]), 0.0)
    cd = jnp.exp((n[:, None] + 1.0) * lg[None, :])
    kd = jnp.exp((ch - 1.0 - n)[:, None] * lg[None, :])
    gl = jnp.exp(ch * lg)[None, :]
    pad = lambda a: jnp.pad(a, ((0, 0), (0, LANE - H_R)))
    return dm, pad(cd), pad(kd), pad(gl)


def _rope_tables(pos):
    half = DK_R // 2
    inv = ROPE_BASE ** (-jnp.arange(half, dtype=F32) / half)
    ang = pos.astype(F32)[:, None] * inv[None, :]
    cos, sin = jnp.cos(ang), jnp.sin(ang)
    cos_t = jnp.tile(jnp.concatenate([cos, cos], axis=1), (1, H_R))
    sin_t = jnp.tile(jnp.concatenate([-sin, sin], axis=1), (1, H_R))
    return cos_t, sin_t


def _ret_prompt(pf, cos_t, sin_t, b, s, ch):
    nc = s // ch
    dm, cd, kd, gl = _ret_tables(ch)
    tok = lambda col: (lambda bb, c: (bb * nc + c, col))
    return pl.pallas_call(
        _ret_prompt_kernel,
        grid=(b, nc),
        in_specs=[pl.BlockSpec((ch, D_RK), tok(E_RQ // D_RK)), pl.BlockSpec((ch, D_RK), tok(E_RK // D_RK)),
                  pl.BlockSpec((ch, D_RV), tok(E_RV // D_RV)),
                  pl.BlockSpec((ch, D_RK), lambda bb, c: (c, 0)), pl.BlockSpec((ch, D_RK), lambda bb, c: (c, 0)),
                  _const_spec(dm.shape), _const_spec(cd.shape), _const_spec(kd.shape), _const_spec(gl.shape)],
        out_specs=[pl.BlockSpec((ch, D_RV), lambda bb, c: (bb * nc + c, 0)),
                   pl.BlockSpec((None, H_R, DK_R, DV_R), lambda bb, c: (bb, 0, 0, 0))],
        out_shape=[jax.ShapeDtypeStruct((b * s, D_RV), F32), jax.ShapeDtypeStruct((b, H_R, DK_R, DV_R), F32)],
        scratch_shapes=[pltpu.VMEM((H_R, DK_R, DV_R), F32)],
        compiler_params=_cparams("parallel", "arbitrary"),
        name="retention_prompt",
    )(pf, pf, pf, cos_t, sin_t, dm, cd, kd, gl)


def _shift_rows(x, prev8, j):
    rolled = pltpu.roll(x, j, 0)
    rid = lax.broadcasted_iota(I32, prev8.shape, 0)
    head = jnp.where(rid < j, pltpu.roll(prev8, j, 0), rolled[0:8])
    return jnp.concatenate([head, rolled[8:]], axis=0)


def _conv_silu(x, prev8, w_ref, b_ref, col0, width):
    out = b_ref[0:1, col0:col0 + width] + x * w_ref[CONV_W - 1:CONV_W, col0:col0 + width]
    for j in range(1, CONV_W):
        out = out + _shift_rows(x, prev8, j) * w_ref[CONV_W - 1 - j:CONV_W - j, col0:col0 + width]
    return _silu(out)


def _split_pack(x, pieces):
    lane = lax.broadcasted_iota(I32, x.shape, 1)
    rest = jnp.where(lane < NH_SSM, x, 0.0)
    packed = jnp.zeros(x.shape, F32)
    for k in range(pieces):
        piece = rest.astype(BF16).astype(F32)
        rest = rest - piece
        packed = packed + (piece if k == 0 else pltpu.roll(piece, k * NH_SSM, 1))
    return packed.astype(BF16)


def _ssd_prompt_kernel(xs_ref, bc_ref, dt_ref, cw_ref, cb_ref, dtb_ref, alog_ref, dskx_ref, tri_ref, ex_ref, sw_ref,
                       y_ref, st_ref, s_ref, pxs_ref, pbc_ref, w_ref, *, ch):
    c = pl.program_id(1)

    @pl.when(c == 0)
    def _():
        s_ref[...] = jnp.zeros(s_ref.shape, F32)
        pxs_ref[...] = jnp.zeros(pxs_ref.shape, F32)
        pbc_ref[...] = jnp.zeros(pbc_ref.shape, F32)

    xs_raw = xs_ref[...]
    bc_raw = bc_ref[...]
    xs = _conv_silu(xs_raw, pxs_ref[...], cw_ref, cb_ref, 0, D_INNER)
    bc = _conv_silu(bc_raw, pbc_ref[...], cw_ref, cb_ref, D_INNER, 2 * D_BC).astype(BF16)
    pxs_ref[...] = xs_raw[ch - 8:ch]
    pbc_ref[...] = bc_raw[ch - 8:ch]

    dt = _softplus(dt_ref[...] + dtb_ref[...])
    a = -jnp.exp(alog_ref[...])
    acs = jnp.dot(tri_ref[...], dt * a, preferred_element_type=F32, precision=lax.Precision.HIGHEST)
    acs_t = jnp.transpose(acs)
    dec_end = jnp.exp(acs[ch - 1:ch, :] - acs)

    def expand(x):
        return jnp.dot(_split_pack(x, 2), ex_ref[...], preferred_element_type=F32)
    dt_x = expand(dt)
    e_x = expand(jnp.exp(acs))
    xdt = (xs * dt_x).astype(BF16)
    xd = (xs * expand(dt * dec_end)).astype(BF16)

    seg_t = jnp.dot(_split_pack(acs, 3), sw_ref[...], preferred_element_type=F32)
    tri = tri_ref[...] > 0.5
    for g in range(G_SSM):
        bm = bc[:, g * N_STATE:(g + 1) * N_STATE]
        cm = bc[:, D_BC + g * N_STATE:D_BC + (g + 1) * N_STATE]
        cb = lax.dot_general(cm, bm, NT_DIMS, preferred_element_type=F32)
        for hh in range(HPG):
            h = g * HPG + hh
            seg = seg_t[:, h * ch:(h + 1) * ch] - acs_t[h:h + 1, :]
            w_ref[h] = (cb * jnp.exp(jnp.where(tri, seg, -jnp.inf))).astype(BF16)

    e_last_x = e_x[ch - 1:ch, :]
    gw = HPG * P_SSM
    for g in range(G_SSM):
        gs = slice(g * gw, (g + 1) * gw)
        bm = bc[:, g * N_STATE:(g + 1) * N_STATE]
        cm = bc[:, D_BC + g * N_STATE:D_BC + (g + 1) * N_STATE]
        st = s_ref[g]
        y_off = jnp.dot(cm, st.astype(BF16), preferred_element_type=F32)
        y_diag = jnp.concatenate(
            [jnp.dot(w_ref[g * HPG + hh], xdt[:, (g * HPG + hh) * P_SSM:(g * HPG + hh + 1) * P_SSM],
                     preferred_element_type=F32) for hh in range(HPG)], axis=1)
        y_ref[:, gs] = y_diag + y_off * e_x[:, gs] + dskx_ref[:, gs] * xs[:, gs]
        upd = lax.dot_general(bm, xd[:, gs], TN_DIMS, preferred_element_type=F32)
        s_ref[g] = st * e_last_x[:, gs] + upd

    @pl.when(c == pl.num_programs(1) - 1)
    def _():
        for g in range(G_SSM):
            st_ref[g * HPG:(g + 1) * HPG] = jnp.transpose(s_ref[g]).reshape(HPG, P_SSM, N_STATE)


def _pad_lanes(v):
    return jnp.pad(v.reshape(1, -1), ((0, 0), (0, LANE - v.shape[-1])))


def _ssd_prompt(pf, cw, cb, dtb, alog, dsk, b, s, ch):
    nc = s // ch
    tri = jnp.tril(jnp.ones((ch, ch), F32))
    lane_head = jnp.arange(LANE) % NH_SSM
    ex = ((lane_head[:, None] == jnp.arange(D_INNER)[None, :] // P_SSM)
          & (jnp.arange(LANE)[:, None] < 2 * NH_SSM)).astype(BF16)
    sw = ((lane_head[:, None] == jnp.arange(NH_SSM * ch)[None, :] // ch)
          & (jnp.arange(LANE)[:, None] < 3 * NH_SSM)).astype(BF16)
    dskx = jnp.repeat(dsk, P_SSM).reshape(1, D_INNER)
    tok = lambda col: (lambda bb, c: (bb * nc + c, col))
    return pl.pallas_call(
        functools.partial(_ssd_prompt_kernel, ch=ch),
        grid=(b, nc),
        in_specs=[pl.BlockSpec((ch, D_INNER), tok(O_XS // D_INNER)), pl.BlockSpec((ch, 2 * D_BC), tok(O_BC // (2 * D_BC))),
                  pl.BlockSpec((ch, LANE), tok(O_DT // LANE)),
                  _const_spec(cw.shape), _const_spec((1, CONV_DIM)), _const_spec((1, LANE)), _const_spec((1, LANE)),
                  _const_spec((1, D_INNER)), _const_spec((ch, ch)), _const_spec(ex.shape), _const_spec(sw.shape)],
        out_specs=[pl.BlockSpec((ch, D_INNER), lambda bb, c: (bb * nc + c, 0)),
                   pl.BlockSpec((None, NH_SSM, P_SSM, N_STATE), lambda bb, c: (bb, 0, 0, 0))],
        out_shape=[jax.ShapeDtypeStruct((b * s, D_INNER), F32),
                   jax.ShapeDtypeStruct((b, NH_SSM, P_SSM, N_STATE), F32)],
        scratch_shapes=[pltpu.VMEM((G_SSM, N_STATE, HPG * P_SSM), F32), pltpu.VMEM((8, D_INNER), F32),
                        pltpu.VMEM((8, 2 * D_BC), F32), pltpu.VMEM((NH_SSM, ch, ch), BF16)],
        compiler_params=_cparams("parallel", "arbitrary"),
        name="ssd_prompt",
    )(pf, pf, pf, cw, cb.reshape(1, CONV_DIM), _pad_lanes(dtb), _pad_lanes(alog), dskx, tri, ex, sw)


def _row_to_col(row, n):
    eye = lax.broadcasted_iota(I32, (n, n), 0) == lax.broadcasted_iota(I32, (n, n), 1)
    return jnp.sum(jnp.where(eye, jnp.broadcast_to(row, (n, n)), 0.0), axis=1, keepdims=True)


def _col_to_row(col, n):
    eye = lax.broadcasted_iota(I32, (n, n), 0) == lax.broadcasted_iota(I32, (n, n), 1)
    return jnp.sum(jnp.where(eye, jnp.broadcast_to(col, (n, n)), 0.0), axis=0, keepdims=True)


def _sample_scores_kernel(pt_ref, iq_ref, iw_ref, *refs, pg):
    page_refs, o_ref = refs[:pg], refs[pg]
    iq = iq_ref[...]
    iw = iw_ref[...] * ((D_IDX ** -0.5) * (H_IDX ** -0.5))
    for u in range(pg):
        s = jnp.dot(iq, page_refs[u][...].astype(BF16), preferred_element_type=F32)
        o_ref[u:u + 1, :] = jnp.sum(iw * jnp.maximum(s, 0.0), axis=0, keepdims=True)


def _sample_scores(iq, iw, cik_t, e, page_table, pg):
    bs, n_pages = page_table.shape
    page_spec = lambda u: pl.BlockSpec((None, None, D_IDX, PAGE_SIZE),
                                       lambda b, j, pt: (e, pt[b, j * pg + u], 0, 0))
    grid_spec = pltpu.PrefetchScalarGridSpec(
        num_scalar_prefetch=1, grid=(bs, n_pages // pg),
        in_specs=[pl.BlockSpec((None, H_IDX, D_IDX), lambda b, j, pt: (b, 0, 0)),
                  pl.BlockSpec((None, H_IDX, 1), lambda b, j, pt: (b, 0, 0))] + [page_spec(u) for u in range(pg)],
        out_specs=pl.BlockSpec((None, pg, PAGE_SIZE), lambda b, j, pt: (b, j, 0)))
    return pl.pallas_call(
        functools.partial(_sample_scores_kernel, pg=pg),
        grid_spec=grid_spec,
        out_shape=jax.ShapeDtypeStruct((bs, n_pages, PAGE_SIZE), F32),
        compiler_params=_cparams("parallel", "arbitrary"),
        name="sample_scores",
    )(page_table, iq, iw, *([cik_t] * pg))


def _sample_select_kernel(sc_ref, iq_ref, ikw_ref, sel_ref, seln_ref, *, k_sel, idx_bits):
    bs, n_keys = sc_ref.shape
    ikw = ikw_ref[...]
    w = ikw[:, E_IW:E_IW + H_IDX] * ((D_IDX ** -0.5) * (H_IDX ** -0.5))
    sc_new = jnp.zeros((bs, 1), F32)
    for h in range(H_IDX):
        s_h = jnp.sum(iq_ref[:, h * D_IDX:(h + 1) * D_IDX] * ikw[:, 0:D_IDX], axis=1, keepdims=True)
        sc_new = sc_new + w[:, h:h + 1] * jnp.maximum(s_h, 0.0)
    key_new = _score_key(sc_new)
    keys = _score_key(sc_ref[...])
    kidx = lax.broadcasted_iota(I32, keys.shape, 1)

    def total(mask, extra):
        m = jnp.where(mask, 1.0, 0.0)
        part = m[:, 0:LANE]
        for t in range(1, n_keys // LANE):
            part = part + m[:, t * LANE:(t + 1) * LANE]
        return jnp.sum(part, axis=1, keepdims=True) + jnp.where(extra, 1.0, 0.0)

    def count_ge(cand):
        return total(keys >= cand, key_new >= cand)

    thr = _kth_largest(count_ge, bs, float(k_sel))
    need = float(k_sel) - count_ge(thr + 1)

    def count_tie_lt(cand):
        return total(jnp.logical_and(keys == thr, kidx < cand), jnp.logical_and(key_new == thr, n_keys < cand))

    lim = _tie_limit(count_tie_lt, need, bs, idx_bits)
    sel = jnp.logical_or(keys > thr, jnp.logical_and(keys == thr, kidx <= lim))
    sel_ref[...] = jnp.where(sel, 1.0, 0.0)
    sel_new = jnp.logical_or(key_new > thr, jnp.logical_and(key_new == thr, n_keys <= lim))
    seln_ref[...] = jnp.broadcast_to(jnp.where(sel_new, 1.0, 0.0), seln_ref.shape)


def _sample_select(scores, iqf, ikw):
    bs, n_keys = scores.shape
    k_sel = min(TOPK_MAX, (n_keys + 1) // 4)
    idx_bits = int(math.log2(n_keys)) + 1
    full = lambda a: pl.BlockSpec(a.shape, lambda: (0,) * a.ndim)
    return pl.pallas_call(
        functools.partial(_sample_select_kernel, k_sel=k_sel, idx_bits=idx_bits),
        in_specs=[full(scores), full(iqf), full(ikw)],
        out_specs=[pl.BlockSpec((bs, n_keys), lambda: (0, 0)), pl.BlockSpec((bs, LANE), lambda: (0, 0))],
        out_shape=[jax.ShapeDtypeStruct((bs, n_keys), F32), jax.ShapeDtypeStruct((bs, LANE), F32)],
        compiler_params=pltpu.CompilerParams(vmem_limit_bytes=VMEM_LIMIT),
        name="sample_select",
    )(scores, iqf, ikw)


def _sample_attn_kernel(pt_ref, sel_ref, seln_ref, q_ref, kn_ref, vn_ref, tbs_ref, b31_ref, b0_ref,
                        *refs, pg, n_pages):
    k_refs, v_refs = refs[:pg], refs[pg:2 * pg]
    o_ref, m_ref, l_ref, acc_ref = refs[2 * pg:]
    j = pl.program_id(1)
    qcol = q_ref[...] * (DH_A ** -0.5)

    def head_sum(x):
        return jnp.concatenate([jnp.sum(x[h * DH_A:(h + 1) * DH_A], axis=0, keepdims=True) for h in range(H_A)],
                               axis=0)

    def head_expand(x, w):
        return jnp.concatenate([jnp.broadcast_to(x[h:h + 1, :], (DH_A, w)) for h in range(H_A)], axis=0)

    @pl.when(j == 0)
    def _():
        sel_new = seln_ref[0:1, 0:1] > 0.5
        lg_new = head_sum(qcol * kn_ref[...]) + b0_ref[...]
        m_ref[...] = jnp.broadcast_to(jnp.where(sel_new, lg_new, NEG_BIG), m_ref.shape)
        l0 = jnp.broadcast_to(jnp.where(sel_new, 1.0, 0.0), (H_A, 1))
        l_ref[...] = jnp.broadcast_to(l0, l_ref.shape)
        first_lane = lax.broadcasted_iota(I32, (D_ATT, PAGE_SIZE), 1) == 0
        acc_ref[...] = jnp.where(first_lane, head_expand(l0, PAGE_SIZE) * vn_ref[...], 0.0)

    qb = jnp.broadcast_to(qcol, (D_ATT, PAGE_SIZE))
    logits = []
    for u in range(pg):
        p_idx = j * pg + u
        s = head_sum(k_refs[u][...] * qb)
        s = s + jnp.where(p_idx == n_pages - 1, tbs_ref[...], b31_ref[...])
        logits.append(jnp.where(sel_ref[pl.ds(p_idx, 1), :] > 0.5, s, NEG_BIG))
    m_old = m_ref[:, 0:1]
    m_new = m_old
    for s in logits:
        m_new = jnp.maximum(m_new, jnp.max(s, axis=1, keepdims=True))
    alpha = jnp.exp(m_old - m_new)
    l_new = alpha * l_ref[:, 0:1]
    acc = head_expand(alpha, PAGE_SIZE) * acc_ref[...]
    for u, s in enumerate(logits):
        p = jnp.exp(s - m_new)
        l_new = l_new + jnp.sum(p, axis=1, keepdims=True)
        acc = acc + v_refs[u][...] * head_expand(p, PAGE_SIZE)
    acc_ref[...] = acc
    m_ref[...] = jnp.broadcast_to(m_new, m_ref.shape)
    l_ref[...] = jnp.broadcast_to(l_new, l_ref.shape)

    @pl.when(j == pl.num_programs(1) - 1)
    def _():
        o_ref[...] = jnp.sum(acc_ref[...], axis=1, keepdims=True) / head_expand(l_ref[:, 0:1], 1)


def _sample_attn(sel, sel_new, q, k_new, v_new, ck_t, cv_t, e, page_table, tbs, b31c, b0c, pg):
    bs, n_pages = page_table.shape
    per_b = lambda shape: pl.BlockSpec((None,) + shape, lambda b, j, pt: (b,) + (0,) * len(shape))
    page_spec = lambda u: pl.BlockSpec((None, None, D_ATT, PAGE_SIZE),
                                       lambda b, j, pt: (e, pt[b, j * pg + u], 0, 0))
    const = lambda shape: pl.BlockSpec(shape, lambda b, j, pt: (0,) * len(shape))
    grid_spec = pltpu.PrefetchScalarGridSpec(
        num_scalar_prefetch=1, grid=(bs, n_pages // pg),
        in_specs=[per_b((n_pages, PAGE_SIZE)), per_b((1, LANE)), per_b((D_ATT, 1)), per_b((D_ATT, 1)),
                  per_b((D_ATT, 1)), const(tbs.shape), const(b31c.shape), const(b0c.shape)]
        + [page_spec(u) for u in range(pg)] + [page_spec(u) for u in range(pg)],
        out_specs=per_b((D_ATT, 1)),
        scratch_shapes=[pltpu.VMEM((H_A, LANE), F32), pltpu.VMEM((H_A, LANE), F32),
                        pltpu.VMEM((D_ATT, PAGE_SIZE), F32)])
    return pl.pallas_call(
        functools.partial(_sample_attn_kernel, pg=pg, n_pages=n_pages),
        grid_spec=grid_spec,
        out_shape=jax.ShapeDtypeStruct((bs, D_ATT, 1), F32),
        compiler_params=_cparams("parallel", "arbitrary"),
        name="sample_attn",
    )(page_table, sel, sel_new, q, k_new, v_new, tbs, b31c, b0c, *([ck_t] * pg), *([cv_t] * pg))


def _sample_ret_kernel(rq_ref, rk_ref, rv_ref, cos_ref, sin_ref, g_ref, s_ref, o_ref, sn_ref):
    cos_t, sin_t = cos_ref[...], sin_ref[...]
    rq = _rotate_half(rq_ref[...], cos_t, sin_t)
    rk = _rotate_half(rk_ref[...], cos_t, sin_t) * (DK_R ** -0.5)
    rv = rv_ref[...]
    for h in range(H_R):
        qc = _row_to_col(rq[:, h * DK_R:(h + 1) * DK_R], DK_R)
        kc = _row_to_col(rk[:, h * DK_R:(h + 1) * DK_R], DK_R)
        vh = rv[:, h * DV_R:(h + 1) * DV_R]
        s_new = s_ref[h] * g_ref[0:1, h:h + 1] + kc * vh
        sn_ref[h] = s_new
        o_ref[:, h * DV_R:(h + 1) * DV_R] = jnp.sum(qc * s_new, axis=0, keepdims=True)


def _layer_b_spec(layer, shape):
    return pl.BlockSpec((None, None) + shape, lambda b: (layer, b) + (0,) * len(shape))


def _sample_ret(rq, rk, rv, cos_t, sin_t, state, e):
    bs = rq.shape[0]
    gamma = _pad_lanes(1.0 - jnp.exp2(-5.0 - jnp.arange(H_R, dtype=F32)))
    per_b = lambda shape: pl.BlockSpec((None,) + shape, lambda b: (b,) + (0,) * len(shape))
    return pl.pallas_call(
        _sample_ret_kernel,
        grid=(bs,),
        in_specs=[per_b((1, D_RK)), per_b((1, D_RK)), per_b((1, D_RV)), _const_spec((1, D_RK)), _const_spec((1, D_RK)),
                  _const_spec((1, LANE)), _layer_b_spec(e, (H_R, DK_R, DV_R))],
        out_specs=[per_b((1, D_RV)), per_b((H_R, DK_R, DV_R))],
        out_shape=[jax.ShapeDtypeStruct((bs, 1, D_RV), F32), jax.ShapeDtypeStruct(state.shape[1:], F32)],
        compiler_params=_cparams("parallel"),
        name="sample_retention",
    )(rq, rk, rv, cos_t, sin_t, gamma, state)


def _sample_ssd_kernel(xbc_ref, dt_ref, hist_ref, cw_ref, cb_ref, dtb_ref, alog_ref, dsk_ref, s_ref,
                       y_ref, hn_ref, sn_ref):
    x_new = xbc_ref[...]
    hist = hist_ref[...]
    conv = cb_ref[...] + x_new * cw_ref[CONV_W - 1:CONV_W, :]
    for j in range(CONV_W - 1):
        conv = conv + hist[j:j + 1, :] * cw_ref[j:j + 1, :]
    xbc = _silu(conv)
    hn_ref[0:CONV_W - 2, :] = hist[1:CONV_W - 1, :]
    hn_ref[CONV_W - 2:CONV_W - 1, :] = x_new
    dt = _softplus(dt_ref[...] + dtb_ref[...])
    d_a = jnp.exp(dt * (-jnp.exp(alog_ref[...])))
    dsk = dsk_ref[...]
    for g in range(G_SSM):
        bm = xbc[:, D_INNER + g * N_STATE:D_INNER + (g + 1) * N_STATE]
        cm = xbc[:, D_INNER + D_BC + g * N_STATE:D_INNER + D_BC + (g + 1) * N_STATE]
        for hh in range(HPG):
            h = g * HPG + hh
            xh = xbc[:, h * P_SSM:(h + 1) * P_SSM]
            xc = _row_to_col(xh * dt[:, h:h + 1], P_SSM)
            s_new = s_ref[h] * d_a[0:1, h:h + 1] + xc * bm
            sn_ref[h] = s_new
            yc = jnp.sum(s_new * cm, axis=1, keepdims=True)
            y_ref[:, h * P_SSM:(h + 1) * P_SSM] = _col_to_row(yc, P_SSM) + dsk[0:1, h:h + 1] * xh


def _sample_ssd(xbc, dt, hist, cw, cb, dtb, alog, dsk, state, layer):
    bs = xbc.shape[0]
    per_b = lambda shape: pl.BlockSpec((None,) + shape, lambda b: (b,) + (0,) * len(shape))
    return pl.pallas_call(
        _sample_ssd_kernel,
        grid=(bs,),
        in_specs=[per_b((1, CONV_DIM)), per_b((1, LANE)), per_b((CONV_W - 1, CONV_DIM)),
                  _const_spec(cw.shape), _const_spec((1, CONV_DIM)), _const_spec((1, LANE)), _const_spec((1, LANE)),
                  _const_spec((1, LANE)), _layer_b_spec(layer, (NH_SSM, P_SSM, N_STATE))],
        out_specs=[per_b((1, D_INNER)), per_b((CONV_W - 1, CONV_DIM)), per_b((NH_SSM, P_SSM, N_STATE))],
        out_shape=[jax.ShapeDtypeStruct((bs, 1, D_INNER), F32), jax.ShapeDtypeStruct(hist.shape, F32),
                   jax.ShapeDtypeStruct(state.shape[1:], F32)],
        compiler_params=_cparams("parallel"),
        name="sample_ssd",
    )(xbc, dt, hist, cw, cb.reshape(1, CONV_DIM), _pad_lanes(dtb), _pad_lanes(alog), _pad_lanes(dsk), state)


def _even_weight(w):
    q, k, v, iq, ik, iw, rq, rk, rv, rg = jnp.split(w, np.cumsum(EVEN_SIZES)[:-1].tolist(), axis=1)
    pad = jnp.zeros((w.shape[0], EVEN_PAD - sum(EVEN_SIZES)), w.dtype)
    return jnp.concatenate([q, k, v, iq, rv, rg, rq, rk, ik, iw, pad], axis=1).astype(BF16)


def _odd_weight(w):
    return jnp.pad(w, ((0, 0), (0, ODD_PAD - ODD_IN))).astype(BF16)


def _t5_bucket(dist):
    max_exact = NUM_BUCKETS // 2
    d = jnp.maximum(dist, 0)
    ratio = jnp.log(jnp.maximum(d, max_exact).astype(F32) / max_exact) / math.log(MAX_DISTANCE / max_exact)
    large = max_exact + (ratio * (NUM_BUCKETS - max_exact)).astype(I32)
    return jnp.where(d < max_exact, d, jnp.minimum(large, NUM_BUCKETS - 1))


def _bias_tables(rel_bias, tq):
    assert tq >= MAX_DISTANCE
    f = rel_bias[_t5_bucket(jnp.arange(2 * tq))]
    w = jnp.concatenate([f[tq:], jnp.zeros((tq, H_A), f.dtype), f[:tq]], axis=0)
    n, period = 2 * tq, 3 * tq
    tb = jnp.tile(w, (n, 1))[:n * (period - 1)].reshape(n, period - 1, H_A)[:, :tq, :]
    tb = jnp.moveaxis(tb, -1, 0)
    b31 = rel_bias[NUM_BUCKETS - 1].reshape(1, H_A)
    tbs = rel_bias[_t5_bucket(PAGE_SIZE - jnp.arange(PAGE_SIZE))].T
    return tb, jnp.pad(b31, ((0, 0), (0, LANE - H_A))), tbs


def kernel(x_prompt, x_sample, cache_k, cache_v, cache_idx_k, state_ret, state_conv, state_ssm, page_table, rel_bias, norm_mix, norm_mlp, norm_final, w_in_even, w_out_even, w_in_odd, conv_w, conv_b, dt_bias, a_log, d_skip, ssm_norm, w_out_odd, w_up, w_down):
    b, s, d = x_prompt.shape
    bs = x_sample.shape[0]
    n_pages = page_table.shape[1]
    past = n_pages * PAGE_SIZE
    tm = min(512, b * s)
    tq = 128
    tk = min(512, s)
    ch = 128
    pg = min(16, n_pages)

    xp = x_prompt.reshape(b * s, d)
    xs = x_sample.reshape(bs, d)
    tb, b31, tbs = _bias_tables(rel_bias, tq)
    b31c = rel_bias[NUM_BUCKETS - 1].reshape(H_A, 1)
    b0c = rel_bias[0].reshape(H_A, 1)
    cos_p, sin_p = _rope_tables(jnp.arange(s))
    cos_s, sin_s = _rope_tables(jnp.full((1,), past))
    bf_slices = ((0, 4 * D_ATT), (E_IK, LANE))
    n_layers, n_pool = cache_k.shape[0], cache_k.shape[1]
    ck_t = jnp.transpose(cache_k, (0, 1, 3, 4, 2)).reshape(n_layers, n_pool, D_ATT, PAGE_SIZE)
    cv_t = jnp.transpose(cache_v, (0, 1, 3, 4, 2)).reshape(n_layers, n_pool, D_ATT, PAGE_SIZE)
    cik_t = jnp.transpose(cache_idx_k, (0, 1, 3, 2))

    k_p, v_p, ik_p, k_s, v_s, ik_s, ret_p, ret_s = [], [], [], [], [], [], [], []
    conv_p, conv_s, ssm_p, ssm_s = [], [], [], []
    for layer in range(DEPTH):
        final = layer == DEPTH - 1
        wup, wdn = w_up[layer].astype(BF16), w_down[layer].astype(BF16)
        if layer % 2 == 0:
            e = layer // 2
            w_in = _even_weight(w_in_even[e])
            wo = w_out_even[e].astype(BF16)
            pf, pbq, pbi, k_t, v_t, ik_t, vtb = _even_proj_prompt(xp, norm_mix[layer], w_in, b, s, tk)
            sf, sbq, _ = _rms_matmul(xs, norm_mix[layer], w_in, bs, bf_slices)
            k_p.append(k_t)
            v_p.append(v_t)
            ik_p.append(ik_t)
            k_s.append(sf[:, E_K:E_K + D_ATT].reshape(bs, 1, H_A, DH_A))
            v_s.append(sf[:, E_V:E_V + D_ATT].reshape(bs, 1, H_A, DH_A))
            ik_s.append(sf[:, E_IK:E_IK + D_IDX].reshape(bs, 1, D_IDX))
            att = _attn_prompt(pf, pbq, pbi, vtb, tb, b31, b, s, tq, tk)
            o, st = _ret_prompt(pf, cos_p, sin_p, b, s, ch)
            ret_p.append(st)
            xp = _mlp(_even_merge(xp, att, o, pf, wo, tm), norm_mlp[layer], wup, wdn, norm_final, tm, final)
            iq_bf = sbq[:, E_IQ:E_IQ + D_ATT].reshape(bs, H_IDX, D_IDX)
            iwf = sf[:, E_IK + E_IW:E_IK + E_IW + H_IDX].reshape(bs, H_IDX, 1)
            scores = _sample_scores(iq_bf, iwf, cik_t, e, page_table, pg)
            sel, sel_new = _sample_select(scores.reshape(bs, past), sf[:, E_IQ:E_IQ + D_ATT],
                                          sf[:, E_IK:E_IK + LANE])
            att_s = _sample_attn(sel.reshape(bs, n_pages, PAGE_SIZE), sel_new.reshape(bs, 1, LANE),
                                 sf[:, E_Q:E_Q + D_ATT].reshape(bs, D_ATT, 1),
                                 sf[:, E_K:E_K + D_ATT].reshape(bs, D_ATT, 1),
                                 sf[:, E_V:E_V + D_ATT].reshape(bs, D_ATT, 1),
                                 ck_t, cv_t, e, page_table, tbs, b31c, b0c, pg)
            o_s, st_s = _sample_ret(sf[:, E_RQ:E_RQ + D_RK].reshape(bs, 1, D_RK), sf[:, E_RK:E_RK + D_RK].reshape(bs, 1, D_RK),
                                    sf[:, E_RV:E_RV + D_RV].reshape(bs, 1, D_RV), cos_s, sin_s, state_ret, e)
            ret_s.append(st_s)
            xs = _mlp(_even_merge(xs, att_s.reshape(bs, D_ATT), o_s.reshape(bs, D_RV), sf, wo, bs),
                      norm_mlp[layer], wup, wdn, norm_final, bs, final)
        else:
            o_ = layer // 2
            w_in = _odd_weight(w_in_odd[o_])
            wo = w_out_odd[o_].astype(BF16)
            (pf,) = _rms_matmul(xp, norm_mix[layer], w_in, tm // 2)
            (sf,) = _rms_matmul(xs, norm_mix[layer], w_in, bs)
            y, st = _ssd_prompt(pf, conv_w[o_], conv_b[o_], dt_bias[o_], a_log[o_], d_skip[o_], b, s, ch)
            conv_p.append(pf.reshape(b, s, ODD_PAD)[:, s - (CONV_W - 1):, O_XS:O_XS + CONV_DIM])
            ssm_p.append(st)
            xp = _mlp(_odd_merge(xp, y, pf, ssm_norm[o_], wo, tm), norm_mlp[layer], wup, wdn, norm_final, tm, final)
            y_s, hn, st_s = _sample_ssd(sf[:, O_XS:O_XS + CONV_DIM].reshape(bs, 1, CONV_DIM),
                                        sf[:, O_DT:O_DT + LANE].reshape(bs, 1, LANE), state_conv[o_],
                                        conv_w[o_], conv_b[o_], dt_bias[o_], a_log[o_], d_skip[o_], state_ssm, o_)
            conv_s.append(hn)
            ssm_s.append(st_s)
            xs = _mlp(_odd_merge(xs, y_s.reshape(bs, D_INNER), sf, ssm_norm[o_], wo, bs),
                      norm_mlp[layer], wup, wdn, norm_final, bs, final)

    n_even = len(k_p)
    k_p = jnp.transpose(jnp.stack(k_p).reshape(n_even, b, H_A, DH_A, s), (0, 1, 4, 2, 3))
    v_p = jnp.transpose(jnp.stack(v_p).reshape(n_even, b, H_A, DH_A, s), (0, 1, 4, 2, 3))
    ik_p = jnp.transpose(jnp.stack(ik_p), (0, 1, 3, 2))
    return (xp.reshape(b, s, d), xs.reshape(bs, 1, d), k_p, v_p, ik_p,
            jnp.stack(k_s), jnp.stack(v_s), jnp.stack(ik_s), jnp.stack(ret_p), jnp.stack(ret_s),
            jnp.stack(conv_p), jnp.stack(conv_s), jnp.stack(ssm_p), jnp.stack(ssm_s))
```

```python
import functools
import math

import jax
import jax.numpy as jnp
import numpy as np
from jax import lax
from jax.experimental import pallas as pl
from jax.experimental.pallas import tpu as pltpu

F32 = jnp.float32
BF16 = jnp.bfloat16
I32 = jnp.int32

D_MODEL = 1024
DEPTH = 4
PAGE_SIZE = 128
DH_A = 64
H_A = (D_MODEL // 2) // DH_A
D_ATT = H_A * DH_A
H_IDX = 8
D_IDX = 64
TOPK_MAX = 256
NUM_BUCKETS = 32
MAX_DISTANCE = 128
H_R = 4
DV_R = (D_MODEL // 2) // H_R
DK_R = DV_R // 2
D_RK = H_R * DK_R
D_RV = H_R * DV_R
ROPE_BASE = 10000.0
D_INNER = 2 * D_MODEL
P_SSM = 64
NH_SSM = D_INNER // P_SSM
G_SSM = 8
HPG = NH_SSM // G_SSM
N_STATE = 128
CONV_W = 4
D_BC = G_SSM * N_STATE
CONV_DIM = D_INNER + 2 * D_BC
D_FF = 4 * D_MODEL
EPS = 1e-6

EVEN_SIZES = (D_ATT, D_ATT, D_ATT, H_IDX * D_IDX, D_IDX, H_IDX, D_RK, D_RK, D_RV, D_RV)
ODD_IN = D_INNER + CONV_DIM + NH_SSM

E_Q, E_K, E_V, E_IQ = 0, 512, 1024, 1536
E_RV, E_RG, E_RQ, E_RK, E_IK = 2048, 2560, 3072, 3328, 3584
E_IW = D_IDX
EVEN_PAD = 3712
O_Z, O_XS, O_BC, O_DT = 0, 2048, 4096, 6144
ODD_PAD = 6272

LANE = 128
VMEM_LIMIT = 56 * 1024 * 1024
NEG_BIG = -1e30
LOG2E = math.log2(math.e)
INT_MIN = -(2 ** 31)
KEY_NEG_INF = int(np.array(0xFF800000, np.uint32).view(np.int32)) ^ 0x7FFFFFFF

NT_DIMS = (((1,), (1,)), ((), ()))
TN_DIMS = (((0,), (0,)), ((), ()))


def _cparams(*sem):
    return pltpu.CompilerParams(dimension_semantics=sem, vmem_limit_bytes=VMEM_LIMIT)


def _const_spec(shape):
    nd = len(shape)
    return pl.BlockSpec(shape, lambda *_: (0,) * nd, pipeline_mode=pl.Buffered(1))


def _rms(x, g):
    return x * lax.rsqrt(jnp.mean(x * x, axis=-1, keepdims=True) + EPS) * g


def _silu(x):
    return x / (1.0 + jnp.exp(-x))


def _softplus(x):
    return jnp.maximum(x, 0.0) + jnp.log(1.0 + jnp.exp(-jnp.abs(x)))


def _rms_matmul_kernel(x_ref, g_ref, w_ref, of_ref, *ob_refs, bf_slices, n_chunk):
    h = _rms(x_ref[...], g_ref[...]).astype(BF16)
    n = w_ref.shape[1]
    for c0 in range(0, n, n_chunk):
        c1 = min(c0 + n_chunk, n)
        of_ref[:, c0:c1] = jnp.dot(h, w_ref[:, c0:c1], preferred_element_type=F32)
    for (start, width), r in zip(bf_slices, ob_refs):
        r[...] = of_ref[:, start:start + width].astype(BF16)


def _rms_matmul(x, g, w, tm, bf_slices=()):
    t, d = x.shape
    n = w.shape[1]
    out_shape = [jax.ShapeDtypeStruct((t, n), F32)]
    out_specs = [pl.BlockSpec((tm, n), lambda i: (i, 0))]
    for _, width in bf_slices:
        out_shape.append(jax.ShapeDtypeStruct((t, width), BF16))
        out_specs.append(pl.BlockSpec((tm, width), lambda i: (i, 0)))
    return pl.pallas_call(
        functools.partial(_rms_matmul_kernel, bf_slices=tuple(bf_slices), n_chunk=512),
        grid=(t // tm,),
        in_specs=[pl.BlockSpec((tm, d), lambda i: (i, 0)), _const_spec((1, d)), _const_spec((d, n))],
        out_specs=out_specs,
        out_shape=out_shape,
        compiler_params=_cparams("parallel"),
        name="rms_matmul",
    )(x, g.reshape(1, d), w)


def _even_proj_kernel(x_ref, g_ref, w_ref, wt_ref, of_ref, obq_ref, obi_ref, kt_ref, vt_ref, ikt_ref, vtb_ref, *,
                      n_chunk):
    h = _rms(x_ref[...], g_ref[...]).astype(BF16)
    n = w_ref.shape[1]
    for c0 in range(0, n, n_chunk):
        c1 = min(c0 + n_chunk, n)
        of_ref[:, c0:c1] = jnp.dot(h, w_ref[:, c0:c1], preferred_element_type=F32)
    obq_ref[...] = of_ref[:, 0:4 * D_ATT].astype(BF16)
    obi_ref[...] = of_ref[:, E_IK:E_IK + LANE].astype(BF16)
    kt_ref[...] = lax.dot_general(wt_ref[0:D_ATT, :], h, NT_DIMS, preferred_element_type=F32)
    vt = lax.dot_general(wt_ref[D_ATT:2 * D_ATT, :], h, NT_DIMS, preferred_element_type=F32)
    vt_ref[...] = vt
    vtb_ref[...] = vt.astype(BF16)
    ikt_ref[...] = lax.dot_general(wt_ref[2 * D_ATT:2 * D_ATT + D_IDX, :], h, NT_DIMS, preferred_element_type=F32)


def _even_proj_prompt(x, g, w, b, s, tm):
    t, d = x.shape
    n = w.shape[1]
    nb = s // tm
    wt = jnp.transpose(jnp.concatenate([w[:, E_K:E_K + D_ATT], w[:, E_V:E_V + D_ATT], w[:, E_IK:E_IK + D_IDX]],
                                       axis=1))
    row = lambda i: (i, 0)
    feat = lambda i: (i // nb, 0, i % nb)
    return pl.pallas_call(
        functools.partial(_even_proj_kernel, n_chunk=512),
        grid=(t // tm,),
        in_specs=[pl.BlockSpec((tm, d), row), _const_spec((1, d)), _const_spec((d, n)), _const_spec(wt.shape)],
        out_specs=[pl.BlockSpec((tm, n), row), pl.BlockSpec((tm, 4 * D_ATT), row), pl.BlockSpec((tm, LANE), row),
                   pl.BlockSpec((None, D_ATT, tm), feat), pl.BlockSpec((None, D_ATT, tm), feat),
                   pl.BlockSpec((None, D_IDX, tm), feat),
                   pl.BlockSpec((None, None, D_ATT, tm), lambda i: (i // nb, i % nb, 0, 0))],
        out_shape=[jax.ShapeDtypeStruct((t, n), F32), jax.ShapeDtypeStruct((t, 4 * D_ATT), BF16),
                   jax.ShapeDtypeStruct((t, LANE), BF16),
                   jax.ShapeDtypeStruct((b, D_ATT, s), F32), jax.ShapeDtypeStruct((b, D_ATT, s), F32),
                   jax.ShapeDtypeStruct((b, D_IDX, s), F32), jax.ShapeDtypeStruct((b, nb, D_ATT, tm), BF16)],
        compiler_params=_cparams("parallel"),
        name="even_proj_prompt",
    )(x, g.reshape(1, d), w, wt)


def _mlp_tail(x, gm_ref, wup_ref, wdn_ref, f_chunk=1024):
    h = _rms(x, gm_ref[...]).astype(BF16)
    acc = x
    for f in range(0, D_FF, f_chunk):
        u = jnp.dot(h, wup_ref[:, f:f + f_chunk], preferred_element_type=F32)
        u = jnp.maximum(u, 0.0)
        u = (u * u).astype(BF16)
        acc = acc + jnp.dot(u, wdn_ref[f:f + f_chunk, :], preferred_element_type=F32)
    return acc


def _even_merge_kernel(x_ref, att_ref, o_ref, rg_ref, wo_ref, out_ref):
    o = o_ref[...]
    parts = []
    for h in range(H_R):
        oh = o[:, h * DV_R:(h + 1) * DV_R]
        d = oh - jnp.mean(oh, axis=-1, keepdims=True)
        parts.append(d * lax.rsqrt(jnp.mean(d * d, axis=-1, keepdims=True) + EPS))
    ret = _silu(rg_ref[...]) * jnp.concatenate(parts, axis=1)
    x = x_ref[...]
    x = x + jnp.dot(att_ref[...].astype(BF16), wo_ref[0:D_ATT, :], preferred_element_type=F32)
    out_ref[...] = x + jnp.dot(ret.astype(BF16), wo_ref[D_ATT:D_ATT + D_RV, :], preferred_element_type=F32)


def _odd_merge_kernel(x_ref, y_ref, z_ref, nw_ref, wo_ref, out_ref):
    g = y_ref[...] * _silu(z_ref[...])
    gw = D_INNER // G_SSM
    parts = []
    for k in range(G_SSM):
        gk = g[:, k * gw:(k + 1) * gw]
        parts.append(gk * lax.rsqrt(jnp.mean(gk * gk, axis=-1, keepdims=True) + EPS))
    gn = (jnp.concatenate(parts, axis=1) * nw_ref[...]).astype(BF16)
    out_ref[...] = x_ref[...] + jnp.dot(gn, wo_ref[...], preferred_element_type=F32)


def _mlp_kernel(x_ref, gm_ref, wup_ref, wdn_ref, gf_ref, out_ref, *, final):
    x = _mlp_tail(x_ref[...], gm_ref, wup_ref, wdn_ref)
    if final:
        x = _rms(x, gf_ref[...])
    out_ref[...] = x


def _mlp(x, gm, wup, wdn, gf, tm, final):
    t, d = x.shape
    row = lambda i: (i, 0)
    return pl.pallas_call(
        functools.partial(_mlp_kernel, final=final),
        grid=(t // tm,),
        in_specs=[pl.BlockSpec((tm, d), row), _const_spec((1, d)), _const_spec(wup.shape), _const_spec(wdn.shape),
                  _const_spec((1, d))],
        out_specs=pl.BlockSpec((tm, d), row),
        out_shape=jax.ShapeDtypeStruct((t, d), F32),
        compiler_params=_cparams("parallel"),
        name="mlp",
    )(x, gm.reshape(1, d), wup, wdn, gf.reshape(1, d))


def _even_merge(x, att, o, pf, wo, tm):
    t, d = x.shape
    row = lambda i: (i, 0)
    return pl.pallas_call(
        _even_merge_kernel,
        grid=(t // tm,),
        in_specs=[pl.BlockSpec((tm, d), row), pl.BlockSpec((tm, D_ATT), row), pl.BlockSpec((tm, D_RV), row),
                  pl.BlockSpec((tm, D_RV), lambda i: (i, E_RG // D_RV)), _const_spec(wo.shape)],
        out_specs=pl.BlockSpec((tm, d), row),
        out_shape=jax.ShapeDtypeStruct((t, d), F32),
        compiler_params=_cparams("parallel"),
        name="even_merge",
    )(x, att, o, pf, wo)


def _odd_merge(x, y, pf, nw, wo, tm):
    t, d = x.shape
    row = lambda i: (i, 0)
    return pl.pallas_call(
        _odd_merge_kernel,
        grid=(t // tm,),
        in_specs=[pl.BlockSpec((tm, d), row), pl.BlockSpec((tm, D_INNER), row),
                  pl.BlockSpec((tm, D_INNER), lambda i: (i, O_Z // D_INNER)),
                  _const_spec((1, D_INNER)), _const_spec(wo.shape)],
        out_specs=pl.BlockSpec((tm, d), row),
        out_shape=jax.ShapeDtypeStruct((t, d), F32),
        compiler_params=_cparams("parallel"),
        name="odd_merge",
    )(x, y, pf, nw.reshape(1, D_INNER), wo)


FOLD = 32


def _fold_rows(x, op):
    n, w = x.shape
    if n > FOLD:
        x = op(x.reshape(n // FOLD, FOLD, w), axis=0)
    if x.shape[0] > 8:
        x = op(x.reshape(x.shape[0] // 8, 8, w), axis=0)
    return op(x, axis=0, keepdims=True)


def _score_key(score):
    bits = lax.bitcast_convert_type(score, I32)
    return bits ^ ((bits >> 31) & 0x7FFFFFFF)


def _kth_largest(count_ge, rows, k_sel):
    def bit_step(b, t):
        cand = t + lax.shift_left(jnp.int32(1), 31 - b)
        return jnp.where(count_ge(cand) >= k_sel, cand, t)
    return lax.fori_loop(0, 32, bit_step, jnp.full((rows, 1), INT_MIN, I32))


def _tie_limit(count_tie_lt, need, rows, n_bits):
    def bit_step(b, x):
        cand = x + lax.shift_left(jnp.int32(1), n_bits - 1 - b)
        return jnp.where(count_tie_lt(cand) < need, cand, x)
    return lax.fori_loop(0, n_bits, bit_step, jnp.zeros((rows, 1), I32))


def _attn_prompt_kernel(q_ref, iq_ref, iw_ref, k_ref, vt_ref, ik_ref, tb_ref, b31_ref, o_ref,
                        sk_ref, x_ref, m_ref, l_ref, acc_ref, s_ref, *, tq, tk, k_sel, idx_bits):
    i = pl.program_id(1)
    r = tk // tq
    nkc = (i + r) // r
    nfar = jnp.maximum(i - 1, 0) // r
    qidx = lax.broadcasted_iota(I32, (tk, tq), 1) + i * tq
    krow = lax.broadcasted_iota(I32, (tk, tq), 0)
    iw = iw_ref[...] * ((D_IDX ** -0.5) * (H_IDX ** -0.5))

    def key_fold(m):
        return jnp.sum(m.reshape(tk // FOLD, FOLD, tq), axis=0)

    def score_chunk(c, carry):
        ikc = ik_ref[c][:, 0:D_IDX]
        acc = jnp.zeros((tk, tq), F32)
        for h in range(H_IDX):
            s = lax.dot_general(ikc, iq_ref[:, h * D_IDX:(h + 1) * D_IDX], NT_DIMS, preferred_element_type=F32)
            acc = acc + iw[h:h + 1, :] * jnp.maximum(s, 0.0)
        sk_ref[c] = _score_key(jnp.where(krow + c * tk <= qidx, acc, -jnp.inf))
        return carry
    lax.fori_loop(0, nkc, score_chunk, 0)

    def count_ge(cand):
        def body(c, part):
            return part + key_fold(jnp.where(sk_ref[c] >= cand, 1.0, 0.0))
        part = lax.fori_loop(0, nkc, body, jnp.zeros((FOLD, tq), F32))
        return _fold_rows(part, jnp.sum)

    def kth_step(b, t):
        cand = t + lax.shift_left(jnp.int32(1), 31 - b)
        return jnp.where(count_ge(cand) >= float(k_sel), cand, t)
    thr = lax.fori_loop(0, 32, kth_step, jnp.full((1, tq), INT_MIN, I32))
    n_gt = count_ge(thr + 1)
    n_ge = count_ge(thr)
    need = float(k_sel) - n_gt
    excess = jnp.logical_and(n_ge - n_gt > need, thr > KEY_NEG_INF)
    x_ref[...] = jnp.full((1, tq), 2 ** idx_bits, I32)

    @pl.when(jnp.max(jnp.where(excess, 1.0, 0.0)) > 0.0)
    def _():
        def count_tie_lt(cand):
            def body(c, part):
                hit = jnp.logical_and(sk_ref[c] == thr, krow + c * tk < cand)
                return part + key_fold(jnp.where(hit, 1.0, 0.0))
            part = lax.fori_loop(0, nkc, body, jnp.zeros((FOLD, tq), F32))
            return _fold_rows(part, jnp.sum)

        def tie_step(b, x):
            cand = x + lax.shift_left(jnp.int32(1), idx_bits - 1 - b)
            return jnp.where(count_tie_lt(cand) < need, cand, x)
        lim = lax.fori_loop(0, idx_bits, tie_step, jnp.zeros((1, tq), I32))
        x_ref[...] = jnp.where(excess, lim, 2 ** idx_bits)

    xlim = x_ref[...]

    m_ref[...] = jnp.full(m_ref.shape, NEG_BIG, F32)
    l_ref[...] = jnp.zeros(l_ref.shape, F32)
    acc_ref[...] = jnp.zeros(acc_ref.shape, F32)

    def attend_chunk(c, near):
        key = sk_ref[c]
        kidx = krow + c * tk
        sel = jnp.logical_or(key > thr, jnp.logical_and(key == thr, kidx <= xlim))
        if near:
            sel = jnp.logical_and(sel, kidx <= qidx)
        amask = jnp.where(sel, 0.0, NEG_BIG)
        kc = k_ref[c]
        vtc = vt_ref[c]
        def logits(h):
            hs = slice(h * DH_A, (h + 1) * DH_A)
            qh = (q_ref[:, hs] * (DH_A ** -0.5 * LOG2E)).astype(BF16)
            return lax.dot_general(kc[:, hs], qh, NT_DIMS, preferred_element_type=F32)

        for h in range(H_A):
            s_ref[h] = logits(h)
        for h in range(H_A):
            b_far = b31_ref[0:1, h:h + 1]
            s = s_ref[h] + amask
            m_old = m_ref[h:h + 1, :]
            if near:
                start = pl.multiple_of(tk + c * tk - (i - 1) * tq, tq)
                s = s + tb_ref[h, pl.ds(start, tk), :]
                m_new = jnp.maximum(m_old, _fold_rows(s, jnp.max))
                shift = -m_new
            else:
                m_new = jnp.maximum(m_old, _fold_rows(s, jnp.max) + b_far)
                shift = b_far - m_new
            alpha = jnp.exp2(m_old - m_new)
            p = jnp.exp2(s + shift)
            l_ref[h:h + 1, :] = alpha * l_ref[h:h + 1, :] + _fold_rows(p, jnp.sum)
            m_ref[h:h + 1, :] = m_new
            hs = slice(h * DH_A, (h + 1) * DH_A)
            pv = jnp.dot(vtc[hs, :], p.astype(BF16), preferred_element_type=F32)
            acc_ref[hs, :] = alpha * acc_ref[hs, :] + pv

    def far_body(c, carry):
        attend_chunk(c, False)
        return carry

    def near_body(c, carry):
        attend_chunk(c, True)
        return carry

    lax.fori_loop(0, nfar, far_body, 0)
    lax.fori_loop(nfar, nkc, near_body, 0)

    for h in range(H_A):
        hs = slice(h * DH_A, (h + 1) * DH_A)
        acc_ref[hs, :] = acc_ref[hs, :] / l_ref[h:h + 1, :]
    o_ref[...] = jnp.transpose(acc_ref[...])


def _attn_prompt(pf, pbq, pbi, vt4, tb_t, b31, b, s, tq, tk):
    nq, nkc = s // tq, s // tk
    k_sel = min(TOPK_MAX, s // 4)
    idx_bits = int(math.log2(s))
    assert 2 ** idx_bits == s and tk % tq == 0 and tq % LANE == 0
    pbq4 = pbq.reshape(b, nkc, tk, pbq.shape[1])
    pbi4 = pbi.reshape(b, nkc, tk, LANE)
    tb_t = jnp.concatenate([jnp.broadcast_to(b31[0, :H_A, None, None], (H_A, tk, tq)), tb_t,
                            jnp.zeros((H_A, tk - tq, tq), F32)], axis=1)
    assert vt4.shape == (b, nkc, D_ATT, tk)
    iw_t = jnp.transpose(pf[:, E_IK + E_IW:E_IK + E_IW + H_IDX])
    return pl.pallas_call(
        functools.partial(_attn_prompt_kernel, tq=tq, tk=tk, k_sel=k_sel, idx_bits=idx_bits),
        grid=(b, nq),
        in_specs=[
            pl.BlockSpec((tq, D_ATT), lambda bb, i: (bb * nq + i, E_Q // D_ATT)),
            pl.BlockSpec((tq, D_ATT), lambda bb, i: (bb * nq + i, E_IQ // D_ATT)),
            pl.BlockSpec((H_IDX, tq), lambda bb, i: (0, bb * nq + i)),
            pl.BlockSpec((None, nkc, tk, D_ATT), lambda bb, i: (bb, 0, 0, E_K // D_ATT), pipeline_mode=pl.Buffered(1)),
            pl.BlockSpec((None, nkc, D_ATT, tk), lambda bb, i: (bb, 0, 0, 0), pipeline_mode=pl.Buffered(1)),
            pl.BlockSpec((None, nkc, tk, LANE), lambda bb, i: (bb, 0, 0, 0), pipeline_mode=pl.Buffered(1)),
            _const_spec(tb_t.shape), _const_spec(b31.shape),
        ],
        out_specs=pl.BlockSpec((tq, D_ATT), lambda bb, i: (bb * nq + i, 0)),
        out_shape=jax.ShapeDtypeStruct((b * s, D_ATT), F32),
        scratch_shapes=[pltpu.VMEM((nkc, tk, tq), I32), pltpu.VMEM((1, tq), I32),
                        pltpu.VMEM((H_A, tq), F32), pltpu.VMEM((H_A, tq), F32),
                        pltpu.VMEM((D_ATT, tq), F32),
                        pltpu.VMEM((H_A, tk, tq), F32)],
        compiler_params=_cparams("parallel", "arbitrary"),
        name="attn_prompt",
    )(pf, pbq, iw_t, pbq4, vt4, pbi4, tb_t * LOG2E, b31 * LOG2E)


def _rotate_half(x, cos_t, sin_t):
    half = DK_R // 2
    width = x.shape[1]
    lane = lax.broadcasted_iota(I32, x.shape, 1)
    partner = jnp.where(lane % DK_R < half, pltpu.roll(x, width - half, 1), pltpu.roll(x, half, 1))
    return x * cos_t + partner * sin_t


def _ret_prompt_kernel(rq_ref, rk_ref, rv_ref, cos_ref, sin_ref, dm_ref, cd_ref, kd_ref, gl_ref,
                       o_ref, st_ref, s_ref):
    c = pl.program_id(1)

    @pl.when(c == 0)
    def _():
        s_ref[...] = jnp.zeros(s_ref.shape, F32)

    cos_t, sin_t = cos_ref[...], sin_ref[...]
    rq = _rotate_half(rq_ref[...], cos_t, sin_t)
    rk = _rotate_half(rk_ref[...], cos_t, sin_t) * (DK_R ** -0.5)
    rv = rv_ref[...]
    for h in range(H_R):
        qh = rq[:, h * DK_R:(h + 1) * DK_R].astype(BF16)
        kh = rk[:, h * DK_R:(h + 1) * DK_R]
        vh = rv[:, h * DV_R:(h + 1) * DV_R].astype(BF16)
        st = s_ref[h]
        scores = lax.dot_general(qh, kh.astype(BF16), NT_DIMS, preferred_element_type=F32) * dm_ref[h]
        inner = jnp.dot(scores.astype(BF16), vh, preferred_element_type=F32)
        cross = jnp.dot(qh, st.astype(BF16), preferred_element_type=F32) * cd_ref[:, h:h + 1]
        o_ref[:, h * DV_R:(h + 1) * DV_R] = inner + cross
        kdec = (kh * kd_ref[:, h:h + 1]).astype(BF16)
        upd = lax.dot_general(kdec, vh, TN_DIMS, preferred_element_type=F32)
        s_ref[h] = st * gl_ref[0:1, h:h + 1] + upd

    @pl.when(c == pl.num_programs(1) - 1)
    def _():
        st_ref[...] = s_ref[...]


def _ret_tables(ch):
    lg = jnp.log(1.0 - jnp.exp2(-5.0 - jnp.arange(H_R, dtype=F32)))
    n = jnp.arange(ch, dtype=F32)
    diff = n[:, None] - n[None, :]
    causal = diff >= 0
    dm = jnp.where(causal[None], jnp.exp(jnp.where(causal, diff, 0.0)[None] * lg[:, None, None]), 0.0)
    cd = jnp.exp((n[:, None] + 1.0) * lg[None, :])
    kd = jnp.exp((ch - 1.0 - n)[:, None] * lg[None, :])
    gl = jnp.exp(ch * lg)[None, :]
    pad = lambda a: jnp.pad(a, ((0, 0), (0, LANE - H_R)))
    return dm, pad(cd), pad(kd), pad(gl)


def _rope_tables(pos):
    half = DK_R // 2
    inv = ROPE_BASE ** (-jnp.arange(half, dtype=F32) / half)
    ang = pos.astype(F32)[:, None] * inv[None, :]
    cos, sin = jnp.cos(ang), jnp.sin(ang)
    cos_t = jnp.tile(jnp.concatenate([cos, cos], axis=1), (1, H_R))
    sin_t = jnp.tile(jnp.concatenate([-sin, sin], axis=1), (1, H_R))
    return cos_t, sin_t


def _ret_prompt(pf, cos_t, sin_t, b, s, ch):
    nc = s // ch
    dm, cd, kd, gl = _ret_tables(ch)
    tok = lambda col: (lambda bb, c: (bb * nc + c, col))
    return pl.pallas_call(
        _ret_prompt_kernel,
        grid=(b, nc),
        in_specs=[pl.BlockSpec((ch, D_RK), tok(E_RQ // D_RK)), pl.BlockSpec((ch, D_RK), tok(E_RK // D_RK)),
                  pl.BlockSpec((ch, D_RV), tok(E_RV // D_RV)),
                  pl.BlockSpec((ch, D_RK), lambda bb, c: (c, 0)), pl.BlockSpec((ch, D_RK), lambda bb, c: (c, 0)),
                  _const_spec(dm.shape), _const_spec(cd.shape), _const_spec(kd.shape), _const_spec(gl.shape)],
        out_specs=[pl.BlockSpec((ch, D_RV), lambda bb, c: (bb * nc + c, 0)),
                   pl.BlockSpec((None, H_R, DK_R, DV_R), lambda bb, c: (bb, 0, 0, 0))],
        out_shape=[jax.ShapeDtypeStruct((b * s, D_RV), F32), jax.ShapeDtypeStruct((b, H_R, DK_R, DV_R), F32)],
        scratch_shapes=[pltpu.VMEM((H_R, DK_R, DV_R), F32)],
        compiler_params=_cparams("parallel", "arbitrary"),
        name="retention_prompt",
    )(pf, pf, pf, cos_t, sin_t, dm, cd, kd, gl)


def _shift_rows(x, prev8, j):
    rolled = pltpu.roll(x, j, 0)
    rid = lax.broadcasted_iota(I32, prev8.shape, 0)
    head = jnp.where(rid < j, pltpu.roll(prev8, j, 0), rolled[0:8])
    return jnp.concatenate([head, rolled[8:]], axis=0)


def _conv_silu(x, prev8, w_ref, b_ref, col0, width):
    out = b_ref[0:1, col0:col0 + width] + x * w_ref[CONV_W - 1:CONV_W, col0:col0 + width]
    for j in range(1, CONV_W):
        out = out + _shift_rows(x, prev8, j) * w_ref[CONV_W - 1 - j:CONV_W - j, col0:col0 + width]
    return _silu(out)


def _split_pack(x, pieces):
    lane = lax.broadcasted_iota(I32, x.shape, 1)
    rest = jnp.where(lane < NH_SSM, x, 0.0)
    packed = jnp.zeros(x.shape, F32)
    for k in range(pieces):
        piece = rest.astype(BF16).astype(F32)
        rest = rest - piece
        packed = packed + (piece if k == 0 else pltpu.roll(piece, k * NH_SSM, 1))
    return packed.astype(BF16)


def _ssd_prompt_kernel(xs_ref, bc_ref, dt_ref, cw_ref, cb_ref, dtb_ref, alog_ref, dskx_ref, tri_ref, ex_ref, sw_ref,
                       y_ref, st_ref, s_ref, pxs_ref, pbc_ref, w_ref, *, ch):
    c = pl.program_id(1)

    @pl.when(c == 0)
    def _():
        s_ref[...] = jnp.zeros(s_ref.shape, F32)
        pxs_ref[...] = jnp.zeros(pxs_ref.shape, F32)
        pbc_ref[...] = jnp.zeros(pbc_ref.shape, F32)

    xs_raw = xs_ref[...]
    bc_raw = bc_ref[...]
    xs = _conv_silu(xs_raw, pxs_ref[...], cw_ref, cb_ref, 0, D_INNER)
    bc = _conv_silu(bc_raw, pbc_ref[...], cw_ref, cb_ref, D_INNER, 2 * D_BC).astype(BF16)
    pxs_ref[...] = xs_raw[ch - 8:ch]
    pbc_ref[...] = bc_raw[ch - 8:ch]

    dt = _softplus(dt_ref[...] + dtb_ref[...])
    a = -jnp.exp(alog_ref[...])
    acs = jnp.dot(tri_ref[...], dt * a, preferred_element_type=F32, precision=lax.Precision.HIGHEST)
    acs_t = jnp.transpose(acs)
    dec_end = jnp.exp(acs[ch - 1:ch, :] - acs)

    def expand(x):
        return jnp.dot(_split_pack(x, 2), ex_ref[...], preferred_element_type=F32)
    dt_x = expand(dt)
    e_x = expand(jnp.exp(acs))
    xdt = (xs * dt_x).astype(BF16)
    xd = (xs * expand(dt * dec_end)).astype(BF16)

    seg_t = jnp.dot(_split_pack(acs, 3), sw_ref[...], preferred_element_type=F32)
    tri = tri_ref[...] > 0.5
    for g in range(G_SSM):
        bm = bc[:, g * N_STATE:(g + 1) * N_STATE]
        cm = bc[:, D_BC + g * N_STATE:D_BC + (g + 1) * N_STATE]
        cb = lax.dot_general(cm, bm, NT_DIMS, preferred_element_type=F32)
        for hh in range(HPG):
            h = g * HPG + hh
            seg = seg_t[:, h * ch:(h + 1) * ch] - acs_t[h:h + 1, :]
            w_ref[h] = (cb * jnp.exp(jnp.where(tri, seg, -jnp.inf))).astype(BF16)

    e_last_x = e_x[ch - 1:ch, :]
    gw = HPG * P_SSM
    for g in range(G_SSM):
        gs = slice(g * gw, (g + 1) * gw)
        bm = bc[:, g * N_STATE:(g + 1) * N_STATE]
        cm = bc[:, D_BC + g * N_STATE:D_BC + (g + 1) * N_STATE]
        st = s_ref[g]
        y_off = jnp.dot(cm, st.astype(BF16), preferred_element_type=F32)
        y_diag = jnp.concatenate(
            [jnp.dot(w_ref[g * HPG + hh], xdt[:, (g * HPG + hh) * P_SSM:(g * HPG + hh + 1) * P_SSM],
                     preferred_element_type=F32) for hh in range(HPG)], axis=1)
        y_ref[:, gs] = y_diag + y_off * e_x[:, gs] + dskx_ref[:, gs] * xs[:, gs]
        upd = lax.dot_general(bm, xd[:, gs], TN_DIMS, preferred_element_type=F32)
        s_ref[g] = st * e_last_x[:, gs] + upd

    @pl.when(c == pl.num_programs(1) - 1)
    def _():
        for g in range(G_SSM):
            st_ref[g * HPG:(g + 1) * HPG] = jnp.transpose(s_ref[g]).reshape(HPG, P_SSM, N_STATE)


def _pad_lanes(v):
    return jnp.pad(v.reshape(1, -1), ((0, 0), (0, LANE - v.shape[-1])))


def _ssd_prompt(pf, cw, cb, dtb, alog, dsk, b, s, ch):
    nc = s // ch
    tri = jnp.tril(jnp.ones((ch, ch), F32))
    lane_head = jnp.arange(LANE) % NH_SSM
    ex = ((lane_head[:, None] == jnp.arange(D_INNER)[None, :] // P_SSM)
          & (jnp.arange(LANE)[:, None] < 2 * NH_SSM)).astype(BF16)
    sw = ((lane_head[:, None] == jnp.arange(NH_SSM * ch)[None, :] // ch)
          & (jnp.arange(LANE)[:, None] < 3 * NH_SSM)).astype(BF16)
    dskx = jnp.repeat(dsk, P_SSM).reshape(1, D_INNER)
    tok = lambda col: (lambda bb, c: (bb * nc + c, col))
    return pl.pallas_call(
        functools.partial(_ssd_prompt_kernel, ch=ch),
        grid=(b, nc),
        in_specs=[pl.BlockSpec((ch, D_INNER), tok(O_XS // D_INNER)), pl.BlockSpec((ch, 2 * D_BC), tok(O_BC // (2 * D_BC))),
                  pl.BlockSpec((ch, LANE), tok(O_DT // LANE)),
                  _const_spec(cw.shape), _const_spec((1, CONV_DIM)), _const_spec((1, LANE)), _const_spec((1, LANE)),
                  _const_spec((1, D_INNER)), _const_spec((ch, ch)), _const_spec(ex.shape), _const_spec(sw.shape)],
        out_specs=[pl.BlockSpec((ch, D_INNER), lambda bb, c: (bb * nc + c, 0)),
                   pl.BlockSpec((None, NH_SSM, P_SSM, N_STATE), lambda bb, c: (bb, 0, 0, 0))],
        out_shape=[jax.ShapeDtypeStruct((b * s, D_INNER), F32),
                   jax.ShapeDtypeStruct((b, NH_SSM, P_SSM, N_STATE), F32)],
        scratch_shapes=[pltpu.VMEM((G_SSM, N_STATE, HPG * P_SSM), F32), pltpu.VMEM((8, D_INNER), F32),
                        pltpu.VMEM((8, 2 * D_BC), F32), pltpu.VMEM((NH_SSM, ch, ch), BF16)],
        compiler_params=_cparams("parallel", "arbitrary"),
        name="ssd_prompt",
    )(pf, pf, pf, cw, cb.reshape(1, CONV_DIM), _pad_lanes(dtb), _pad_lanes(alog), dskx, tri, ex, sw)


def _row_to_col(row, n):
    eye = lax.broadcasted_iota(I32, (n, n), 0) == lax.broadcasted_iota(I32, (n, n), 1)
    return jnp.sum(jnp.where(eye, jnp.broadcast_to(row, (n, n)), 0.0), axis=1, keepdims=True)


def _col_to_row(col, n):
    eye = lax.broadcasted_iota(I32, (n, n), 0) == lax.broadcasted_iota(I32, (n, n), 1)
    return jnp.sum(jnp.where(eye, jnp.broadcast_to(col, (n, n)), 0.0), axis=0, keepdims=True)


def _sample_scores_kernel(pt_ref, iq_ref, iw_ref, *refs, pg):
    page_refs, o_ref = refs[:pg], refs[pg]
    iq = iq_ref[...]
    iw = iw_ref[...] * ((D_IDX ** -0.5) * (H_IDX ** -0.5))
    for u in range(pg):
        s = jnp.dot(iq, page_refs[u][...].astype(BF16), preferred_element_type=F32)
        o_ref[u:u + 1, :] = jnp.sum(iw * jnp.maximum(s, 0.0), axis=0, keepdims=True)


def _sample_scores(iq, iw, cik_t, e, page_table, pg):
    bs, n_pages = page_table.shape
    page_spec = lambda u: pl.BlockSpec((None, None, D_IDX, PAGE_SIZE),
                                       lambda b, j, pt: (e, pt[b, j * pg + u], 0, 0))
    grid_spec = pltpu.PrefetchScalarGridSpec(
        num_scalar_prefetch=1, grid=(bs, n_pages // pg),
        in_specs=[pl.BlockSpec((None, H_IDX, D_IDX), lambda b, j, pt: (b, 0, 0)),
                  pl.BlockSpec((None, H_IDX, 1), lambda b, j, pt: (b, 0, 0))] + [page_spec(u) for u in range(pg)],
        out_specs=pl.BlockSpec((None, pg, PAGE_SIZE), lambda b, j, pt: (b, j, 0)))
    return pl.pallas_call(
        functools.partial(_sample_scores_kernel, pg=pg),
        grid_spec=grid_spec,
        out_shape=jax.ShapeDtypeStruct((bs, n_pages, PAGE_SIZE), F32),
        compiler_params=_cparams("parallel", "arbitrary"),
        name="sample_scores",
    )(page_table, iq, iw, *([cik_t] * pg))


def _sample_select_kernel(sc_ref, iq_ref, ikw_ref, sel_ref, seln_ref, *, k_sel, idx_bits):
    bs, n_keys = sc_ref.shape
    ikw = ikw_ref[...]
    w = ikw[:, E_IW:E_IW + H_IDX] * ((D_IDX ** -0.5) * (H_IDX ** -0.5))
    sc_new = jnp.zeros((bs, 1), F32)
    for h in range(H_IDX):
        s_h = jnp.sum(iq_ref[:, h * D_IDX:(h + 1) * D_IDX] * ikw[:, 0:D_IDX], axis=1, keepdims=True)
        sc_new = sc_new + w[:, h:h + 1] * jnp.maximum(s_h, 0.0)
    key_new = _score_key(sc_new)
    keys = _score_key(sc_ref[...])
    kidx = lax.broadcasted_iota(I32, keys.shape, 1)

    def total(mask, extra):
        m = jnp.where(mask, 1.0, 0.0)
        part = m[:, 0:LANE]
        for t in range(1, n_keys // LANE):
            part = part + m[:, t * LANE:(t + 1) * LANE]
        return jnp.sum(part, axis=1, keepdims=True) + jnp.where(extra, 1.0, 0.0)

    def count_ge(cand):
        return total(keys >= cand, key_new >= cand)

    thr = _kth_largest(count_ge, bs, float(k_sel))
    need = float(k_sel) - count_ge(thr + 1)

    def count_tie_lt(cand):
        return total(jnp.logical_and(keys == thr, kidx < cand), jnp.logical_and(key_new == thr, n_keys < cand))

    lim = _tie_limit(count_tie_lt, need, bs, idx_bits)
    sel = jnp.logical_or(keys > thr, jnp.logical_and(keys == thr, kidx <= lim))
    sel_ref[...] = jnp.where(sel, 1.0, 0.0)
    sel_new = jnp.logical_or(key_new > thr, jnp.logical_and(key_new == thr, n_keys <= lim))
    seln_ref[...] = jnp.broadcast_to(jnp.where(sel_new, 1.0, 0.0), seln_ref.shape)


def _sample_select(scores, iqf, ikw):
    bs, n_keys = scores.shape
    k_sel = min(TOPK_MAX, (n_keys + 1) // 4)
    idx_bits = int(math.log2(n_keys)) + 1
    full = lambda a: pl.BlockSpec(a.shape, lambda: (0,) * a.ndim)
    return pl.pallas_call(
        functools.partial(_sample_select_kernel, k_sel=k_sel, idx_bits=idx_bits),
        in_specs=[full(scores), full(iqf), full(ikw)],
        out_specs=[pl.BlockSpec((bs, n_keys), lambda: (0, 0)), pl.BlockSpec((bs, LANE), lambda: (0, 0))],
        out_shape=[jax.ShapeDtypeStruct((bs, n_keys), F32), jax.ShapeDtypeStruct((bs, LANE), F32)],
        compiler_params=pltpu.CompilerParams(vmem_limit_bytes=VMEM_LIMIT),
        name="sample_select",
    )(scores, iqf, ikw)


def _sample_attn_kernel(pt_ref, sel_ref, seln_ref, q_ref, kn_ref, vn_ref, tbs_ref, b31_ref, b0_ref,
                        *refs, pg, n_pages):
    k_refs, v_refs = refs[:pg], refs[pg:2 * pg]
    o_ref, m_ref, l_ref, acc_ref = refs[2 * pg:]
    j = pl.program_id(1)
    qcol = q_ref[...] * (DH_A ** -0.5)

    def head_sum(x):
        return jnp.concatenate([jnp.sum(x[h * DH_A:(h + 1) * DH_A], axis=0, keepdims=True) for h in range(H_A)],
                               axis=0)

    def head_expand(x, w):
        return jnp.concatenate([jnp.broadcast_to(x[h:h + 1, :], (DH_A, w)) for h in range(H_A)], axis=0)

    @pl.when(j == 0)
    def _():
        sel_new = seln_ref[0:1, 0:1] > 0.5
        lg_new = head_sum(qcol * kn_ref[...]) + b0_ref[...]
        m_ref[...] = jnp.broadcast_to(jnp.where(sel_new, lg_new, NEG_BIG), m_ref.shape)
        l0 = jnp.broadcast_to(jnp.where(sel_new, 1.0, 0.0), (H_A, 1))
        l_ref[...] = jnp.broadcast_to(l0, l_ref.shape)
        first_lane = lax.broadcasted_iota(I32, (D_ATT, PAGE_SIZE), 1) == 0
        acc_ref[...] = jnp.where(first_lane, head_expand(l0, PAGE_SIZE) * vn_ref[...], 0.0)

    qb = jnp.broadcast_to(qcol, (D_ATT, PAGE_SIZE))
    logits = []
    for u in range(pg):
        p_idx = j * pg + u
        s = head_sum(k_refs[u][...] * qb)
        s = s + jnp.where(p_idx == n_pages - 1, tbs_ref[...], b31_ref[...])
        logits.append(jnp.where(sel_ref[pl.ds(p_idx, 1), :] > 0.5, s, NEG_BIG))
    m_old = m_ref[:, 0:1]
    m_new = m_old
    for s in logits:
        m_new = jnp.maximum(m_new, jnp.max(s, axis=1, keepdims=True))
    alpha = jnp.exp(m_old - m_new)
    l_new = alpha * l_ref[:, 0:1]
    acc = head_expand(alpha, PAGE_SIZE) * acc_ref[...]
    for u, s in enumerate(logits):
        p = jnp.exp(s - m_new)
        l_new = l_new + jnp.sum(p, axis=1, keepdims=True)
        acc = acc + v_refs[u][...] * head_expand(p, PAGE_SIZE)
    acc_ref[...] = acc
    m_ref[...] = jnp.broadcast_to(m_new, m_ref.shape)
    l_ref[...] = jnp.broadcast_to(l_new, l_ref.shape)

    @pl.when(j == pl.num_programs(1) - 1)
    def _():
        o_ref[...] = jnp.sum(acc_ref[...], axis=1, keepdims=True) / head_expand(l_ref[:, 0:1], 1)


def _sample_attn(sel, sel_new, q, k_new, v_new, ck_t, cv_t, e, page_table, tbs, b31c, b0c, pg):
    bs, n_pages = page_table.shape
    per_b = lambda shape: pl.BlockSpec((None,) + shape, lambda b, j, pt: (b,) + (0,) * len(shape))
    page_spec = lambda u: pl.BlockSpec((None, None, D_ATT, PAGE_SIZE),
                                       lambda b, j, pt: (e, pt[b, j * pg + u], 0, 0))
    const = lambda shape: pl.BlockSpec(shape, lambda b, j, pt: (0,) * len(shape))
    grid_spec = pltpu.PrefetchScalarGridSpec(
        num_scalar_prefetch=1, grid=(bs, n_pages // pg),
        in_specs=[per_b((n_pages, PAGE_SIZE)), per_b((1, LANE)), per_b((D_ATT, 1)), per_b((D_ATT, 1)),
                  per_b((D_ATT, 1)), const(tbs.shape), const(b31c.shape), const(b0c.shape)]
        + [page_spec(u) for u in range(pg)] + [page_spec(u) for u in range(pg)],
        out_specs=per_b((D_ATT, 1)),
        scratch_shapes=[pltpu.VMEM((H_A, LANE), F32), pltpu.VMEM((H_A, LANE), F32),
                        pltpu.VMEM((D_ATT, PAGE_SIZE), F32)])
    return pl.pallas_call(
        functools.partial(_sample_attn_kernel, pg=pg, n_pages=n_pages),
        grid_spec=grid_spec,
        out_shape=jax.ShapeDtypeStruct((bs, D_ATT, 1), F32),
        compiler_params=_cparams("parallel", "arbitrary"),
        name="sample_attn",
    )(page_table, sel, sel_new, q, k_new, v_new, tbs, b31c, b0c, *([ck_t] * pg), *([cv_t] * pg))


def _sample_ret_kernel(rq_ref, rk_ref, rv_ref, cos_ref, sin_ref, g_ref, s_ref, o_ref, sn_ref):
    cos_t, sin_t = cos_ref[...], sin_ref[...]
    rq = _rotate_half(rq_ref[...], cos_t, sin_t)
    rk = _rotate_half(rk_ref[...], cos_t, sin_t) * (DK_R ** -0.5)
    rv = rv_ref[...]
    for h in range(H_R):
        qc = _row_to_col(rq[:, h * DK_R:(h + 1) * DK_R], DK_R)
        kc = _row_to_col(rk[:, h * DK_R:(h + 1) * DK_R], DK_R)
        vh = rv[:, h * DV_R:(h + 1) * DV_R]
        s_new = s_ref[h] * g_ref[0:1, h:h + 1] + kc * vh
        sn_ref[h] = s_new
        o_ref[:, h * DV_R:(h + 1) * DV_R] = jnp.sum(qc * s_new, axis=0, keepdims=True)


def _layer_b_spec(layer, shape):
    return pl.BlockSpec((None, None) + shape, lambda b: (layer, b) + (0,) * len(shape))


def _sample_ret(rq, rk, rv, cos_t, sin_t, state, e):
    bs = rq.shape[0]
    gamma = _pad_lanes(1.0 - jnp.exp2(-5.0 - jnp.arange(H_R, dtype=F32)))
    per_b = lambda shape: pl.BlockSpec((None,) + shape, lambda b: (b,) + (0,) * len(shape))
    return pl.pallas_call(
        _sample_ret_kernel,
        grid=(bs,),
        in_specs=[per_b((1, D_RK)), per_b((1, D_RK)), per_b((1, D_RV)), _const_spec((1, D_RK)), _const_spec((1, D_RK)),
                  _const_spec((1, LANE)), _layer_b_spec(e, (H_R, DK_R, DV_R))],
        out_specs=[per_b((1, D_RV)), per_b((H_R, DK_R, DV_R))],
        out_shape=[jax.ShapeDtypeStruct((bs, 1, D_RV), F32), jax.ShapeDtypeStruct(state.shape[1:], F32)],
        compiler_params=_cparams("parallel"),
        name="sample_retention",
    )(rq, rk, rv, cos_t, sin_t, gamma, state)


def _sample_conv_kernel(xbc_ref, dt_ref, hist_ref, cw_ref, cb_ref, dtb_ref, alog_ref, act_ref, hn_ref, dtd_ref):
    x_new = xbc_ref[...]
    conv = cb_ref[...] + x_new * cw_ref[CONV_W - 1:CONV_W, :]
    for j in range(CONV_W - 1):
        conv = conv + hist_ref[j] * cw_ref[j:j + 1, :]
    act_ref[...] = _silu(conv)
    for j in range(CONV_W - 2):
        hn_ref[j] = hist_ref[j + 1]
    hn_ref[CONV_W - 2] = x_new
    dt = _softplus(dt_ref[...] + dtb_ref[...])
    dtd_ref[0] = dt
    dtd_ref[1] = jnp.exp(dt * (-jnp.exp(alog_ref[...])))


def _sample_state_kernel(xt_ref, bc_ref, dtd_ref, dsk_ref, s_ref, yt_ref, sn_ref):
    xs_t = xt_ref[...]
    dt = dtd_ref[0:1, 0:NH_SSM]
    d_a = dtd_ref[1:2, :]
    xdt_t = xs_t * dt
    lane = lax.broadcasted_iota(I32, xs_t.shape, 1)
    y_t = dsk_ref[:, 0:NH_SSM] * xs_t
    for g in range(G_SSM):
        bm = bc_ref[:, g * N_STATE:(g + 1) * N_STATE]
        cm = bc_ref[:, D_BC + g * N_STATE:D_BC + (g + 1) * N_STATE]
        for hh in range(HPG):
            h = g * HPG + hh
            s_new = s_ref[h] * d_a[0:1, h:h + 1] + xdt_t[:, h:h + 1] * bm
            sn_ref[h] = s_new
            yc = jnp.sum(s_new * cm, axis=1, keepdims=True)
            y_t = y_t + jnp.where(lane == h, yc, 0.0)
    yt_ref[...] = y_t


def _sample_ssd(xbc, dt, hist_t, cw, cb, dtb, alog, dsk, state, layer):
    bs = xbc.shape[0]
    full = lambda shape: pl.BlockSpec(shape, lambda i: (0,) * len(shape))
    act, hn_t, dtd = pl.pallas_call(
        _sample_conv_kernel,
        grid=(1,),
        in_specs=[full((bs, CONV_DIM)), full((bs, LANE)),
                  pl.BlockSpec((None, CONV_W - 1, bs, CONV_DIM), lambda i: (layer, 0, 0, 0)),
                  full(cw.shape), full((1, CONV_DIM)), full((1, LANE)), full((1, LANE))],
        out_specs=[full((bs, CONV_DIM)), full((CONV_W - 1, bs, CONV_DIM)), full((2, bs, LANE))],
        out_shape=[jax.ShapeDtypeStruct((bs, CONV_DIM), F32), jax.ShapeDtypeStruct((CONV_W - 1, bs, CONV_DIM), F32),
                   jax.ShapeDtypeStruct((2, bs, LANE), F32)],
        compiler_params=pltpu.CompilerParams(vmem_limit_bytes=VMEM_LIMIT),
        name="sample_conv",
    )(xbc, dt, hist_t, cw, cb.reshape(1, CONV_DIM), _pad_lanes(dtb), _pad_lanes(alog))
    xs_t = jnp.swapaxes(act[:, :D_INNER].reshape(bs, NH_SSM, P_SSM), 1, 2)
    bc = act[:, D_INNER:].reshape(bs, 1, 2 * D_BC)
    dtd_b = jnp.swapaxes(dtd, 0, 1)
    per_b = lambda shape: pl.BlockSpec((None,) + shape, lambda b: (b,) + (0,) * len(shape))
    y_t, st = pl.pallas_call(
        _sample_state_kernel,
        grid=(bs,),
        in_specs=[per_b((P_SSM, NH_SSM)), per_b((1, 2 * D_BC)), per_b((2, LANE)), _const_spec((1, LANE)),
                  _layer_b_spec(layer, (NH_SSM, P_SSM, N_STATE))],
        out_specs=[per_b((P_SSM, NH_SSM)), per_b((NH_SSM, P_SSM, N_STATE))],
        out_shape=[jax.ShapeDtypeStruct((bs, P_SSM, NH_SSM), F32), jax.ShapeDtypeStruct(state.shape[1:], F32)],
        compiler_params=_cparams("parallel"),
        name="sample_state",
    )(xs_t, bc, dtd_b, _pad_lanes(dsk), state)
    return jnp.swapaxes(y_t, 1, 2).reshape(bs, D_INNER), hn_t, st


def _even_weight(w):
    q, k, v, iq, ik, iw, rq, rk, rv, rg = jnp.split(w, np.cumsum(EVEN_SIZES)[:-1].tolist(), axis=1)
    pad = jnp.zeros((w.shape[0], EVEN_PAD - sum(EVEN_SIZES)), w.dtype)
    return jnp.concatenate([q, k, v, iq, rv, rg, rq, rk, ik, iw, pad], axis=1).astype(BF16)


def _odd_weight(w):
    return jnp.pad(w, ((0, 0), (0, ODD_PAD - ODD_IN))).astype(BF16)


def _t5_bucket(dist):
    max_exact = NUM_BUCKETS // 2
    d = jnp.maximum(dist, 0)
    ratio = jnp.log(jnp.maximum(d, max_exact).astype(F32) / max_exact) / math.log(MAX_DISTANCE / max_exact)
    large = max_exact + (ratio * (NUM_BUCKETS - max_exact)).astype(I32)
    return jnp.where(d < max_exact, d, jnp.minimum(large, NUM_BUCKETS - 1))


def _bias_tables(rel_bias, tq):
    assert tq >= MAX_DISTANCE
    f = rel_bias[_t5_bucket(jnp.arange(2 * tq))]
    w = jnp.concatenate([f[tq:], jnp.zeros((tq, H_A), f.dtype), f[:tq]], axis=0)
    n, period = 2 * tq, 3 * tq
    tb = jnp.tile(w, (n, 1))[:n * (period - 1)].reshape(n, period - 1, H_A)[:, :tq, :]
    tb = jnp.moveaxis(tb, -1, 0)
    b31 = rel_bias[NUM_BUCKETS - 1].reshape(1, H_A)
    tbs = rel_bias[_t5_bucket(PAGE_SIZE - jnp.arange(PAGE_SIZE))].T
    return tb, jnp.pad(b31, ((0, 0), (0, LANE - H_A))), tbs


def kernel(x_prompt, x_sample, cache_k, cache_v, cache_idx_k, state_ret, state_conv, state_ssm, page_table, rel_bias, norm_mix, norm_mlp, norm_final, w_in_even, w_out_even, w_in_odd, conv_w, conv_b, dt_bias, a_log, d_skip, ssm_norm, w_out_odd, w_up, w_down):
    b, s, d = x_prompt.shape
    bs = x_sample.shape[0]
    n_pages = page_table.shape[1]
    past = n_pages * PAGE_SIZE
    tm = min(512, b * s)
    tq = 128
    tk = min(512, s)
    ch = 128
    pg = min(16, n_pages)

    xp = x_prompt.reshape(b * s, d)
    xs = x_sample.reshape(bs, d)
    tb, b31, tbs = _bias_tables(rel_bias, tq)
    b31c = rel_bias[NUM_BUCKETS - 1].reshape(H_A, 1)
    b0c = rel_bias[0].reshape(H_A, 1)
    cos_p, sin_p = _rope_tables(jnp.arange(s))
    cos_s, sin_s = _rope_tables(jnp.full((1,), past))
    bf_slices = ((0, 4 * D_ATT), (E_IK, LANE))
    n_layers, n_pool = cache_k.shape[0], cache_k.shape[1]
    ck_t = jnp.transpose(cache_k, (0, 1, 3, 4, 2)).reshape(n_layers, n_pool, D_ATT, PAGE_SIZE)
    cv_t = jnp.transpose(cache_v, (0, 1, 3, 4, 2)).reshape(n_layers, n_pool, D_ATT, PAGE_SIZE)
    cik_t = jnp.transpose(cache_idx_k, (0, 1, 3, 2))
    hist_t = jnp.transpose(state_conv, (0, 2, 1, 3))

    k_p, v_p, ik_p, k_s, v_s, ik_s, ret_p, ret_s = [], [], [], [], [], [], [], []
    conv_p, conv_s, ssm_p, ssm_s = [], [], [], []
    for layer in range(DEPTH):
        final = layer == DEPTH - 1
        wup, wdn = w_up[layer].astype(BF16), w_down[layer].astype(BF16)
        if layer % 2 == 0:
            e = layer // 2
            w_in = _even_weight(w_in_even[e])
            wo = w_out_even[e].astype(BF16)
            pf, pbq, pbi, k_t, v_t, ik_t, vtb = _even_proj_prompt(xp, norm_mix[layer], w_in, b, s, tk)
            sf, sbq, _ = _rms_matmul(xs, norm_mix[layer], w_in, bs, bf_slices)
            k_p.append(k_t)
            v_p.append(v_t)
            ik_p.append(ik_t)
            k_s.append(sf[:, E_K:E_K + D_ATT].reshape(bs, 1, H_A, DH_A))
            v_s.append(sf[:, E_V:E_V + D_ATT].reshape(bs, 1, H_A, DH_A))
            ik_s.append(sf[:, E_IK:E_IK + D_IDX].reshape(bs, 1, D_IDX))
            att = _attn_prompt(pf, pbq, pbi, vtb, tb, b31, b, s, tq, tk)
            o, st = _ret_prompt(pf, cos_p, sin_p, b, s, ch)
            ret_p.append(st)
            xp = _mlp(_even_merge(xp, att, o, pf, wo, tm), norm_mlp[layer], wup, wdn, norm_final, tm, final)
            iq_bf = sbq[:, E_IQ:E_IQ + D_ATT].reshape(bs, H_IDX, D_IDX)
            iwf = sf[:, E_IK + E_IW:E_IK + E_IW + H_IDX].reshape(bs, H_IDX, 1)
            scores = _sample_scores(iq_bf, iwf, cik_t, e, page_table, pg)
            sel, sel_new = _sample_select(scores.reshape(bs, past), sf[:, E_IQ:E_IQ + D_ATT],
                                          sf[:, E_IK:E_IK + LANE])
            att_s = _sample_attn(sel.reshape(bs, n_pages, PAGE_SIZE), sel_new.reshape(bs, 1, LANE),
                                 sf[:, E_Q:E_Q + D_ATT].reshape(bs, D_ATT, 1),
                                 sf[:, E_K:E_K + D_ATT].reshape(bs, D_ATT, 1),
                                 sf[:, E_V:E_V + D_ATT].reshape(bs, D_ATT, 1),
                                 ck_t, cv_t, e, page_table, tbs, b31c, b0c, pg)
            o_s, st_s = _sample_ret(sf[:, E_RQ:E_RQ + D_RK].reshape(bs, 1, D_RK), sf[:, E_RK:E_RK + D_RK].reshape(bs, 1, D_RK),
                                    sf[:, E_RV:E_RV + D_RV].reshape(bs, 1, D_RV), cos_s, sin_s, state_ret, e)
            ret_s.append(st_s)
            xs = _mlp(_even_merge(xs, att_s.reshape(bs, D_ATT), o_s.reshape(bs, D_RV), sf, wo, bs),
                      norm_mlp[layer], wup, wdn, norm_final, bs, final)
        else:
            o_ = layer // 2
            w_in = _odd_weight(w_in_odd[o_])
            wo = w_out_odd[o_].astype(BF16)
            (pf,) = _rms_matmul(xp, norm_mix[layer], w_in, tm // 2)
            (sf,) = _rms_matmul(xs, norm_mix[layer], w_in, bs)
            y, st = _ssd_prompt(pf, conv_w[o_], conv_b[o_], dt_bias[o_], a_log[o_], d_skip[o_], b, s, ch)
            conv_p.append(pf.reshape(b, s, ODD_PAD)[:, s - (CONV_W - 1):, O_XS:O_XS + CONV_DIM])
            ssm_p.append(st)
            xp = _mlp(_odd_merge(xp, y, pf, ssm_norm[o_], wo, tm), norm_mlp[layer], wup, wdn, norm_final, tm, final)
            y_s, hn_t, st_s = _sample_ssd(sf[:, O_XS:O_XS + CONV_DIM], sf[:, O_DT:O_DT + LANE], hist_t,
                                          conv_w[o_], conv_b[o_], dt_bias[o_], a_log[o_], d_skip[o_], state_ssm, o_)
            conv_s.append(jnp.swapaxes(hn_t, 0, 1))
            ssm_s.append(st_s)
            xs = _mlp(_odd_merge(xs, y_s, sf, ssm_norm[o_], wo, bs),
                      norm_mlp[layer], wup, wdn, norm_final, bs, final)

    n_even = len(k_p)
    k_p = jnp.transpose(jnp.stack(k_p).reshape(n_even, b, H_A, DH_A, s), (0, 1, 4, 2, 3))
    v_p = jnp.transpose(jnp.stack(v_p).reshape(n_even, b, H_A, DH_A, s), (0, 1, 4, 2, 3))
    ik_p = jnp.transpose(jnp.stack(ik_p), (0, 1, 3, 2))
    return (xp.reshape(b, s, d), xs.reshape(bs, 1, d), k_p, v_p, ik_p,
            jnp.stack(k_s), jnp.stack(v_s), jnp.stack(ik_s), jnp.stack(ret_p), jnp.stack(ret_s),
            jnp.stack(conv_p), jnp.stack(conv_s), jnp.stack(ssm_p), jnp.stack(ssm_s))
```

```python
import functools
import math

import jax
import jax.numpy as jnp
import numpy as np
from jax import lax
from jax.experimental import pallas as pl
from jax.experimental.pallas import tpu as pltpu

F32 = jnp.float32
BF16 = jnp.bfloat16
I32 = jnp.int32

D_MODEL = 1024
DEPTH = 4
PAGE_SIZE = 128
DH_A = 64
H_A = (D_MODEL // 2) // DH_A
D_ATT = H_A * DH_A
H_IDX = 8
D_IDX = 64
TOPK_MAX = 256
NUM_BUCKETS = 32
MAX_DISTANCE = 128
H_R = 4
DV_R = (D_MODEL // 2) // H_R
DK_R = DV_R // 2
D_RK = H_R * DK_R
D_RV = H_R * DV_R
ROPE_BASE = 10000.0
D_INNER = 2 * D_MODEL
P_SSM = 64
NH_SSM = D_INNER // P_SSM
G_SSM = 8
HPG = NH_SSM // G_SSM
N_STATE = 128
CONV_W = 4
D_BC = G_SSM * N_STATE
CONV_DIM = D_INNER + 2 * D_BC
D_FF = 4 * D_MODEL
EPS = 1e-6

EVEN_SIZES = (D_ATT, D_ATT, D_ATT, H_IDX * D_IDX, D_IDX, H_IDX, D_RK, D_RK, D_RV, D_RV)
ODD_IN = D_INNER + CONV_DIM + NH_SSM

E_Q, E_K, E_V, E_IQ = 0, 512, 1024, 1536
E_RV, E_RG, E_RQ, E_RK, E_IK = 2048, 2560, 3072, 3328, 3584
E_IW = D_IDX
EVEN_PAD = 3712
O_Z, O_XS, O_BC, O_DT = 0, 2048, 4096, 6144
ODD_PAD = 6272

LANE = 128
VMEM_LIMIT = 56 * 1024 * 1024
NEG_BIG = -1e30
LOG2E = math.log2(math.e)
INT_MIN = -(2 ** 31)
KEY_NEG_INF = int(np.array(0xFF800000, np.uint32).view(np.int32)) ^ 0x7FFFFFFF

NT_DIMS = (((1,), (1,)), ((), ()))
TN_DIMS = (((0,), (0,)), ((), ()))


def _cparams(*sem):
    return pltpu.CompilerParams(dimension_semantics=sem, vmem_limit_bytes=VMEM_LIMIT)


def _const_spec(shape):
    nd = len(shape)
    return pl.BlockSpec(shape, lambda *_: (0,) * nd, pipeline_mode=pl.Buffered(1))


def _rms(x, g):
    return x * lax.rsqrt(jnp.mean(x * x, axis=-1, keepdims=True) + EPS) * g


def _silu(x):
    return x / (1.0 + jnp.exp(-x))


def _softplus(x):
    return jnp.maximum(x, 0.0) + jnp.log(1.0 + jnp.exp(-jnp.abs(x)))


def _rms_matmul_kernel(x_ref, g_ref, w_ref, of_ref, *ob_refs, bf_slices, n_chunk):
    h = _rms(x_ref[...], g_ref[...]).astype(BF16)
    n = w_ref.shape[1]
    for c0 in range(0, n, n_chunk):
        c1 = min(c0 + n_chunk, n)
        of_ref[:, c0:c1] = jnp.dot(h, w_ref[:, c0:c1], preferred_element_type=F32)
    for (start, width), r in zip(bf_slices, ob_refs):
        r[...] = of_ref[:, start:start + width].astype(BF16)


def _rms_matmul(x, g, w, tm, bf_slices=()):
    t, d = x.shape
    n = w.shape[1]
    out_shape = [jax.ShapeDtypeStruct((t, n), F32)]
    out_specs = [pl.BlockSpec((tm, n), lambda i: (i, 0))]
    for _, width in bf_slices:
        out_shape.append(jax.ShapeDtypeStruct((t, width), BF16))
        out_specs.append(pl.BlockSpec((tm, width), lambda i: (i, 0)))
    return pl.pallas_call(
        functools.partial(_rms_matmul_kernel, bf_slices=tuple(bf_slices), n_chunk=512),
        grid=(t // tm,),
        in_specs=[pl.BlockSpec((tm, d), lambda i: (i, 0)), _const_spec((1, d)), _const_spec((d, n))],
        out_specs=out_specs,
        out_shape=out_shape,
        compiler_params=_cparams("parallel"),
        name="rms_matmul",
    )(x, g.reshape(1, d), w)


def _even_proj_kernel(x_ref, g_ref, w_ref, wt_ref, of_ref, obq_ref, obi_ref, kt_ref, vt_ref, ikt_ref, vtb_ref, *,
                      n_chunk):
    h = _rms(x_ref[...], g_ref[...]).astype(BF16)
    n = w_ref.shape[1]
    for c0 in range(0, n, n_chunk):
        c1 = min(c0 + n_chunk, n)
        of_ref[:, c0:c1] = jnp.dot(h, w_ref[:, c0:c1], preferred_element_type=F32)
    obq_ref[...] = of_ref[:, 0:4 * D_ATT].astype(BF16)
    obi_ref[...] = of_ref[:, E_IK:E_IK + LANE].astype(BF16)
    kt_ref[...] = lax.dot_general(wt_ref[0:D_ATT, :], h, NT_DIMS, preferred_element_type=F32)
    vt = lax.dot_general(wt_ref[D_ATT:2 * D_ATT, :], h, NT_DIMS, preferred_element_type=F32)
    vt_ref[...] = vt
    vtb_ref[...] = vt.astype(BF16)
    ikt_ref[...] = lax.dot_general(wt_ref[2 * D_ATT:2 * D_ATT + D_IDX, :], h, NT_DIMS, preferred_element_type=F32)


def _even_proj_prompt(x, g, w, b, s, tm):
    t, d = x.shape
    n = w.shape[1]
    nb = s // tm
    wt = jnp.transpose(jnp.concatenate([w[:, E_K:E_K + D_ATT], w[:, E_V:E_V + D_ATT], w[:, E_IK:E_IK + D_IDX]],
                                       axis=1))
    row = lambda i: (i, 0)
    feat = lambda i: (i // nb, 0, i % nb)
    return pl.pallas_call(
        functools.partial(_even_proj_kernel, n_chunk=512),
        grid=(t // tm,),
        in_specs=[pl.BlockSpec((tm, d), row), _const_spec((1, d)), _const_spec((d, n)), _const_spec(wt.shape)],
        out_specs=[pl.BlockSpec((tm, n), row), pl.BlockSpec((tm, 4 * D_ATT), row), pl.BlockSpec((tm, LANE), row),
                   pl.BlockSpec((None, D_ATT, tm), feat), pl.BlockSpec((None, D_ATT, tm), feat),
                   pl.BlockSpec((None, D_IDX, tm), feat),
                   pl.BlockSpec((None, None, D_ATT, tm), lambda i: (i // nb, i % nb, 0, 0))],
        out_shape=[jax.ShapeDtypeStruct((t, n), F32), jax.ShapeDtypeStruct((t, 4 * D_ATT), BF16),
                   jax.ShapeDtypeStruct((t, LANE), BF16),
                   jax.ShapeDtypeStruct((b, D_ATT, s), F32), jax.ShapeDtypeStruct((b, D_ATT, s), F32),
                   jax.ShapeDtypeStruct((b, D_IDX, s), F32), jax.ShapeDtypeStruct((b, nb, D_ATT, tm), BF16)],
        compiler_params=_cparams("parallel"),
        name="even_proj_prompt",
    )(x, g.reshape(1, d), w, wt)


def _mlp_tail(x, gm_ref, wup_ref, wdn_ref, f_chunk=1024):
    h = _rms(x, gm_ref[...]).astype(BF16)
    acc = x
    for f in range(0, D_FF, f_chunk):
        u = jnp.dot(h, wup_ref[:, f:f + f_chunk], preferred_element_type=F32)
        u = jnp.maximum(u, 0.0)
        u = (u * u).astype(BF16)
        acc = acc + jnp.dot(u, wdn_ref[f:f + f_chunk, :], preferred_element_type=F32)
    return acc


def _even_merge_kernel(x_ref, att_ref, o_ref, rg_ref, wo_ref, out_ref):
    o = o_ref[...]
    parts = []
    for h in range(H_R):
        oh = o[:, h * DV_R:(h + 1) * DV_R]
        d = oh - jnp.mean(oh, axis=-1, keepdims=True)
        parts.append(d * lax.rsqrt(jnp.mean(d * d, axis=-1, keepdims=True) + EPS))
    ret = _silu(rg_ref[...]) * jnp.concatenate(parts, axis=1)
    x = x_ref[...]
    x = x + jnp.dot(att_ref[...].astype(BF16), wo_ref[0:D_ATT, :], preferred_element_type=F32)
    out_ref[...] = x + jnp.dot(ret.astype(BF16), wo_ref[D_ATT:D_ATT + D_RV, :], preferred_element_type=F32)


def _odd_merge_kernel(x_ref, y_ref, z_ref, nw_ref, wo_ref, out_ref):
    g = y_ref[...] * _silu(z_ref[...])
    gw = D_INNER // G_SSM
    parts = []
    for k in range(G_SSM):
        gk = g[:, k * gw:(k + 1) * gw]
        parts.append(gk * lax.rsqrt(jnp.mean(gk * gk, axis=-1, keepdims=True) + EPS))
    gn = (jnp.concatenate(parts, axis=1) * nw_ref[...]).astype(BF16)
    out_ref[...] = x_ref[...] + jnp.dot(gn, wo_ref[...], preferred_element_type=F32)


def _mlp_kernel(x_ref, gm_ref, wup_ref, wdn_ref, gf_ref, out_ref, *, final):
    x = _mlp_tail(x_ref[...], gm_ref, wup_ref, wdn_ref)
    if final:
        x = _rms(x, gf_ref[...])
    out_ref[...] = x


def _mlp(x, gm, wup, wdn, gf, tm, final):
    t, d = x.shape
    row = lambda i: (i, 0)
    return pl.pallas_call(
        functools.partial(_mlp_kernel, final=final),
        grid=(t // tm,),
        in_specs=[pl.BlockSpec((tm, d), row), _const_spec((1, d)), _const_spec(wup.shape), _const_spec(wdn.shape),
                  _const_spec((1, d))],
        out_specs=pl.BlockSpec((tm, d), row),
        out_shape=jax.ShapeDtypeStruct((t, d), F32),
        compiler_params=_cparams("parallel"),
        name="mlp",
    )(x, gm.reshape(1, d), wup, wdn, gf.reshape(1, d))


def _even_merge(x, att, o, pf, wo, tm):
    t, d = x.shape
    row = lambda i: (i, 0)
    return pl.pallas_call(
        _even_merge_kernel,
        grid=(t // tm,),
        in_specs=[pl.BlockSpec((tm, d), row), pl.BlockSpec((tm, D_ATT), row), pl.BlockSpec((tm, D_RV), row),
                  pl.BlockSpec((tm, D_RV), lambda i: (i, E_RG // D_RV)), _const_spec(wo.shape)],
        out_specs=pl.BlockSpec((tm, d), row),
        out_shape=jax.ShapeDtypeStruct((t, d), F32),
        compiler_params=_cparams("parallel"),
        name="even_merge",
    )(x, att, o, pf, wo)


def _odd_merge(x, y, pf, nw, wo, tm):
    t, d = x.shape
    row = lambda i: (i, 0)
    return pl.pallas_call(
        _odd_merge_kernel,
        grid=(t // tm,),
        in_specs=[pl.BlockSpec((tm, d), row), pl.BlockSpec((tm, D_INNER), row),
                  pl.BlockSpec((tm, D_INNER), lambda i: (i, O_Z // D_INNER)),
                  _const_spec((1, D_INNER)), _const_spec(wo.shape)],
        out_specs=pl.BlockSpec((tm, d), row),
        out_shape=jax.ShapeDtypeStruct((t, d), F32),
        compiler_params=_cparams("parallel"),
        name="odd_merge",
    )(x, y, pf, nw.reshape(1, D_INNER), wo)


FOLD = 32


def _fold_rows(x, op):
    n, w = x.shape
    if n > FOLD:
        x = op(x.reshape(n // FOLD, FOLD, w), axis=0)
    if x.shape[0] > 8:
        x = op(x.reshape(x.shape[0] // 8, 8, w), axis=0)
    return op(x, axis=0, keepdims=True)


def _score_key(score):
    bits = lax.bitcast_convert_type(score, I32)
    return bits ^ ((bits >> 31) & 0x7FFFFFFF)


def _kth_largest(count_ge, rows, k_sel):
    def bit_step(b, t):
        cand = t + lax.shift_left(jnp.int32(1), 31 - b)
        return jnp.where(count_ge(cand) >= k_sel, cand, t)
    return lax.fori_loop(0, 32, bit_step, jnp.full((rows, 1), INT_MIN, I32))


def _tie_limit(count_tie_lt, need, rows, n_bits):
    def bit_step(b, x):
        cand = x + lax.shift_left(jnp.int32(1), n_bits - 1 - b)
        return jnp.where(count_tie_lt(cand) < need, cand, x)
    return lax.fori_loop(0, n_bits, bit_step, jnp.zeros((rows, 1), I32))


def _attn_prompt_kernel(q_ref, iq_ref, iw_ref, k_ref, vt_ref, ik_ref, tb_ref, b31_ref, o_ref,
                        sk_ref, x_ref, m_ref, l_ref, acc_ref, s_ref, *, tq, tk, k_sel, idx_bits):
    i = pl.program_id(1)
    r = tk // tq
    nkc = (i + r) // r
    nfar = jnp.maximum(i - 1, 0) // r
    qidx = lax.broadcasted_iota(I32, (tk, tq), 1) + i * tq
    krow = lax.broadcasted_iota(I32, (tk, tq), 0)
    iw = iw_ref[...] * ((D_IDX ** -0.5) * (H_IDX ** -0.5))

    def key_fold(m):
        return jnp.sum(m.reshape(tk // FOLD, FOLD, tq), axis=0)

    def score_chunk(c, carry):
        ikc = ik_ref[c][:, 0:D_IDX]
        acc = jnp.zeros((tk, tq), F32)
        for h in range(H_IDX):
            s = lax.dot_general(ikc, iq_ref[:, h * D_IDX:(h + 1) * D_IDX], NT_DIMS, preferred_element_type=F32)
            acc = acc + iw[h:h + 1, :] * jnp.maximum(s, 0.0)
        sk_ref[c] = _score_key(jnp.where(krow + c * tk <= qidx, acc, -jnp.inf))
        return carry
    lax.fori_loop(0, nkc, score_chunk, 0)

    def count_ge(cand):
        def body(c, part):
            return part + key_fold(jnp.where(sk_ref[c] >= cand, 1.0, 0.0))
        part = lax.fori_loop(0, nkc, body, jnp.zeros((FOLD, tq), F32))
        return _fold_rows(part, jnp.sum)

    def kth_step(b, carry):
        t, n = carry
        cand = t + lax.shift_left(jnp.int32(1), 31 - b)
        cnt = count_ge(cand)
        take = cnt >= float(k_sel)
        return jnp.where(take, cand, t), jnp.where(take, cnt, n)
    all_keys = jnp.full((1, tq), nkc * tk, I32).astype(F32)
    thr, n_ge = lax.fori_loop(0, 32, kth_step, (jnp.full((1, tq), INT_MIN, I32), all_keys))
    excess = jnp.logical_and(n_ge > float(k_sel), thr > KEY_NEG_INF)
    x_ref[...] = jnp.full((1, tq), 2 ** idx_bits, I32)

    @pl.when(jnp.max(jnp.where(excess, 1.0, 0.0)) > 0.0)
    def _():
        need = float(k_sel) - count_ge(thr + 1)

        def count_tie_lt(cand):
            def body(c, part):
                hit = jnp.logical_and(sk_ref[c] == thr, krow + c * tk < cand)
                return part + key_fold(jnp.where(hit, 1.0, 0.0))
            part = lax.fori_loop(0, nkc, body, jnp.zeros((FOLD, tq), F32))
            return _fold_rows(part, jnp.sum)

        def tie_step(b, x):
            cand = x + lax.shift_left(jnp.int32(1), idx_bits - 1 - b)
            return jnp.where(count_tie_lt(cand) < need, cand, x)
        lim = lax.fori_loop(0, idx_bits, tie_step, jnp.zeros((1, tq), I32))
        x_ref[...] = jnp.where(excess, lim, 2 ** idx_bits)

    xlim = x_ref[...]

    m_ref[...] = jnp.full(m_ref.shape, NEG_BIG, F32)
    l_ref[...] = jnp.zeros(l_ref.shape, F32)
    acc_ref[...] = jnp.zeros(acc_ref.shape, F32)

    def attend_chunk(c, near):
        key = sk_ref[c]
        kidx = krow + c * tk
        sel = jnp.logical_or(key > thr, jnp.logical_and(key == thr, kidx <= xlim))
        if near:
            sel = jnp.logical_and(sel, kidx <= qidx)
        amask = jnp.where(sel, 0.0, NEG_BIG)
        kc = k_ref[c]
        vtc = vt_ref[c]
        def logits(h):
            hs = slice(h * DH_A, (h + 1) * DH_A)
            qh = (q_ref[:, hs] * (DH_A ** -0.5 * LOG2E)).astype(BF16)
            return lax.dot_general(kc[:, hs], qh, NT_DIMS, preferred_element_type=F32)

        for h in range(H_A):
            s_ref[h] = logits(h)
        for h in range(H_A):
            b_far = b31_ref[0:1, h:h + 1]
            s = s_ref[h] + amask
            m_old = m_ref[h:h + 1, :]
            if near:
                start = pl.multiple_of(tk + c * tk - (i - 1) * tq, tq)
                s = s + tb_ref[h, pl.ds(start, tk), :]
                m_new = jnp.maximum(m_old, _fold_rows(s, jnp.max))
                shift = -m_new
            else:
                m_new = jnp.maximum(m_old, _fold_rows(s, jnp.max) + b_far)
                shift = b_far - m_new
            alpha = jnp.exp2(m_old - m_new)
            p = jnp.exp2(s + shift)
            l_ref[h:h + 1, :] = alpha * l_ref[h:h + 1, :] + _fold_rows(p, jnp.sum)
            m_ref[h:h + 1, :] = m_new
            hs = slice(h * DH_A, (h + 1) * DH_A)
            pv = jnp.dot(vtc[hs, :], p.astype(BF16), preferred_element_type=F32)
            acc_ref[hs, :] = alpha * acc_ref[hs, :] + pv

    def far_body(c, carry):
        attend_chunk(c, False)
        return carry

    def near_body(c, carry):
        attend_chunk(c, True)
        return carry

    lax.fori_loop(0, nfar, far_body, 0)
    lax.fori_loop(nfar, nkc, near_body, 0)

    for h in range(H_A):
        hs = slice(h * DH_A, (h + 1) * DH_A)
        acc_ref[hs, :] = acc_ref[hs, :] / l_ref[h:h + 1, :]
    o_ref[...] = jnp.transpose(acc_ref[...])


def _attn_prompt(pf, pbq, pbi, vt4, tb_t, b31, b, s, tq, tk):
    nq, nkc = s // tq, s // tk
    k_sel = min(TOPK_MAX, s // 4)
    idx_bits = int(math.log2(s))
    assert 2 ** idx_bits == s and tk % tq == 0 and tq % LANE == 0
    pbq4 = pbq.reshape(b, nkc, tk, pbq.shape[1])
    pbi4 = pbi.reshape(b, nkc, tk, LANE)
    tb_t = jnp.concatenate([jnp.broadcast_to(b31[0, :H_A, None, None], (H_A, tk, tq)), tb_t,
                            jnp.zeros((H_A, tk - tq, tq), F32)], axis=1)
    assert vt4.shape == (b, nkc, D_ATT, tk)
    iw_t = jnp.transpose(pf[:, E_IK + E_IW:E_IK + E_IW + H_IDX])
    return pl.pallas_call(
        functools.partial(_attn_prompt_kernel, tq=tq, tk=tk, k_sel=k_sel, idx_bits=idx_bits),
        grid=(b, nq),
        in_specs=[
            pl.BlockSpec((tq, D_ATT), lambda bb, i: (bb * nq + i, E_Q // D_ATT)),
            pl.BlockSpec((tq, D_ATT), lambda bb, i: (bb * nq + i, E_IQ // D_ATT)),
            pl.BlockSpec((H_IDX, tq), lambda bb, i: (0, bb * nq + i)),
            pl.BlockSpec((None, nkc, tk, D_ATT), lambda bb, i: (bb, 0, 0, E_K // D_ATT), pipeline_mode=pl.Buffered(1)),
            pl.BlockSpec((None, nkc, D_ATT, tk), lambda bb, i: (bb, 0, 0, 0), pipeline_mode=pl.Buffered(1)),
            pl.BlockSpec((None, nkc, tk, LANE), lambda bb, i: (bb, 0, 0, 0), pipeline_mode=pl.Buffered(1)),
            _const_spec(tb_t.shape), _const_spec(b31.shape),
        ],
        out_specs=pl.BlockSpec((tq, D_ATT), lambda bb, i: (bb * nq + i, 0)),
        out_shape=jax.ShapeDtypeStruct((b * s, D_ATT), F32),
        scratch_shapes=[pltpu.VMEM((nkc, tk, tq), I32), pltpu.VMEM((1, tq), I32),
                        pltpu.VMEM((H_A, tq), F32), pltpu.VMEM((H_A, tq), F32),
                        pltpu.VMEM((D_ATT, tq), F32),
                        pltpu.VMEM((H_A, tk, tq), F32)],
        compiler_params=_cparams("parallel", "arbitrary"),
        name="attn_prompt",
    )(pf, pbq, iw_t, pbq4, vt4, pbi4, tb_t * LOG2E, b31 * LOG2E)


def _rotate_half(x, cos_t, sin_t):
    half = DK_R // 2
    width = x.shape[1]
    lane = lax.broadcasted_iota(I32, x.shape, 1)
    partner = jnp.where(lane % DK_R < half, pltpu.roll(x, width - half, 1), pltpu.roll(x, half, 1))
    return x * cos_t + partner * sin_t


def _ret_prompt_kernel(rq_ref, rk_ref, rv_ref, cos_ref, sin_ref, dm_ref, cd_ref, kd_ref, gl_ref,
                       o_ref, st_ref, s_ref):
    c = pl.program_id(1)

    @pl.when(c == 0)
    def _():
        s_ref[...] = jnp.zeros(s_ref.shape, F32)

    cos_t, sin_t = cos_ref[...], sin_ref[...]
    rq = _rotate_half(rq_ref[...], cos_t, sin_t)
    rk = _rotate_half(rk_ref[...], cos_t, sin_t) * (DK_R ** -0.5)
    rv = rv_ref[...]
    for h in range(H_R):
        qh = rq[:, h * DK_R:(h + 1) * DK_R].astype(BF16)
        kh = rk[:, h * DK_R:(h + 1) * DK_R]
        vh = rv[:, h * DV_R:(h + 1) * DV_R].astype(BF16)
        st = s_ref[h]
        scores = lax.dot_general(qh, kh.astype(BF16), NT_DIMS, preferred_element_type=F32) * dm_ref[h]
        inner = jnp.dot(scores.astype(BF16), vh, preferred_element_type=F32)
        cross = jnp.dot(qh, st.astype(BF16), preferred_element_type=F32) * cd_ref[:, h:h + 1]
        o_ref[:, h * DV_R:(h + 1) * DV_R] = inner + cross
        kdec = (kh * kd_ref[:, h:h + 1]).astype(BF16)
        upd = lax.dot_general(kdec, vh, TN_DIMS, preferred_element_type=F32)
        s_ref[h] = st * gl_ref[0:1, h:h + 1] + upd

    @pl.when(c == pl.num_programs(1) - 1)
    def _():
        st_ref[...] = s_ref[...]


def _ret_tables(ch):
    lg = jnp.log(1.0 - jnp.exp2(-5.0 - jnp.arange(H_R, dtype=F32)))
    n = jnp.arange(ch, dtype=F32)
    diff = n[:, None] - n[None, :]
    causal = diff >= 0
    dm = jnp.where(causal[None], jnp.exp(jnp.where(causal, diff, 0.0)[None] * lg[:, None, None]), 0.0)
    cd = jnp.exp((n[:, None] + 1.0) * lg[None, :])
    kd = jnp.exp((ch - 1.0 - n)[:, None] * lg[None, :])
    gl = jnp.exp(ch * lg)[None, :]
    pad = lambda a: jnp.pad(a, ((0, 0), (0, LANE - H_R)))
    return dm, pad(cd), pad(kd), pad(gl)


def _rope_tables(pos):
    half = DK_R // 2
    inv = ROPE_BASE ** (-jnp.arange(half, dtype=F32) / half)
    ang = pos.astype(F32)[:, None] * inv[None, :]
    cos, sin = jnp.cos(ang), jnp.sin(ang)
    cos_t = jnp.tile(jnp.concatenate([cos, cos], axis=1), (1, H_R))
    sin_t = jnp.tile(jnp.concatenate([-sin, sin], axis=1), (1, H_R))
    return cos_t, sin_t


def _ret_prompt(pf, cos_t, sin_t, b, s, ch):
    nc = s // ch
    dm, cd, kd, gl = _ret_tables(ch)
    tok = lambda col: (lambda bb, c: (bb * nc + c, col))
    return pl.pallas_call(
        _ret_prompt_kernel,
        grid=(b, nc),
        in_specs=[pl.BlockSpec((ch, D_RK), tok(E_RQ // D_RK)), pl.BlockSpec((ch, D_RK), tok(E_RK // D_RK)),
                  pl.BlockSpec((ch, D_RV), tok(E_RV // D_RV)),
                  pl.BlockSpec((ch, D_RK), lambda bb, c: (c, 0)), pl.BlockSpec((ch, D_RK), lambda bb, c: (c, 0)),
                  _const_spec(dm.shape), _const_spec(cd.shape), _const_spec(kd.shape), _const_spec(gl.shape)],
        out_specs=[pl.BlockSpec((ch, D_RV), lambda bb, c: (bb * nc + c, 0)),
                   pl.BlockSpec((None, H_R, DK_R, DV_R), lambda bb, c: (bb, 0, 0, 0))],
        out_shape=[jax.ShapeDtypeStruct((b * s, D_RV), F32), jax.ShapeDtypeStruct((b, H_R, DK_R, DV_R), F32)],
        scratch_shapes=[pltpu.VMEM((H_R, DK_R, DV_R), F32)],
        compiler_params=_cparams("parallel", "arbitrary"),
        name="retention_prompt",
    )(pf, pf, pf, cos_t, sin_t, dm, cd, kd, gl)


def _shift_rows(x, prev8, j):
    rolled = pltpu.roll(x, j, 0)
    rid = lax.broadcasted_iota(I32, prev8.shape, 0)
    head = jnp.where(rid < j, pltpu.roll(prev8, j, 0), rolled[0:8])
    return jnp.concatenate([head, rolled[8:]], axis=0)


def _conv_silu(x, prev8, w_ref, b_ref, col0, width):
    out = b_ref[0:1, col0:col0 + width] + x * w_ref[CONV_W - 1:CONV_W, col0:col0 + width]
    for j in range(1, CONV_W):
        out = out + _shift_rows(x, prev8, j) * w_ref[CONV_W - 1 - j:CONV_W - j, col0:col0 + width]
    return _silu(out)


def _split_pack(x, pieces):
    lane = lax.broadcasted_iota(I32, x.shape, 1)
    rest = jnp.where(lane < NH_SSM, x, 0.0)
    packed = jnp.zeros(x.shape, F32)
    for k in range(pieces):
        piece = rest.astype(BF16).astype(F32)
        rest = rest - piece
        packed = packed + (piece if k == 0 else pltpu.roll(piece, k * NH_SSM, 1))
    return packed.astype(BF16)


def _ssd_prompt_kernel(xs_ref, bc_ref, dt_ref, cw_ref, cb_ref, dtb_ref, alog_ref, dskx_ref, tri_ref, ex_ref, sw_ref,
                       y_ref, st_ref, s_ref, pxs_ref, pbc_ref, w_ref, *, ch):
    c = pl.program_id(1)

    @pl.when(c == 0)
    def _():
        s_ref[...] = jnp.zeros(s_ref.shape, F32)
        pxs_ref[...] = jnp.zeros(pxs_ref.shape, F32)
        pbc_ref[...] = jnp.zeros(pbc_ref.shape, F32)

    xs_raw = xs_ref[...]
    bc_raw = bc_ref[...]
    xs = _conv_silu(xs_raw, pxs_ref[...], cw_ref, cb_ref, 0, D_INNER)
    bc = _conv_silu(bc_raw, pbc_ref[...], cw_ref, cb_ref, D_INNER, 2 * D_BC).astype(BF16)
    pxs_ref[...] = xs_raw[ch - 8:ch]
    pbc_ref[...] = bc_raw[ch - 8:ch]

    dt = _softplus(dt_ref[...] + dtb_ref[...])
    a = -jnp.exp(alog_ref[...])
    acs = jnp.dot(tri_ref[...], dt * a, preferred_element_type=F32, precision=lax.Precision.HIGHEST)
    acs_t = jnp.transpose(acs)
    dec_end = jnp.exp(acs[ch - 1:ch, :] - acs)

    def expand(x):
        return jnp.dot(_split_pack(x, 2), ex_ref[...], preferred_element_type=F32)
    dt_x = expand(dt)
    e_x = expand(jnp.exp(acs))
    xdt = (xs * dt_x).astype(BF16)
    xd = (xs * expand(dt * dec_end)).astype(BF16)

    seg_t = jnp.dot(_split_pack(acs, 3), sw_ref[...], preferred_element_type=F32)
    tri = tri_ref[...] > 0.5
    for g in range(G_SSM):
        bm = bc[:, g * N_STATE:(g + 1) * N_STATE]
        cm = bc[:, D_BC + g * N_STATE:D_BC + (g + 1) * N_STATE]
        cb = lax.dot_general(cm, bm, NT_DIMS, preferred_element_type=F32)
        for hh in range(HPG):
            h = g * HPG + hh
            seg = seg_t[:, h * ch:(h + 1) * ch] - acs_t[h:h + 1, :]
            w_ref[h] = (cb * jnp.exp(jnp.where(tri, seg, -jnp.inf))).astype(BF16)

    e_last_x = e_x[ch - 1:ch, :]
    gw = HPG * P_SSM
    for g in range(G_SSM):
        gs = slice(g * gw, (g + 1) * gw)
        bm = bc[:, g * N_STATE:(g + 1) * N_STATE]
        cm = bc[:, D_BC + g * N_STATE:D_BC + (g + 1) * N_STATE]
        st = s_ref[g]
        y_off = jnp.dot(cm, st.astype(BF16), preferred_element_type=F32)
        y_diag = jnp.concatenate(
            [jnp.dot(w_ref[g * HPG + hh], xdt[:, (g * HPG + hh) * P_SSM:(g * HPG + hh + 1) * P_SSM],
                     preferred_element_type=F32) for hh in range(HPG)], axis=1)
        y_ref[:, gs] = y_diag + y_off * e_x[:, gs] + dskx_ref[:, gs] * xs[:, gs]
        upd = lax.dot_general(bm, xd[:, gs], TN_DIMS, preferred_element_type=F32)
        s_ref[g] = st * e_last_x[:, gs] + upd

    @pl.when(c == pl.num_programs(1) - 1)
    def _():
        for g in range(G_SSM):
            st_ref[g * HPG:(g + 1) * HPG] = jnp.transpose(s_ref[g]).reshape(HPG, P_SSM, N_STATE)


def _pad_lanes(v):
    return jnp.pad(v.reshape(1, -1), ((0, 0), (0, LANE - v.shape[-1])))


def _ssd_prompt(pf, cw, cb, dtb, alog, dsk, b, s, ch):
    nc = s // ch
    tri = jnp.tril(jnp.ones((ch, ch), F32))
    lane_head = jnp.arange(LANE) % NH_SSM
    ex = ((lane_head[:, None] == jnp.arange(D_INNER)[None, :] // P_SSM)
          & (jnp.arange(LANE)[:, None] < 2 * NH_SSM)).astype(BF16)
    sw = ((lane_head[:, None] == jnp.arange(NH_SSM * ch)[None, :] // ch)
          & (jnp.arange(LANE)[:, None] < 3 * NH_SSM)).astype(BF16)
    dskx = jnp.repeat(dsk, P_SSM).reshape(1, D_INNER)
    tok = lambda col: (lambda bb, c: (bb * nc + c, col))
    return pl.pallas_call(
        functools.partial(_ssd_prompt_kernel, ch=ch),
        grid=(b, nc),
        in_specs=[pl.BlockSpec((ch, D_INNER), tok(O_XS // D_INNER)), pl.BlockSpec((ch, 2 * D_BC), tok(O_BC // (2 * D_BC))),
                  pl.BlockSpec((ch, LANE), tok(O_DT // LANE)),
                  _const_spec(cw.shape), _const_spec((1, CONV_DIM)), _const_spec((1, LANE)), _const_spec((1, LANE)),
                  _const_spec((1, D_INNER)), _const_spec((ch, ch)), _const_spec(ex.shape), _const_spec(sw.shape)],
        out_specs=[pl.BlockSpec((ch, D_INNER), lambda bb, c: (bb * nc + c, 0)),
                   pl.BlockSpec((None, NH_SSM, P_SSM, N_STATE), lambda bb, c: (bb, 0, 0, 0))],
        out_shape=[jax.ShapeDtypeStruct((b * s, D_INNER), F32),
                   jax.ShapeDtypeStruct((b, NH_SSM, P_SSM, N_STATE), F32)],
        scratch_shapes=[pltpu.VMEM((G_SSM, N_STATE, HPG * P_SSM), F32), pltpu.VMEM((8, D_INNER), F32),
                        pltpu.VMEM((8, 2 * D_BC), F32), pltpu.VMEM((NH_SSM, ch, ch), BF16)],
        compiler_params=_cparams("parallel", "arbitrary"),
        name="ssd_prompt",
    )(pf, pf, pf, cw, cb.reshape(1, CONV_DIM), _pad_lanes(dtb), _pad_lanes(alog), dskx, tri, ex, sw)


def _row_to_col(row, n):
    eye = lax.broadcasted_iota(I32, (n, n), 0) == lax.broadcasted_iota(I32, (n, n), 1)
    return jnp.sum(jnp.where(eye, jnp.broadcast_to(row, (n, n)), 0.0), axis=1, keepdims=True)


def _col_to_row(col, n):
    eye = lax.broadcasted_iota(I32, (n, n), 0) == lax.broadcasted_iota(I32, (n, n), 1)
    return jnp.sum(jnp.where(eye, jnp.broadcast_to(col, (n, n)), 0.0), axis=0, keepdims=True)


def _sample_scores_kernel(pt_ref, iq_ref, iw_ref, *refs, pg):
    page_refs, o_ref = refs[:pg], refs[pg]
    iq = iq_ref[...]
    iw = iw_ref[...] * ((D_IDX ** -0.5) * (H_IDX ** -0.5))
    for u in range(pg):
        s = jnp.dot(iq, page_refs[u][...].astype(BF16), preferred_element_type=F32)
        o_ref[u:u + 1, :] = jnp.sum(iw * jnp.maximum(s, 0.0), axis=0, keepdims=True)


def _sample_scores(iq, iw, cik_t, e, page_table, pg):
    bs, n_pages = page_table.shape
    page_spec = lambda u: pl.BlockSpec((None, None, D_IDX, PAGE_SIZE),
                                       lambda b, j, pt: (e, pt[b, j * pg + u], 0, 0))
    grid_spec = pltpu.PrefetchScalarGridSpec(
        num_scalar_prefetch=1, grid=(bs, n_pages // pg),
        in_specs=[pl.BlockSpec((None, H_IDX, D_IDX), lambda b, j, pt: (b, 0, 0)),
                  pl.BlockSpec((None, H_IDX, 1), lambda b, j, pt: (b, 0, 0))] + [page_spec(u) for u in range(pg)],
        out_specs=pl.BlockSpec((None, pg, PAGE_SIZE), lambda b, j, pt: (b, j, 0)))
    return pl.pallas_call(
        functools.partial(_sample_scores_kernel, pg=pg),
        grid_spec=grid_spec,
        out_shape=jax.ShapeDtypeStruct((bs, n_pages, PAGE_SIZE), F32),
        compiler_params=_cparams("parallel", "arbitrary"),
        name="sample_scores",
    )(page_table, iq, iw, *([cik_t] * pg))


def _sample_select_kernel(sc_ref, iq_ref, ikw_ref, sel_ref, seln_ref, *, k_sel, idx_bits):
    bs, n_keys = sc_ref.shape
    ikw = ikw_ref[...]
    w = ikw[:, E_IW:E_IW + H_IDX] * ((D_IDX ** -0.5) * (H_IDX ** -0.5))
    sc_new = jnp.zeros((bs, 1), F32)
    for h in range(H_IDX):
        s_h = jnp.sum(iq_ref[:, h * D_IDX:(h + 1) * D_IDX] * ikw[:, 0:D_IDX], axis=1, keepdims=True)
        sc_new = sc_new + w[:, h:h + 1] * jnp.maximum(s_h, 0.0)
    key_new = _score_key(sc_new)
    keys = _score_key(sc_ref[...])
    kidx = lax.broadcasted_iota(I32, keys.shape, 1)

    def total(mask, extra):
        m = jnp.where(mask, 1.0, 0.0)
        part = m[:, 0:LANE]
        for t in range(1, n_keys // LANE):
            part = part + m[:, t * LANE:(t + 1) * LANE]
        return jnp.sum(part, axis=1, keepdims=True) + jnp.where(extra, 1.0, 0.0)

    def count_ge(cand):
        return total(keys >= cand, key_new >= cand)

    thr = _kth_largest(count_ge, bs, float(k_sel))
    need = float(k_sel) - count_ge(thr + 1)

    def count_tie_lt(cand):
        return total(jnp.logical_and(keys == thr, kidx < cand), jnp.logical_and(key_new == thr, n_keys < cand))

    lim = _tie_limit(count_tie_lt, need, bs, idx_bits)
    sel = jnp.logical_or(keys > thr, jnp.logical_and(keys == thr, kidx <= lim))
    sel_ref[...] = jnp.where(sel, 1.0, 0.0)
    sel_new = jnp.logical_or(key_new > thr, jnp.logical_and(key_new == thr, n_keys <= lim))
    seln_ref[...] = jnp.broadcast_to(jnp.where(sel_new, 1.0, 0.0), seln_ref.shape)


def _sample_select(scores, iqf, ikw):
    bs, n_keys = scores.shape
    k_sel = min(TOPK_MAX, (n_keys + 1) // 4)
    idx_bits = int(math.log2(n_keys)) + 1
    full = lambda a: pl.BlockSpec(a.shape, lambda: (0,) * a.ndim)
    return pl.pallas_call(
        functools.partial(_sample_select_kernel, k_sel=k_sel, idx_bits=idx_bits),
        in_specs=[full(scores), full(iqf), full(ikw)],
        out_specs=[pl.BlockSpec((bs, n_keys), lambda: (0, 0)), pl.BlockSpec((bs, LANE), lambda: (0, 0))],
        out_shape=[jax.ShapeDtypeStruct((bs, n_keys), F32), jax.ShapeDtypeStruct((bs, LANE), F32)],
        compiler_params=pltpu.CompilerParams(vmem_limit_bytes=VMEM_LIMIT),
        name="sample_select",
    )(scores, iqf, ikw)


def _sample_attn_kernel(pt_ref, sel_ref, seln_ref, q_ref, kn_ref, vn_ref, tbs_ref, b31_ref, b0_ref,
                        *refs, pg, n_pages):
    k_refs, v_refs = refs[:pg], refs[pg:2 * pg]
    o_ref, m_ref, l_ref, acc_ref = refs[2 * pg:]
    j = pl.program_id(1)
    qcol = q_ref[...] * (DH_A ** -0.5)

    def head_sum(x):
        return jnp.concatenate([jnp.sum(x[h * DH_A:(h + 1) * DH_A], axis=0, keepdims=True) for h in range(H_A)],
                               axis=0)

    def head_expand(x, w):
        return jnp.concatenate([jnp.broadcast_to(x[h:h + 1, :], (DH_A, w)) for h in range(H_A)], axis=0)

    @pl.when(j == 0)
    def _():
        sel_new = seln_ref[0:1, 0:1] > 0.5
        lg_new = head_sum(qcol * kn_ref[...]) + b0_ref[...]
        m_ref[...] = jnp.broadcast_to(jnp.where(sel_new, lg_new, NEG_BIG), m_ref.shape)
        l0 = jnp.broadcast_to(jnp.where(sel_new, 1.0, 0.0), (H_A, 1))
        l_ref[...] = jnp.broadcast_to(l0, l_ref.shape)
        first_lane = lax.broadcasted_iota(I32, (D_ATT, PAGE_SIZE), 1) == 0
        acc_ref[...] = jnp.where(first_lane, head_expand(l0, PAGE_SIZE) * vn_ref[...], 0.0)

    qb = jnp.broadcast_to(qcol, (D_ATT, PAGE_SIZE))
    logits = []
    for u in range(pg):
        p_idx = j * pg + u
        s = head_sum(k_refs[u][...] * qb)
        s = s + jnp.where(p_idx == n_pages - 1, tbs_ref[...], b31_ref[...])
        logits.append(jnp.where(sel_ref[pl.ds(p_idx, 1), :] > 0.5, s, NEG_BIG))
    m_old = m_ref[:, 0:1]
    m_new = m_old
    for s in logits:
        m_new = jnp.maximum(m_new, jnp.max(s, axis=1, keepdims=True))
    alpha = jnp.exp(m_old - m_new)
    l_new = alpha * l_ref[:, 0:1]
    acc = head_expand(alpha, PAGE_SIZE) * acc_ref[...]
    for u, s in enumerate(logits):
        p = jnp.exp(s - m_new)
        l_new = l_new + jnp.sum(p, axis=1, keepdims=True)
        acc = acc + v_refs[u][...] * head_expand(p, PAGE_SIZE)
    acc_ref[...] = acc
    m_ref[...] = jnp.broadcast_to(m_new, m_ref.shape)
    l_ref[...] = jnp.broadcast_to(l_new, l_ref.shape)

    @pl.when(j == pl.num_programs(1) - 1)
    def _():
        o_ref[...] = jnp.sum(acc_ref[...], axis=1, keepdims=True) / head_expand(l_ref[:, 0:1], 1)


def _sample_attn(sel, sel_new, q, k_new, v_new, ck_t, cv_t, e, page_table, tbs, b31c, b0c, pg):
    bs, n_pages = page_table.shape
    per_b = lambda shape: pl.BlockSpec((None,) + shape, lambda b, j, pt: (b,) + (0,) * len(shape))
    page_spec = lambda u: pl.BlockSpec((None, None, D_ATT, PAGE_SIZE),
                                       lambda b, j, pt: (e, pt[b, j * pg + u], 0, 0))
    const = lambda shape: pl.BlockSpec(shape, lambda b, j, pt: (0,) * len(shape))
    grid_spec = pltpu.PrefetchScalarGridSpec(
        num_scalar_prefetch=1, grid=(bs, n_pages // pg),
        in_specs=[per_b((n_pages, PAGE_SIZE)), per_b((1, LANE)), per_b((D_ATT, 1)), per_b((D_ATT, 1)),
                  per_b((D_ATT, 1)), const(tbs.shape), const(b31c.shape), const(b0c.shape)]
        + [page_spec(u) for u in range(pg)] + [page_spec(u) for u in range(pg)],
        out_specs=per_b((D_ATT, 1)),
        scratch_shapes=[pltpu.VMEM((H_A, LANE), F32), pltpu.VMEM((H_A, LANE), F32),
                        pltpu.VMEM((D_ATT, PAGE_SIZE), F32)])
    return pl.pallas_call(
        functools.partial(_sample_attn_kernel, pg=pg, n_pages=n_pages),
        grid_spec=grid_spec,
        out_shape=jax.ShapeDtypeStruct((bs, D_ATT, 1), F32),
        compiler_params=_cparams("parallel", "arbitrary"),
        name="sample_attn",
    )(page_table, sel, sel_new, q, k_new, v_new, tbs, b31c, b0c, *([ck_t] * pg), *([cv_t] * pg))


def _sample_ret_kernel(rq_ref, rk_ref, rv_ref, cos_ref, sin_ref, g_ref, s_ref, o_ref, sn_ref):
    cos_t, sin_t = cos_ref[...], sin_ref[...]
    rq = _rotate_half(rq_ref[...], cos_t, sin_t)
    rk = _rotate_half(rk_ref[...], cos_t, sin_t) * (DK_R ** -0.5)
    rv = rv_ref[...]
    for h in range(H_R):
        qc = _row_to_col(rq[:, h * DK_R:(h + 1) * DK_R], DK_R)
        kc = _row_to_col(rk[:, h * DK_R:(h + 1) * DK_R], DK_R)
        vh = rv[:, h * DV_R:(h + 1) * DV_R]
        s_new = s_ref[h] * g_ref[0:1, h:h + 1] + kc * vh
        sn_ref[h] = s_new
        o_ref[:, h * DV_R:(h + 1) * DV_R] = jnp.sum(qc * s_new, axis=0, keepdims=True)


def _layer_b_spec(layer, shape):
    return pl.BlockSpec((None, None) + shape, lambda b: (layer, b) + (0,) * len(shape))


def _sample_ret(rq, rk, rv, cos_t, sin_t, state, e):
    bs = rq.shape[0]
    gamma = _pad_lanes(1.0 - jnp.exp2(-5.0 - jnp.arange(H_R, dtype=F32)))
    per_b = lambda shape: pl.BlockSpec((None,) + shape, lambda b: (b,) + (0,) * len(shape))
    return pl.pallas_call(
        _sample_ret_kernel,
        grid=(bs,),
        in_specs=[per_b((1, D_RK)), per_b((1, D_RK)), per_b((1, D_RV)), _const_spec((1, D_RK)), _const_spec((1, D_RK)),
                  _const_spec((1, LANE)), _layer_b_spec(e, (H_R, DK_R, DV_R))],
        out_specs=[per_b((1, D_RV)), per_b((H_R, DK_R, DV_R))],
        out_shape=[jax.ShapeDtypeStruct((bs, 1, D_RV), F32), jax.ShapeDtypeStruct(state.shape[1:], F32)],
        compiler_params=_cparams("parallel"),
        name="sample_retention",
    )(rq, rk, rv, cos_t, sin_t, gamma, state)


def _sample_conv_kernel(xbc_ref, dt_ref, hist_ref, cw_ref, cb_ref, dtb_ref, alog_ref, act_ref, hn_ref, dtd_ref):
    x_new = xbc_ref[...]
    conv = cb_ref[...] + x_new * cw_ref[CONV_W - 1:CONV_W, :]
    for j in range(CONV_W - 1):
        conv = conv + hist_ref[j] * cw_ref[j:j + 1, :]
    act_ref[...] = _silu(conv)
    for j in range(CONV_W - 2):
        hn_ref[j] = hist_ref[j + 1]
    hn_ref[CONV_W - 2] = x_new
    dt = _softplus(dt_ref[...] + dtb_ref[...])
    dtd_ref[0] = dt
    dtd_ref[1] = jnp.exp(dt * (-jnp.exp(alog_ref[...])))


def _sample_state_kernel(xt_ref, bc_ref, dtd_ref, dsk_ref, s_ref, yt_ref, sn_ref):
    xs_t = xt_ref[...]
    dt = dtd_ref[0:1, 0:NH_SSM]
    d_a = dtd_ref[1:2, :]
    xdt_t = xs_t * dt
    lane = lax.broadcasted_iota(I32, xs_t.shape, 1)
    y_t = dsk_ref[:, 0:NH_SSM] * xs_t
    for g in range(G_SSM):
        bm = bc_ref[:, g * N_STATE:(g + 1) * N_STATE]
        cm = bc_ref[:, D_BC + g * N_STATE:D_BC + (g + 1) * N_STATE]
        for hh in range(HPG):
            h = g * HPG + hh
            s_new = s_ref[h] * d_a[0:1, h:h + 1] + xdt_t[:, h:h + 1] * bm
            sn_ref[h] = s_new
            yc = jnp.sum(s_new * cm, axis=1, keepdims=True)
            y_t = y_t + jnp.where(lane == h, yc, 0.0)
    yt_ref[...] = y_t


def _sample_ssd(xbc, dt, hist_t, cw, cb, dtb, alog, dsk, state, layer):
    bs = xbc.shape[0]
    full = lambda shape: pl.BlockSpec(shape, lambda i: (0,) * len(shape))
    act, hn_t, dtd = pl.pallas_call(
        _sample_conv_kernel,
        grid=(1,),
        in_specs=[full((bs, CONV_DIM)), full((bs, LANE)),
                  pl.BlockSpec((None, CONV_W - 1, bs, CONV_DIM), lambda i: (layer, 0, 0, 0)),
                  full(cw.shape), full((1, CONV_DIM)), full((1, LANE)), full((1, LANE))],
        out_specs=[full((bs, CONV_DIM)), full((CONV_W - 1, bs, CONV_DIM)), full((2, bs, LANE))],
        out_shape=[jax.ShapeDtypeStruct((bs, CONV_DIM), F32), jax.ShapeDtypeStruct((CONV_W - 1, bs, CONV_DIM), F32),
                   jax.ShapeDtypeStruct((2, bs, LANE), F32)],
        compiler_params=pltpu.CompilerParams(vmem_limit_bytes=VMEM_LIMIT),
        name="sample_conv",
    )(xbc, dt, hist_t, cw, cb.reshape(1, CONV_DIM), _pad_lanes(dtb), _pad_lanes(alog))
    xs_t = jnp.swapaxes(act[:, :D_INNER].reshape(bs, NH_SSM, P_SSM), 1, 2)
    bc = act[:, D_INNER:].reshape(bs, 1, 2 * D_BC)
    dtd_b = jnp.swapaxes(dtd, 0, 1)
    per_b = lambda shape: pl.BlockSpec((None,) + shape, lambda b: (b,) + (0,) * len(shape))
    y_t, st = pl.pallas_call(
        _sample_state_kernel,
        grid=(bs,),
        in_specs=[per_b((P_SSM, NH_SSM)), per_b((1, 2 * D_BC)), per_b((2, LANE)), _const_spec((1, LANE)),
                  _layer_b_spec(layer, (NH_SSM, P_SSM, N_STATE))],
        out_specs=[per_b((P_SSM, NH_SSM)), per_b((NH_SSM, P_SSM, N_STATE))],
        out_shape=[jax.ShapeDtypeStruct((bs, P_SSM, NH_SSM), F32), jax.ShapeDtypeStruct(state.shape[1:], F32)],
        compiler_params=_cparams("parallel"),
        name="sample_state",
    )(xs_t, bc, dtd_b, _pad_lanes(dsk), state)
    return jnp.swapaxes(y_t, 1, 2).reshape(bs, D_INNER), hn_t, st


def _even_weight(w):
    q, k, v, iq, ik, iw, rq, rk, rv, rg = jnp.split(w, np.cumsum(EVEN_SIZES)[:-1].tolist(), axis=1)
    pad = jnp.zeros((w.shape[0], EVEN_PAD - sum(EVEN_SIZES)), w.dtype)
    return jnp.concatenate([q, k, v, iq, rv, rg, rq, rk, ik, iw, pad], axis=1).astype(BF16)


def _odd_weight(w):
    return jnp.pad(w, ((0, 0), (0, ODD_PAD - ODD_IN))).astype(BF16)


def _t5_bucket(dist):
    max_exact = NUM_BUCKETS // 2
    d = jnp.maximum(dist, 0)
    ratio = jnp.log(jnp.maximum(d, max_exact).astype(F32) / max_exact) / math.log(MAX_DISTANCE / max_exact)
    large = max_exact + (ratio * (NUM_BUCKETS - max_exact)).astype(I32)
    return jnp.where(d < max_exact, d, jnp.minimum(large, NUM_BUCKETS - 1))


def _bias_tables(rel_bias, tq):
    assert tq >= MAX_DISTANCE
    f = rel_bias[_t5_bucket(jnp.arange(2 * tq))]
    w = jnp.concatenate([f[tq:], jnp.zeros((tq, H_A), f.dtype), f[:tq]], axis=0)
    n, period = 2 * tq, 3 * tq
    tb = jnp.tile(w, (n, 1))[:n * (period - 1)].reshape(n, period - 1, H_A)[:, :tq, :]
    tb = jnp.moveaxis(tb, -1, 0)
    b31 = rel_bias[NUM_BUCKETS - 1].reshape(1, H_A)
    tbs = rel_bias[_t5_bucket(PAGE_SIZE - jnp.arange(PAGE_SIZE))].T
    return tb, jnp.pad(b31, ((0, 0), (0, LANE - H_A))), tbs


def kernel(x_prompt, x_sample, cache_k, cache_v, cache_idx_k, state_ret, state_conv, state_ssm, page_table, rel_bias, norm_mix, norm_mlp, norm_final, w_in_even, w_out_even, w_in_odd, conv_w, conv_b, dt_bias, a_log, d_skip, ssm_norm, w_out_odd, w_up, w_down):
    b, s, d = x_prompt.shape
    bs = x_sample.shape[0]
    n_pages = page_table.shape[1]
    past = n_pages * PAGE_SIZE
    tm = min(512, b * s)
    tq = 128
    tk = min(512, s)
    ch = 128
    ch_ret = min(256, s)
    pg = min(32, n_pages)

    xp = x_prompt.reshape(b * s, d)
    xs = x_sample.reshape(bs, d)
    tb, b31, tbs = _bias_tables(rel_bias, tq)
    b31c = rel_bias[NUM_BUCKETS - 1].reshape(H_A, 1)
    b0c = rel_bias[0].reshape(H_A, 1)
    cos_p, sin_p = _rope_tables(jnp.arange(s))
    cos_s, sin_s = _rope_tables(jnp.full((1,), past))
    bf_slices = ((0, 4 * D_ATT), (E_IK, LANE))
    n_layers, n_pool = cache_k.shape[0], cache_k.shape[1]
    ck_t = jnp.transpose(cache_k, (0, 1, 3, 4, 2)).reshape(n_layers, n_pool, D_ATT, PAGE_SIZE)
    cv_t = jnp.transpose(cache_v, (0, 1, 3, 4, 2)).reshape(n_layers, n_pool, D_ATT, PAGE_SIZE)
    cik_t = jnp.transpose(cache_idx_k, (0, 1, 3, 2))
    hist_t = jnp.transpose(state_conv, (0, 2, 1, 3))

    k_p, v_p, ik_p, k_s, v_s, ik_s, ret_p, ret_s = [], [], [], [], [], [], [], []
    conv_p, conv_s, ssm_p, ssm_s = [], [], [], []
    for layer in range(DEPTH):
        final = layer == DEPTH - 1
        wup, wdn = w_up[layer].astype(BF16), w_down[layer].astype(BF16)
        if layer % 2 == 0:
            e = layer // 2
            w_in = _even_weight(w_in_even[e])
            wo = w_out_even[e].astype(BF16)
            pf, pbq, pbi, k_t, v_t, ik_t, vtb = _even_proj_prompt(xp, norm_mix[layer], w_in, b, s, tk)
            sf, sbq, _ = _rms_matmul(xs, norm_mix[layer], w_in, bs, bf_slices)
            k_p.append(k_t)
            v_p.append(v_t)
            ik_p.append(ik_t)
            k_s.append(sf[:, E_K:E_K + D_ATT].reshape(bs, 1, H_A, DH_A))
            v_s.append(sf[:, E_V:E_V + D_ATT].reshape(bs, 1, H_A, DH_A))
            ik_s.append(sf[:, E_IK:E_IK + D_IDX].reshape(bs, 1, D_IDX))
            att = _attn_prompt(pf, pbq, pbi, vtb, tb, b31, b, s, tq, tk)
            o, st = _ret_prompt(pf, cos_p, sin_p, b, s, ch_ret)
            ret_p.append(st)
            xp = _mlp(_even_merge(xp, att, o, pf, wo, tm), norm_mlp[layer], wup, wdn, norm_final, tm, final)
            iq_bf = sbq[:, E_IQ:E_IQ + D_ATT].reshape(bs, H_IDX, D_IDX)
            iwf = sf[:, E_IK + E_IW:E_IK + E_IW + H_IDX].reshape(bs, H_IDX, 1)
            scores = _sample_scores(iq_bf, iwf, cik_t, e, page_table, pg)
            sel, sel_new = _sample_select(scores.reshape(bs, past), sf[:, E_IQ:E_IQ + D_ATT],
                                          sf[:, E_IK:E_IK + LANE])
            att_s = _sample_attn(sel.reshape(bs, n_pages, PAGE_SIZE), sel_new.reshape(bs, 1, LANE),
                                 sf[:, E_Q:E_Q + D_ATT].reshape(bs, D_ATT, 1),
                                 sf[:, E_K:E_K + D_ATT].reshape(bs, D_ATT, 1),
                                 sf[:, E_V:E_V + D_ATT].reshape(bs, D_ATT, 1),
                                 ck_t, cv_t, e, page_table, tbs, b31c, b0c, pg)
            o_s, st_s = _sample_ret(sf[:, E_RQ:E_RQ + D_RK].reshape(bs, 1, D_RK), sf[:, E_RK:E_RK + D_RK].reshape(bs, 1, D_RK),
                                    sf[:, E_RV:E_RV + D_RV].reshape(bs, 1, D_RV), cos_s, sin_s, state_ret, e)
            ret_s.append(st_s)
            xs = _mlp(_even_merge(xs, att_s.reshape(bs, D_ATT), o_s.reshape(bs, D_RV), sf, wo, bs),
                      norm_mlp[layer], wup, wdn, norm_final, bs, final)
        else:
            o_ = layer // 2
            w_in = _odd_weight(w_in_odd[o_])
            wo = w_out_odd[o_].astype(BF16)
            (pf,) = _rms_matmul(xp, norm_mix[layer], w_in, tm // 2)
            (sf,) = _rms_matmul(xs, norm_mix[layer], w_in, bs)
            y, st = _ssd_prompt(pf, conv_w[o_], conv_b[o_], dt_bias[o_], a_log[o_], d_skip[o_], b, s, ch)
            conv_p.append(pf.reshape(b, s, ODD_PAD)[:, s - (CONV_W - 1):, O_XS:O_XS + CONV_DIM])
            ssm_p.append(st)
            xp = _mlp(_odd_merge(xp, y, pf, ssm_norm[o_], wo, tm), norm_mlp[layer], wup, wdn, norm_final, tm, final)
            y_s, hn_t, st_s = _sample_ssd(sf[:, O_XS:O_XS + CONV_DIM], sf[:, O_DT:O_DT + LANE], hist_t,
                                          conv_w[o_], conv_b[o_], dt_bias[o_], a_log[o_], d_skip[o_], state_ssm, o_)
            conv_s.append(jnp.swapaxes(hn_t, 0, 1))
            ssm_s.append(st_s)
            xs = _mlp(_odd_merge(xs, y_s, sf, ssm_norm[o_], wo, bs),
                      norm_mlp[layer], wup, wdn, norm_final, bs, final)

    n_even = len(k_p)
    k_p = jnp.transpose(jnp.stack(k_p).reshape(n_even, b, H_A, DH_A, s), (0, 1, 4, 2, 3))
    v_p = jnp.transpose(jnp.stack(v_p).reshape(n_even, b, H_A, DH_A, s), (0, 1, 4, 2, 3))
    ik_p = jnp.transpose(jnp.stack(ik_p), (0, 1, 3, 2))
    return (xp.reshape(b, s, d), xs.reshape(bs, 1, d), k_p, v_p, ik_p,
            jnp.stack(k_s), jnp.stack(v_s), jnp.stack(ik_s), jnp.stack(ret_p), jnp.stack(ret_s),
            jnp.stack(conv_p), jnp.stack(conv_s), jnp.stack(ssm_p), jnp.stack(ssm_s))
```

```python
import functools
import math

import jax
import jax.numpy as jnp
import numpy as np
from jax import lax
from jax.experimental import pallas as pl
from jax.experimental.pallas import tpu as pltpu

F32 = jnp.float32
BF16 = jnp.bfloat16
I32 = jnp.int32

D_MODEL = 1024
DEPTH = 4
PAGE_SIZE = 128
DH_A = 64
H_A = (D_MODEL // 2) // DH_A
D_ATT = H_A * DH_A
H_IDX = 8
D_IDX = 64
TOPK_MAX = 256
NUM_BUCKETS = 32
MAX_DISTANCE = 128
H_R = 4
DV_R = (D_MODEL // 2) // H_R
DK_R = DV_R // 2
D_RK = H_R * DK_R
D_RV = H_R * DV_R
ROPE_BASE = 10000.0
D_INNER = 2 * D_MODEL
P_SSM = 64
NH_SSM = D_INNER // P_SSM
G_SSM = 8
HPG = NH_SSM // G_SSM
N_STATE = 128
CONV_W = 4
D_BC = G_SSM * N_STATE
CONV_DIM = D_INNER + 2 * D_BC
D_FF = 4 * D_MODEL
EPS = 1e-6

EVEN_SIZES = (D_ATT, D_ATT, D_ATT, H_IDX * D_IDX, D_IDX, H_IDX, D_RK, D_RK, D_RV, D_RV)
ODD_IN = D_INNER + CONV_DIM + NH_SSM

E_Q, E_K, E_V, E_IQ = 0, 512, 1024, 1536
E_RV, E_RG, E_RQ, E_RK, E_IK = 2048, 2560, 3072, 3328, 3584
E_IW = D_IDX
EVEN_PAD = 3712
O_Z, O_XS, O_BC, O_DT = 0, 2048, 4096, 6144
ODD_PAD = 6272

LANE = 128
VMEM_LIMIT = 56 * 1024 * 1024
NEG_BIG = -1e30
LOG2E = math.log2(math.e)
INT_MIN = -(2 ** 31)
KEY_NEG_INF = int(np.array(0xFF800000, np.uint32).view(np.int32)) ^ 0x7FFFFFFF

NT_DIMS = (((1,), (1,)), ((), ()))
TN_DIMS = (((0,), (0,)), ((), ()))


def _cparams(*sem):
    return pltpu.CompilerParams(dimension_semantics=sem, vmem_limit_bytes=VMEM_LIMIT)


def _const_spec(shape):
    nd = len(shape)
    return pl.BlockSpec(shape, lambda *_: (0,) * nd, pipeline_mode=pl.Buffered(1))


def _rms(x, g):
    return x * lax.rsqrt(jnp.mean(x * x, axis=-1, keepdims=True) + EPS) * g


def _silu(x):
    return x / (1.0 + jnp.exp(-x))


def _softplus(x):
    return jnp.maximum(x, 0.0) + jnp.log(1.0 + jnp.exp(-jnp.abs(x)))


def _rms_matmul_kernel(x_ref, g_ref, w_ref, of_ref, *ob_refs, bf_slices, n_chunk):
    h = _rms(x_ref[...], g_ref[...]).astype(BF16)
    n = w_ref.shape[1]
    for c0 in range(0, n, n_chunk):
        c1 = min(c0 + n_chunk, n)
        of_ref[:, c0:c1] = jnp.dot(h, w_ref[:, c0:c1], preferred_element_type=F32)
    for (start, width), r in zip(bf_slices, ob_refs):
        r[...] = of_ref[:, start:start + width].astype(BF16)


def _rms_matmul(x, g, w, tm, bf_slices=()):
    t, d = x.shape
    n = w.shape[1]
    out_shape = [jax.ShapeDtypeStruct((t, n), F32)]
    out_specs = [pl.BlockSpec((tm, n), lambda i: (i, 0))]
    for _, width in bf_slices:
        out_shape.append(jax.ShapeDtypeStruct((t, width), BF16))
        out_specs.append(pl.BlockSpec((tm, width), lambda i: (i, 0)))
    return pl.pallas_call(
        functools.partial(_rms_matmul_kernel, bf_slices=tuple(bf_slices), n_chunk=512),
        grid=(t // tm,),
        in_specs=[pl.BlockSpec((tm, d), lambda i: (i, 0)), _const_spec((1, d)), _const_spec((d, n))],
        out_specs=out_specs,
        out_shape=out_shape,
        compiler_params=_cparams("parallel"),
        name="rms_matmul",
    )(x, g.reshape(1, d), w)


def _even_proj_kernel(x_ref, g_ref, w_ref, wt_ref, of_ref, obq_ref, obi_ref, kt_ref, vt_ref, ikt_ref, vtb_ref, *,
                      n_chunk):
    h = _rms(x_ref[...], g_ref[...]).astype(BF16)
    n = w_ref.shape[1]
    for c0 in range(0, n, n_chunk):
        c1 = min(c0 + n_chunk, n)
        of_ref[:, c0:c1] = jnp.dot(h, w_ref[:, c0:c1], preferred_element_type=F32)
    obq_ref[...] = of_ref[:, 0:4 * D_ATT].astype(BF16)
    obi_ref[...] = of_ref[:, E_IK:E_IK + LANE].astype(BF16)
    kt_ref[...] = lax.dot_general(wt_ref[0:D_ATT, :], h, NT_DIMS, preferred_element_type=F32)
    vt = lax.dot_general(wt_ref[D_ATT:2 * D_ATT, :], h, NT_DIMS, preferred_element_type=F32)
    vt_ref[...] = vt
    vtb_ref[...] = vt.astype(BF16)
    ikt_ref[...] = lax.dot_general(wt_ref[2 * D_ATT:2 * D_ATT + D_IDX, :], h, NT_DIMS, preferred_element_type=F32)


def _even_proj_prompt(x, g, w, b, s, tm):
    t, d = x.shape
    n = w.shape[1]
    nb = s // tm
    wt = jnp.transpose(jnp.concatenate([w[:, E_K:E_K + D_ATT], w[:, E_V:E_V + D_ATT], w[:, E_IK:E_IK + D_IDX]],
                                       axis=1))
    row = lambda i: (i, 0)
    feat = lambda i: (i // nb, 0, i % nb)
    return pl.pallas_call(
        functools.partial(_even_proj_kernel, n_chunk=512),
        grid=(t // tm,),
        in_specs=[pl.BlockSpec((tm, d), row), _const_spec((1, d)), _const_spec((d, n)), _const_spec(wt.shape)],
        out_specs=[pl.BlockSpec((tm, n), row), pl.BlockSpec((tm, 4 * D_ATT), row), pl.BlockSpec((tm, LANE), row),
                   pl.BlockSpec((None, D_ATT, tm), feat), pl.BlockSpec((None, D_ATT, tm), feat),
                   pl.BlockSpec((None, D_IDX, tm), feat),
                   pl.BlockSpec((None, None, D_ATT, tm), lambda i: (i // nb, i % nb, 0, 0))],
        out_shape=[jax.ShapeDtypeStruct((t, n), F32), jax.ShapeDtypeStruct((t, 4 * D_ATT), BF16),
                   jax.ShapeDtypeStruct((t, LANE), BF16),
                   jax.ShapeDtypeStruct((b, D_ATT, s), F32), jax.ShapeDtypeStruct((b, D_ATT, s), F32),
                   jax.ShapeDtypeStruct((b, D_IDX, s), F32), jax.ShapeDtypeStruct((b, nb, D_ATT, tm), BF16)],
        compiler_params=_cparams("parallel"),
        name="even_proj_prompt",
    )(x, g.reshape(1, d), w, wt)


def _mlp_tail(x, gm_ref, wup_ref, wdn_ref, f_chunk=1024):
    h = _rms(x, gm_ref[...]).astype(BF16)
    acc = x
    for f in range(0, D_FF, f_chunk):
        u = jnp.dot(h, wup_ref[:, f:f + f_chunk], preferred_element_type=F32)
        u = jnp.maximum(u, 0.0)
        u = (u * u).astype(BF16)
        acc = acc + jnp.dot(u, wdn_ref[f:f + f_chunk, :], preferred_element_type=F32)
    return acc


def _even_merge_kernel(x_ref, att_ref, o_ref, rg_ref, wo_ref, out_ref):
    o = o_ref[...]
    parts = []
    for h in range(H_R):
        oh = o[:, h * DV_R:(h + 1) * DV_R]
        d = oh - jnp.mean(oh, axis=-1, keepdims=True)
        parts.append(d * lax.rsqrt(jnp.mean(d * d, axis=-1, keepdims=True) + EPS))
    ret = _silu(rg_ref[...]) * jnp.concatenate(parts, axis=1)
    x = x_ref[...]
    x = x + jnp.dot(att_ref[...].astype(BF16), wo_ref[0:D_ATT, :], preferred_element_type=F32)
    out_ref[...] = x + jnp.dot(ret.astype(BF16), wo_ref[D_ATT:D_ATT + D_RV, :], preferred_element_type=F32)


def _odd_merge_kernel(x_ref, y_ref, z_ref, nw_ref, wo_ref, out_ref):
    g = y_ref[...] * _silu(z_ref[...])
    gw = D_INNER // G_SSM
    parts = []
    for k in range(G_SSM):
        gk = g[:, k * gw:(k + 1) * gw]
        parts.append(gk * lax.rsqrt(jnp.mean(gk * gk, axis=-1, keepdims=True) + EPS))
    gn = (jnp.concatenate(parts, axis=1) * nw_ref[...]).astype(BF16)
    out_ref[...] = x_ref[...] + jnp.dot(gn, wo_ref[...], preferred_element_type=F32)


def _mlp_kernel(x_ref, gm_ref, wup_ref, wdn_ref, gf_ref, out_ref, *, final):
    x = _mlp_tail(x_ref[...], gm_ref, wup_ref, wdn_ref)
    if final:
        x = _rms(x, gf_ref[...])
    out_ref[...] = x


def _mlp(x, gm, wup, wdn, gf, tm, final):
    t, d = x.shape
    row = lambda i: (i, 0)
    return pl.pallas_call(
        functools.partial(_mlp_kernel, final=final),
        grid=(t // tm,),
        in_specs=[pl.BlockSpec((tm, d), row), _const_spec((1, d)), _const_spec(wup.shape), _const_spec(wdn.shape),
                  _const_spec((1, d))],
        out_specs=pl.BlockSpec((tm, d), row),
        out_shape=jax.ShapeDtypeStruct((t, d), F32),
        compiler_params=_cparams("parallel"),
        name="mlp",
    )(x, gm.reshape(1, d), wup, wdn, gf.reshape(1, d))


def _even_merge(x, att, o, pf, wo, tm):
    t, d = x.shape
    row = lambda i: (i, 0)
    return pl.pallas_call(
        _even_merge_kernel,
        grid=(t // tm,),
        in_specs=[pl.BlockSpec((tm, d), row), pl.BlockSpec((tm, D_ATT), row), pl.BlockSpec((tm, D_RV), row),
                  pl.BlockSpec((tm, D_RV), lambda i: (i, E_RG // D_RV)), _const_spec(wo.shape)],
        out_specs=pl.BlockSpec((tm, d), row),
        out_shape=jax.ShapeDtypeStruct((t, d), F32),
        compiler_params=_cparams("parallel"),
        name="even_merge",
    )(x, att, o, pf, wo)


def _odd_merge(x, y, pf, nw, wo, tm):
    t, d = x.shape
    row = lambda i: (i, 0)
    return pl.pallas_call(
        _odd_merge_kernel,
        grid=(t // tm,),
        in_specs=[pl.BlockSpec((tm, d), row), pl.BlockSpec((tm, D_INNER), row),
                  pl.BlockSpec((tm, D_INNER), lambda i: (i, O_Z // D_INNER)),
                  _const_spec((1, D_INNER)), _const_spec(wo.shape)],
        out_specs=pl.BlockSpec((tm, d), row),
        out_shape=jax.ShapeDtypeStruct((t, d), F32),
        compiler_params=_cparams("parallel"),
        name="odd_merge",
    )(x, y, pf, nw.reshape(1, D_INNER), wo)


FOLD = 32


def _fold_rows(x, op):
    n, w = x.shape
    if n > FOLD:
        x = op(x.reshape(n // FOLD, FOLD, w), axis=0)
    if x.shape[0] > 8:
        x = op(x.reshape(x.shape[0] // 8, 8, w), axis=0)
    return op(x, axis=0, keepdims=True)


def _score_key(score):
    bits = lax.bitcast_convert_type(score, I32)
    return bits ^ ((bits >> 31) & 0x7FFFFFFF)


def _kth_largest(count_ge, rows, k_sel):
    def bit_step(b, t):
        cand = t + lax.shift_left(jnp.int32(1), 31 - b)
        return jnp.where(count_ge(cand) >= k_sel, cand, t)
    return lax.fori_loop(0, 32, bit_step, jnp.full((rows, 1), INT_MIN, I32))


def _tie_limit(count_tie_lt, need, rows, n_bits):
    def bit_step(b, x):
        cand = x + lax.shift_left(jnp.int32(1), n_bits - 1 - b)
        return jnp.where(count_tie_lt(cand) < need, cand, x)
    return lax.fori_loop(0, n_bits, bit_step, jnp.zeros((rows, 1), I32))


def _attn_prompt_kernel(q_ref, iq_ref, iw_ref, k_ref, vt_ref, ik_ref, tb_ref, b31_ref, o_ref,
                        sk_ref, x_ref, m_ref, l_ref, acc_ref, s_ref, *, tq, tk, k_sel, idx_bits):
    i = pl.program_id(1)
    r = tk // tq
    nkc = (i + r) // r
    nfar = jnp.maximum(i - 1, 0) // r
    qidx = lax.broadcasted_iota(I32, (tk, tq), 1) + i * tq
    krow = lax.broadcasted_iota(I32, (tk, tq), 0)
    iw = iw_ref[...] * ((D_IDX ** -0.5) * (H_IDX ** -0.5))

    def key_fold(m):
        return jnp.sum(m.reshape(tk // FOLD, FOLD, tq), axis=0)

    def score_chunk(c, carry):
        ikc = ik_ref[c][:, 0:D_IDX]
        acc = jnp.zeros((tk, tq), F32)
        for h in range(H_IDX):
            s = lax.dot_general(ikc, iq_ref[:, h * D_IDX:(h + 1) * D_IDX], NT_DIMS, preferred_element_type=F32)
            acc = acc + iw[h:h + 1, :] * jnp.maximum(s, 0.0)
        sk_ref[c] = _score_key(jnp.where(krow + c * tk <= qidx, acc, -jnp.inf))
        return carry
    lax.fori_loop(0, nkc, score_chunk, 0)

    def count_ge(cand):
        def body(c, part):
            return part + key_fold(jnp.where(sk_ref[c] >= cand, 1.0, 0.0))
        part = lax.fori_loop(0, nkc, body, jnp.zeros((FOLD, tq), F32))
        return _fold_rows(part, jnp.sum)

    def kth_step(b, carry):
        t, n = carry
        cand = t + lax.shift_left(jnp.int32(1), 31 - b)
        cnt = count_ge(cand)
        take = cnt >= float(k_sel)
        return jnp.where(take, cand, t), jnp.where(take, cnt, n)
    all_keys = jnp.full((1, tq), nkc * tk, I32).astype(F32)
    thr, n_ge = lax.fori_loop(0, 32, kth_step, (jnp.full((1, tq), INT_MIN, I32), all_keys))
    excess = jnp.logical_and(n_ge > float(k_sel), thr > KEY_NEG_INF)
    x_ref[...] = jnp.full((1, tq), 2 ** idx_bits, I32)

    @pl.when(jnp.max(jnp.where(excess, 1.0, 0.0)) > 0.0)
    def _():
        need = float(k_sel) - count_ge(thr + 1)

        def count_tie_lt(cand):
            def body(c, part):
                hit = jnp.logical_and(sk_ref[c] == thr, krow + c * tk < cand)
                return part + key_fold(jnp.where(hit, 1.0, 0.0))
            part = lax.fori_loop(0, nkc, body, jnp.zeros((FOLD, tq), F32))
            return _fold_rows(part, jnp.sum)

        def tie_step(b, x):
            cand = x + lax.shift_left(jnp.int32(1), idx_bits - 1 - b)
            return jnp.where(count_tie_lt(cand) < need, cand, x)
        lim = lax.fori_loop(0, idx_bits, tie_step, jnp.zeros((1, tq), I32))
        x_ref[...] = jnp.where(excess, lim, 2 ** idx_bits)

    xlim = x_ref[...]

    m_ref[...] = jnp.full(m_ref.shape, NEG_BIG, F32)
    l_ref[...] = jnp.zeros(l_ref.shape, F32)
    acc_ref[...] = jnp.zeros(acc_ref.shape, F32)

    def attend_chunk(c, near):
        key = sk_ref[c]
        kidx = krow + c * tk
        sel = jnp.logical_or(key > thr, jnp.logical_and(key == thr, kidx <= xlim))
        if near:
            sel = jnp.logical_and(sel, kidx <= qidx)
        amask = jnp.where(sel, 0.0, NEG_BIG)
        kc = k_ref[c]
        vtc = vt_ref[c]
        def logits(h):
            hs = slice(h * DH_A, (h + 1) * DH_A)
            qh = (q_ref[:, hs] * (DH_A ** -0.5 * LOG2E)).astype(BF16)
            return lax.dot_general(kc[:, hs], qh, NT_DIMS, preferred_element_type=F32)

        for h in range(H_A):
            s_ref[h] = logits(h)
        for h in range(H_A):
            b_far = b31_ref[0:1, h:h + 1]
            s = s_ref[h] + amask
            m_old = m_ref[h:h + 1, :]
            if near:
                start = pl.multiple_of(tk + c * tk - (i - 1) * tq, tq)
                s = s + tb_ref[h, pl.ds(start, tk), :]
                m_new = jnp.maximum(m_old, _fold_rows(s, jnp.max))
                shift = -m_new
            else:
                m_new = jnp.maximum(m_old, _fold_rows(s, jnp.max) + b_far)
                shift = b_far - m_new
            alpha = jnp.exp2(m_old - m_new)
            p = jnp.exp2(s + shift)
            l_ref[h:h + 1, :] = alpha * l_ref[h:h + 1, :] + _fold_rows(p, jnp.sum)
            m_ref[h:h + 1, :] = m_new
            hs = slice(h * DH_A, (h + 1) * DH_A)
            pv = jnp.dot(vtc[hs, :], p.astype(BF16), preferred_element_type=F32)
            acc_ref[hs, :] = alpha * acc_ref[hs, :] + pv

    def far_body(c, carry):
        attend_chunk(c, False)
        return carry

    def near_body(c, carry):
        attend_chunk(c, True)
        return carry

    lax.fori_loop(0, nfar, far_body, 0)
    lax.fori_loop(nfar, nkc, near_body, 0)

    for h in range(H_A):
        hs = slice(h * DH_A, (h + 1) * DH_A)
        acc_ref[hs, :] = acc_ref[hs, :] / l_ref[h:h + 1, :]
    o_ref[...] = jnp.transpose(acc_ref[...])


def _attn_prompt(pf, pbq, pbi, vt4, tb_t, b31, b, s, tq, tk):
    nq, nkc = s // tq, s // tk
    k_sel = min(TOPK_MAX, s // 4)
    idx_bits = int(math.log2(s))
    assert 2 ** idx_bits == s and tk % tq == 0 and tq % LANE == 0
    pbq4 = pbq.reshape(b, nkc, tk, pbq.shape[1])
    pbi4 = pbi.reshape(b, nkc, tk, LANE)
    tb_t = jnp.concatenate([jnp.broadcast_to(b31[0, :H_A, None, None], (H_A, tk, tq)), tb_t,
                            jnp.zeros((H_A, tk - tq, tq), F32)], axis=1)
    assert vt4.shape == (b, nkc, D_ATT, tk)
    iw_t = jnp.transpose(pf[:, E_IK + E_IW:E_IK + E_IW + H_IDX])
    return pl.pallas_call(
        functools.partial(_attn_prompt_kernel, tq=tq, tk=tk, k_sel=k_sel, idx_bits=idx_bits),
        grid=(b, nq),
        in_specs=[
            pl.BlockSpec((tq, D_ATT), lambda bb, i: (bb * nq + i, E_Q // D_ATT)),
            pl.BlockSpec((tq, D_ATT), lambda bb, i: (bb * nq + i, E_IQ // D_ATT)),
            pl.BlockSpec((H_IDX, tq), lambda bb, i: (0, bb * nq + i)),
            pl.BlockSpec((None, nkc, tk, D_ATT), lambda bb, i: (bb, 0, 0, E_K // D_ATT), pipeline_mode=pl.Buffered(1)),
            pl.BlockSpec((None, nkc, D_ATT, tk), lambda bb, i: (bb, 0, 0, 0), pipeline_mode=pl.Buffered(1)),
            pl.BlockSpec((None, nkc, tk, LANE), lambda bb, i: (bb, 0, 0, 0), pipeline_mode=pl.Buffered(1)),
            _const_spec(tb_t.shape), _const_spec(b31.shape),
        ],
        out_specs=pl.BlockSpec((tq, D_ATT), lambda bb, i: (bb * nq + i, 0)),
        out_shape=jax.ShapeDtypeStruct((b * s, D_ATT), F32),
        scratch_shapes=[pltpu.VMEM((nkc, tk, tq), I32), pltpu.VMEM((1, tq), I32),
                        pltpu.VMEM((H_A, tq), F32), pltpu.VMEM((H_A, tq), F32),
                        pltpu.VMEM((D_ATT, tq), F32),
                        pltpu.VMEM((H_A, tk, tq), F32)],
        compiler_params=_cparams("parallel", "arbitrary"),
        name="attn_prompt",
    )(pf, pbq, iw_t, pbq4, vt4, pbi4, tb_t * LOG2E, b31 * LOG2E)


def _rotate_half(x, cos_t, sin_t):
    half = DK_R // 2
    width = x.shape[1]
    lane = lax.broadcasted_iota(I32, x.shape, 1)
    partner = jnp.where(lane % DK_R < half, pltpu.roll(x, width - half, 1), pltpu.roll(x, half, 1))
    return x * cos_t + partner * sin_t


def _ret_prompt_kernel(rq_ref, rk_ref, rv_ref, cos_ref, sin_ref, dm_ref, cd_ref, kd_ref, gl_ref,
                       o_ref, st_ref, s_ref):
    c = pl.program_id(1)

    @pl.when(c == 0)
    def _():
        s_ref[...] = jnp.zeros(s_ref.shape, F32)

    cos_t, sin_t = cos_ref[...], sin_ref[...]
    rq = _rotate_half(rq_ref[...], cos_t, sin_t)
    rk = _rotate_half(rk_ref[...], cos_t, sin_t) * (DK_R ** -0.5)
    rv = rv_ref[...]
    for h in range(H_R):
        qh = rq[:, h * DK_R:(h + 1) * DK_R].astype(BF16)
        kh = rk[:, h * DK_R:(h + 1) * DK_R]
        vh = rv[:, h * DV_R:(h + 1) * DV_R].astype(BF16)
        st = s_ref[h]
        scores = lax.dot_general(qh, kh.astype(BF16), NT_DIMS, preferred_element_type=F32) * dm_ref[h]
        inner = jnp.dot(scores.astype(BF16), vh, preferred_element_type=F32)
        cross = jnp.dot(qh, st.astype(BF16), preferred_element_type=F32) * cd_ref[:, h:h + 1]
        o_ref[:, h * DV_R:(h + 1) * DV_R] = inner + cross
        kdec = (kh * kd_ref[:, h:h + 1]).astype(BF16)
        upd = lax.dot_general(kdec, vh, TN_DIMS, preferred_element_type=F32)
        s_ref[h] = st * gl_ref[0:1, h:h + 1] + upd

    @pl.when(c == pl.num_programs(1) - 1)
    def _():
        st_ref[...] = s_ref[...]


def _ret_tables(ch):
    lg = jnp.log(1.0 - jnp.exp2(-5.0 - jnp.arange(H_R, dtype=F32)))
    n = jnp.arange(ch, dtype=F32)
    diff = n[:, None] - n[None, :]
    causal = diff >= 0
    dm = jnp.where(causal[None], jnp.exp(jnp.where(causal, diff, 0.0)[None] * lg[:, None, None]), 0.0)
    cd = jnp.exp((n[:, None] + 1.0) * lg[None, :])
    kd = jnp.exp((ch - 1.0 - n)[:, None] * lg[None, :])
    gl = jnp.exp(ch * lg)[None, :]
    pad = lambda a: jnp.pad(a, ((0, 0), (0, LANE - H_R)))
    return dm, pad(cd), pad(kd), pad(gl)


def _rope_tables(pos):
    half = DK_R // 2
    inv = ROPE_BASE ** (-jnp.arange(half, dtype=F32) / half)
    ang = pos.astype(F32)[:, None] * inv[None, :]
    cos, sin = jnp.cos(ang), jnp.sin(ang)
    cos_t = jnp.tile(jnp.concatenate([cos, cos], axis=1), (1, H_R))
    sin_t = jnp.tile(jnp.concatenate([-sin, sin], axis=1), (1, H_R))
    return cos_t, sin_t


def _ret_prompt(pf, cos_t, sin_t, b, s, ch):
    nc = s // ch
    dm, cd, kd, gl = _ret_tables(ch)
    tok = lambda col: (lambda bb, c: (bb * nc + c, col))
    return pl.pallas_call(
        _ret_prompt_kernel,
        grid=(b, nc),
        in_specs=[pl.BlockSpec((ch, D_RK), tok(E_RQ // D_RK)), pl.BlockSpec((ch, D_RK), tok(E_RK // D_RK)),
                  pl.BlockSpec((ch, D_RV), tok(E_RV // D_RV)),
                  pl.BlockSpec((ch, D_RK), lambda bb, c: (c, 0)), pl.BlockSpec((ch, D_RK), lambda bb, c: (c, 0)),
                  _const_spec(dm.shape), _const_spec(cd.shape), _const_spec(kd.shape), _const_spec(gl.shape)],
        out_specs=[pl.BlockSpec((ch, D_RV), lambda bb, c: (bb * nc + c, 0)),
                   pl.BlockSpec((None, H_R, DK_R, DV_R), lambda bb, c: (bb, 0, 0, 0))],
        out_shape=[jax.ShapeDtypeStruct((b * s, D_RV), F32), jax.ShapeDtypeStruct((b, H_R, DK_R, DV_R), F32)],
        scratch_shapes=[pltpu.VMEM((H_R, DK_R, DV_R), F32)],
        compiler_params=_cparams("parallel", "arbitrary"),
        name="retention_prompt",
    )(pf, pf, pf, cos_t, sin_t, dm, cd, kd, gl)


def _shift_rows(x, prev8, j):
    rolled = pltpu.roll(x, j, 0)
    rid = lax.broadcasted_iota(I32, prev8.shape, 0)
    head = jnp.where(rid < j, pltpu.roll(prev8, j, 0), rolled[0:8])
    return jnp.concatenate([head, rolled[8:]], axis=0)


def _conv_silu(x, prev8, w_ref, b_ref, col0, width):
    out = b_ref[0:1, col0:col0 + width] + x * w_ref[CONV_W - 1:CONV_W, col0:col0 + width]
    for j in range(1, CONV_W):
        out = out + _shift_rows(x, prev8, j) * w_ref[CONV_W - 1 - j:CONV_W - j, col0:col0 + width]
    return _silu(out)


def _split_pack(x, pieces):
    lane = lax.broadcasted_iota(I32, x.shape, 1)
    rest = jnp.where(lane < NH_SSM, x, 0.0)
    packed = jnp.zeros(x.shape, F32)
    for k in range(pieces):
        piece = rest.astype(BF16).astype(F32)
        rest = rest - piece
        packed = packed + (piece if k == 0 else pltpu.roll(piece, k * NH_SSM, 1))
    return packed.astype(BF16)


def _ssd_prompt_kernel(xs_ref, bc_ref, dt_ref, cw_ref, cb_ref, dtb_ref, alog_ref, dskx_ref, tri_ref, ex_ref, sw_ref,
                       y_ref, st_ref, s_ref, pxs_ref, pbc_ref, w_ref, *, ch):
    c = pl.program_id(1)

    @pl.when(c == 0)
    def _():
        s_ref[...] = jnp.zeros(s_ref.shape, F32)
        pxs_ref[...] = jnp.zeros(pxs_ref.shape, F32)
        pbc_ref[...] = jnp.zeros(pbc_ref.shape, F32)

    xs_raw = xs_ref[...]
    bc_raw = bc_ref[...]
    xs = _conv_silu(xs_raw, pxs_ref[...], cw_ref, cb_ref, 0, D_INNER)
    bc = _conv_silu(bc_raw, pbc_ref[...], cw_ref, cb_ref, D_INNER, 2 * D_BC).astype(BF16)
    pxs_ref[...] = xs_raw[ch - 8:ch]
    pbc_ref[...] = bc_raw[ch - 8:ch]

    dt = _softplus(dt_ref[...] + dtb_ref[...])
    a = -jnp.exp(alog_ref[...])
    acs = jnp.dot(tri_ref[...], dt * a, preferred_element_type=F32, precision=lax.Precision.HIGHEST)
    acs_t = jnp.transpose(acs)
    dec_end = jnp.exp(acs[ch - 1:ch, :] - acs)

    def expand(x):
        return jnp.dot(_split_pack(x, 2), ex_ref[...], preferred_element_type=F32)
    dt_x = expand(dt)
    e_x = expand(jnp.exp(acs))
    xdt = (xs * dt_x).astype(BF16)
    xd = (xs * expand(dt * dec_end)).astype(BF16)

    seg_t = jnp.dot(_split_pack(acs, 3), sw_ref[...], preferred_element_type=F32)
    tri = tri_ref[...] > 0.5
    for g in range(G_SSM):
        bm = bc[:, g * N_STATE:(g + 1) * N_STATE]
        cm = bc[:, D_BC + g * N_STATE:D_BC + (g + 1) * N_STATE]
        cb = lax.dot_general(cm, bm, NT_DIMS, preferred_element_type=F32)
        for hh in range(HPG):
            h = g * HPG + hh
            seg = seg_t[:, h * ch:(h + 1) * ch] - acs_t[h:h + 1, :]
            w_ref[h] = (cb * jnp.exp(jnp.where(tri, seg, -jnp.inf))).astype(BF16)

    e_last_x = e_x[ch - 1:ch, :]
    gw = HPG * P_SSM
    for g in range(G_SSM):
        gs = slice(g * gw, (g + 1) * gw)
        bm = bc[:, g * N_STATE:(g + 1) * N_STATE]
        cm = bc[:, D_BC + g * N_STATE:D_BC + (g + 1) * N_STATE]
        st = s_ref[g]
        y_off = jnp.dot(cm, st.astype(BF16), preferred_element_type=F32)
        y_diag = jnp.concatenate(
            [jnp.dot(w_ref[g * HPG + hh], xdt[:, (g * HPG + hh) * P_SSM:(g * HPG + hh + 1) * P_SSM],
                     preferred_element_type=F32) for hh in range(HPG)], axis=1)
        y_ref[:, gs] = y_diag + y_off * e_x[:, gs] + dskx_ref[:, gs] * xs[:, gs]
        upd = lax.dot_general(bm, xd[:, gs], TN_DIMS, preferred_element_type=F32)
        s_ref[g] = st * e_last_x[:, gs] + upd

    @pl.when(c == pl.num_programs(1) - 1)
    def _():
        for g in range(G_SSM):
            st_ref[g * HPG:(g + 1) * HPG] = jnp.transpose(s_ref[g]).reshape(HPG, P_SSM, N_STATE)


def _pad_lanes(v):
    return jnp.pad(v.reshape(1, -1), ((0, 0), (0, LANE - v.shape[-1])))


def _ssd_prompt(pf, cw, cb, dtb, alog, dsk, b, s, ch):
    nc = s // ch
    tri = jnp.tril(jnp.ones((ch, ch), F32))
    lane_head = jnp.arange(LANE) % NH_SSM
    ex = ((lane_head[:, None] == jnp.arange(D_INNER)[None, :] // P_SSM)
          & (jnp.arange(LANE)[:, None] < 2 * NH_SSM)).astype(BF16)
    sw = ((lane_head[:, None] == jnp.arange(NH_SSM * ch)[None, :] // ch)
          & (jnp.arange(LANE)[:, None] < 3 * NH_SSM)).astype(BF16)
    dskx = jnp.repeat(dsk, P_SSM).reshape(1, D_INNER)
    tok = lambda col: (lambda bb, c: (bb * nc + c, col))
    return pl.pallas_call(
        functools.partial(_ssd_prompt_kernel, ch=ch),
        grid=(b, nc),
        in_specs=[pl.BlockSpec((ch, D_INNER), tok(O_XS // D_INNER)), pl.BlockSpec((ch, 2 * D_BC), tok(O_BC // (2 * D_BC))),
                  pl.BlockSpec((ch, LANE), tok(O_DT // LANE)),
                  _const_spec(cw.shape), _const_spec((1, CONV_DIM)), _const_spec((1, LANE)), _const_spec((1, LANE)),
                  _const_spec((1, D_INNER)), _const_spec((ch, ch)), _const_spec(ex.shape), _const_spec(sw.shape)],
        out_specs=[pl.BlockSpec((ch, D_INNER), lambda bb, c: (bb * nc + c, 0)),
                   pl.BlockSpec((None, NH_SSM, P_SSM, N_STATE), lambda bb, c: (bb, 0, 0, 0))],
        out_shape=[jax.ShapeDtypeStruct((b * s, D_INNER), F32),
                   jax.ShapeDtypeStruct((b, NH_SSM, P_SSM, N_STATE), F32)],
        scratch_shapes=[pltpu.VMEM((G_SSM, N_STATE, HPG * P_SSM), F32), pltpu.VMEM((8, D_INNER), F32),
                        pltpu.VMEM((8, 2 * D_BC), F32), pltpu.VMEM((NH_SSM, ch, ch), BF16)],
        compiler_params=_cparams("parallel", "arbitrary"),
        name="ssd_prompt",
    )(pf, pf, pf, cw, cb.reshape(1, CONV_DIM), _pad_lanes(dtb), _pad_lanes(alog), dskx, tri, ex, sw)


def _row_to_col(row, n):
    eye = lax.broadcasted_iota(I32, (n, n), 0) == lax.broadcasted_iota(I32, (n, n), 1)
    return jnp.sum(jnp.where(eye, jnp.broadcast_to(row, (n, n)), 0.0), axis=1, keepdims=True)


def _sample_scores_kernel(pt_ref, iq_ref, iw_ref, *refs, pg):
    page_refs, o_ref = refs[:pg], refs[pg]
    iq = iq_ref[...]
    iw = iw_ref[...] * ((D_IDX ** -0.5) * (H_IDX ** -0.5))
    for u in range(pg):
        s = jnp.dot(iq, page_refs[u][...].astype(BF16), preferred_element_type=F32)
        o_ref[u:u + 1, :] = jnp.sum(iw * jnp.maximum(s, 0.0), axis=0, keepdims=True)


def _sample_scores(iq, iw, cik_t, e, page_table, pg):
    bs, n_pages = page_table.shape
    page_spec = lambda u: pl.BlockSpec((None, None, D_IDX, PAGE_SIZE),
                                       lambda b, j, pt: (e, pt[b, j * pg + u], 0, 0))
    grid_spec = pltpu.PrefetchScalarGridSpec(
        num_scalar_prefetch=1, grid=(bs, n_pages // pg),
        in_specs=[pl.BlockSpec((None, H_IDX, D_IDX), lambda b, j, pt: (b, 0, 0)),
                  pl.BlockSpec((None, H_IDX, 1), lambda b, j, pt: (b, 0, 0))] + [page_spec(u) for u in range(pg)],
        out_specs=pl.BlockSpec((None, pg, PAGE_SIZE), lambda b, j, pt: (b, j, 0)))
    return pl.pallas_call(
        functools.partial(_sample_scores_kernel, pg=pg),
        grid_spec=grid_spec,
        out_shape=jax.ShapeDtypeStruct((bs, n_pages, PAGE_SIZE), F32),
        compiler_params=_cparams("parallel", "arbitrary"),
        name="sample_scores",
    )(page_table, iq, iw, *([cik_t] * pg))


def _sample_select_kernel(sc_ref, iq_ref, ikw_ref, sel_ref, seln_ref, *, k_sel, idx_bits):
    bs, n_keys = sc_ref.shape
    ikw = ikw_ref[...]
    w = ikw[:, E_IW:E_IW + H_IDX] * ((D_IDX ** -0.5) * (H_IDX ** -0.5))
    sc_new = jnp.zeros((bs, 1), F32)
    for h in range(H_IDX):
        s_h = jnp.sum(iq_ref[:, h * D_IDX:(h + 1) * D_IDX] * ikw[:, 0:D_IDX], axis=1, keepdims=True)
        sc_new = sc_new + w[:, h:h + 1] * jnp.maximum(s_h, 0.0)
    key_new = _score_key(sc_new)
    keys = _score_key(sc_ref[...])
    kidx = lax.broadcasted_iota(I32, keys.shape, 1)

    def total(mask, extra):
        m = jnp.where(mask, 1.0, 0.0)
        part = m[:, 0:LANE]
        for t in range(1, n_keys // LANE):
            part = part + m[:, t * LANE:(t + 1) * LANE]
        return jnp.sum(part, axis=1, keepdims=True) + jnp.where(extra, 1.0, 0.0)

    def count_ge(cand):
        return total(keys >= cand, key_new >= cand)

    thr = _kth_largest(count_ge, bs, float(k_sel))
    need = float(k_sel) - count_ge(thr + 1)

    def count_tie_lt(cand):
        return total(jnp.logical_and(keys == thr, kidx < cand), jnp.logical_and(key_new == thr, n_keys < cand))

    lim = _tie_limit(count_tie_lt, need, bs, idx_bits)
    sel = jnp.logical_or(keys > thr, jnp.logical_and(keys == thr, kidx <= lim))
    sel_ref[...] = jnp.where(sel, 1.0, 0.0)
    sel_new = jnp.logical_or(key_new > thr, jnp.logical_and(key_new == thr, n_keys <= lim))
    seln_ref[...] = jnp.broadcast_to(jnp.where(sel_new, 1.0, 0.0), seln_ref.shape)


def _sample_select(scores, iqf, ikw):
    bs, n_keys = scores.shape
    k_sel = min(TOPK_MAX, (n_keys + 1) // 4)
    idx_bits = int(math.log2(n_keys)) + 1
    full = lambda a: pl.BlockSpec(a.shape, lambda: (0,) * a.ndim)
    return pl.pallas_call(
        functools.partial(_sample_select_kernel, k_sel=k_sel, idx_bits=idx_bits),
        in_specs=[full(scores), full(iqf), full(ikw)],
        out_specs=[pl.BlockSpec((bs, n_keys), lambda: (0, 0)), pl.BlockSpec((bs, LANE), lambda: (0, 0))],
        out_shape=[jax.ShapeDtypeStruct((bs, n_keys), F32), jax.ShapeDtypeStruct((bs, LANE), F32)],
        compiler_params=pltpu.CompilerParams(vmem_limit_bytes=VMEM_LIMIT),
        name="sample_select",
    )(scores, iqf, ikw)


def _sample_attn_kernel(pt_ref, sel_ref, seln_ref, q_ref, kn_ref, vn_ref, tbs_ref, b31_ref, b0_ref,
                        *refs, pg, n_pages):
    k_refs, v_refs = refs[:pg], refs[pg:2 * pg]
    o_ref, m_ref, l_ref, acc_ref = refs[2 * pg:]
    j = pl.program_id(1)
    qcol = q_ref[...] * (DH_A ** -0.5)

    def head_sum(x):
        return jnp.concatenate([jnp.sum(x[h * DH_A:(h + 1) * DH_A], axis=0, keepdims=True) for h in range(H_A)],
                               axis=0)

    def head_expand(x, w):
        return jnp.concatenate([jnp.broadcast_to(x[h:h + 1, :], (DH_A, w)) for h in range(H_A)], axis=0)

    @pl.when(j == 0)
    def _():
        sel_new = seln_ref[0:1, 0:1] > 0.5
        lg_new = head_sum(qcol * kn_ref[...]) + b0_ref[...]
        m_ref[...] = jnp.broadcast_to(jnp.where(sel_new, lg_new, NEG_BIG), m_ref.shape)
        l0 = jnp.broadcast_to(jnp.where(sel_new, 1.0, 0.0), (H_A, 1))
        l_ref[...] = jnp.broadcast_to(l0, l_ref.shape)
        first_lane = lax.broadcasted_iota(I32, (D_ATT, PAGE_SIZE), 1) == 0
        acc_ref[...] = jnp.where(first_lane, head_expand(l0, PAGE_SIZE) * vn_ref[...], 0.0)

    qb = jnp.broadcast_to(qcol, (D_ATT, PAGE_SIZE))
    logits = []
    for u in range(pg):
        p_idx = j * pg + u
        s = head_sum(k_refs[u][...] * qb)
        s = s + jnp.where(p_idx == n_pages - 1, tbs_ref[...], b31_ref[...])
        logits.append(jnp.where(sel_ref[pl.ds(p_idx, 1), :] > 0.5, s, NEG_BIG))
    m_old = m_ref[:, 0:1]
    m_new = m_old
    for s in logits:
        m_new = jnp.maximum(m_new, jnp.max(s, axis=1, keepdims=True))
    alpha = jnp.exp(m_old - m_new)
    l_new = alpha * l_ref[:, 0:1]
    acc = head_expand(alpha, PAGE_SIZE) * acc_ref[...]
    for u, s in enumerate(logits):
        p = jnp.exp(s - m_new)
        l_new = l_new + jnp.sum(p, axis=1, keepdims=True)
        acc = acc + v_refs[u][...] * head_expand(p, PAGE_SIZE)
    acc_ref[...] = acc
    m_ref[...] = jnp.broadcast_to(m_new, m_ref.shape)
    l_ref[...] = jnp.broadcast_to(l_new, l_ref.shape)

    @pl.when(j == pl.num_programs(1) - 1)
    def _():
        o_ref[...] = jnp.sum(acc_ref[...], axis=1, keepdims=True) / head_expand(l_ref[:, 0:1], 1)


def _sample_attn(sel, sel_new, q, k_new, v_new, ck_t, cv_t, e, page_table, tbs, b31c, b0c, pg):
    bs, n_pages = page_table.shape
    per_b = lambda shape: pl.BlockSpec((None,) + shape, lambda b, j, pt: (b,) + (0,) * len(shape))
    page_spec = lambda u: pl.BlockSpec((None, None, D_ATT, PAGE_SIZE),
                                       lambda b, j, pt: (e, pt[b, j * pg + u], 0, 0))
    const = lambda shape: pl.BlockSpec(shape, lambda b, j, pt: (0,) * len(shape))
    grid_spec = pltpu.PrefetchScalarGridSpec(
        num_scalar_prefetch=1, grid=(bs, n_pages // pg),
        in_specs=[per_b((n_pages, PAGE_SIZE)), per_b((1, LANE)), per_b((D_ATT, 1)), per_b((D_ATT, 1)),
                  per_b((D_ATT, 1)), const(tbs.shape), const(b31c.shape), const(b0c.shape)]
        + [page_spec(u) for u in range(pg)] + [page_spec(u) for u in range(pg)],
        out_specs=per_b((D_ATT, 1)),
        scratch_shapes=[pltpu.VMEM((H_A, LANE), F32), pltpu.VMEM((H_A, LANE), F32),
                        pltpu.VMEM((D_ATT, PAGE_SIZE), F32)])
    return pl.pallas_call(
        functools.partial(_sample_attn_kernel, pg=pg, n_pages=n_pages),
        grid_spec=grid_spec,
        out_shape=jax.ShapeDtypeStruct((bs, D_ATT, 1), F32),
        compiler_params=_cparams("parallel", "arbitrary"),
        name="sample_attn",
    )(page_table, sel, sel_new, q, k_new, v_new, tbs, b31c, b0c, *([ck_t] * pg), *([cv_t] * pg))


def _sample_ret_kernel(rq_ref, rk_ref, rv_ref, cos_ref, sin_ref, g_ref, s_ref, o_ref, sn_ref):
    cos_t, sin_t = cos_ref[...], sin_ref[...]
    rq = _rotate_half(rq_ref[...], cos_t, sin_t)
    rk = _rotate_half(rk_ref[...], cos_t, sin_t) * (DK_R ** -0.5)
    rv = rv_ref[...]
    for h in range(H_R):
        qc = _row_to_col(rq[:, h * DK_R:(h + 1) * DK_R], DK_R)
        kc = _row_to_col(rk[:, h * DK_R:(h + 1) * DK_R], DK_R)
        vh = rv[:, h * DV_R:(h + 1) * DV_R]
        s_new = s_ref[h] * g_ref[0:1, h:h + 1] + kc * vh
        sn_ref[h] = s_new
        o_ref[:, h * DV_R:(h + 1) * DV_R] = jnp.sum(qc * s_new, axis=0, keepdims=True)


def _layer_b_spec(layer, shape):
    return pl.BlockSpec((None, None) + shape, lambda b: (layer, b) + (0,) * len(shape))


def _sample_ret(rq, rk, rv, cos_t, sin_t, state, e):
    bs = rq.shape[0]
    gamma = _pad_lanes(1.0 - jnp.exp2(-5.0 - jnp.arange(H_R, dtype=F32)))
    per_b = lambda shape: pl.BlockSpec((None,) + shape, lambda b: (b,) + (0,) * len(shape))
    return pl.pallas_call(
        _sample_ret_kernel,
        grid=(bs,),
        in_specs=[per_b((1, D_RK)), per_b((1, D_RK)), per_b((1, D_RV)), _const_spec((1, D_RK)), _const_spec((1, D_RK)),
                  _const_spec((1, LANE)), _layer_b_spec(e, (H_R, DK_R, DV_R))],
        out_specs=[per_b((1, D_RV)), per_b((H_R, DK_R, DV_R))],
        out_shape=[jax.ShapeDtypeStruct((bs, 1, D_RV), F32), jax.ShapeDtypeStruct(state.shape[1:], F32)],
        compiler_params=_cparams("parallel"),
        name="sample_retention",
    )(rq, rk, rv, cos_t, sin_t, gamma, state)


def _sample_conv_kernel(xbc_ref, dt_ref, hist_ref, cw_ref, cb_ref, dtb_ref, alog_ref, act_ref, hn_ref, dtd_ref):
    x_new = xbc_ref[...]
    conv = cb_ref[...] + x_new * cw_ref[CONV_W - 1:CONV_W, :]
    for j in range(CONV_W - 1):
        conv = conv + hist_ref[j] * cw_ref[j:j + 1, :]
    act_ref[...] = _silu(conv)
    for j in range(CONV_W - 2):
        hn_ref[j] = hist_ref[j + 1]
    hn_ref[CONV_W - 2] = x_new
    dt = _softplus(dt_ref[...] + dtb_ref[...])
    dtd_ref[0] = dt
    dtd_ref[1] = jnp.exp(dt * (-jnp.exp(alog_ref[...])))


def _sample_state_kernel(xt_ref, bc_ref, dtd_ref, dsk_ref, s_ref, yt_ref, sn_ref):
    xs_t = xt_ref[...]
    dt = dtd_ref[0:1, 0:NH_SSM]
    d_a = dtd_ref[1:2, :]
    xdt_t = xs_t * dt
    lane = lax.broadcasted_iota(I32, xs_t.shape, 1)
    y_t = dsk_ref[:, 0:NH_SSM] * xs_t
    for g in range(G_SSM):
        bm = bc_ref[:, g * N_STATE:(g + 1) * N_STATE]
        for hh in range(HPG):
            h = g * HPG + hh
            sn_ref[h] = s_ref[h] * d_a[0:1, h:h + 1] + xdt_t[:, h:h + 1] * bm
    cols = []
    for g in range(G_SSM):
        cm = bc_ref[:, D_BC + g * N_STATE:D_BC + (g + 1) * N_STATE]
        for hh in range(HPG):
            cols.append(jnp.sum(sn_ref[g * HPG + hh] * cm, axis=1, keepdims=True))
    for h in range(NH_SSM):
        y_t = y_t + jnp.where(lane == h, cols[h], 0.0)
    yt_ref[...] = y_t


def _sample_ssd(xbc, dt, hist_t, cw, cb, dtb, alog, dsk, state, layer):
    bs = xbc.shape[0]
    full = lambda shape: pl.BlockSpec(shape, lambda i: (0,) * len(shape))
    act, hn_t, dtd = pl.pallas_call(
        _sample_conv_kernel,
        grid=(1,),
        in_specs=[full((bs, CONV_DIM)), full((bs, LANE)),
                  pl.BlockSpec((None, CONV_W - 1, bs, CONV_DIM), lambda i: (layer, 0, 0, 0)),
                  full(cw.shape), full((1, CONV_DIM)), full((1, LANE)), full((1, LANE))],
        out_specs=[full((bs, CONV_DIM)), full((CONV_W - 1, bs, CONV_DIM)), full((2, bs, LANE))],
        out_shape=[jax.ShapeDtypeStruct((bs, CONV_DIM), F32), jax.ShapeDtypeStruct((CONV_W - 1, bs, CONV_DIM), F32),
                   jax.ShapeDtypeStruct((2, bs, LANE), F32)],
        compiler_params=pltpu.CompilerParams(vmem_limit_bytes=VMEM_LIMIT),
        name="sample_conv",
    )(xbc, dt, hist_t, cw, cb.reshape(1, CONV_DIM), _pad_lanes(dtb), _pad_lanes(alog))
    xs_t = jnp.swapaxes(act[:, :D_INNER].reshape(bs, NH_SSM, P_SSM), 1, 2)
    bc = act[:, D_INNER:].reshape(bs, 1, 2 * D_BC)
    dtd_b = jnp.swapaxes(dtd, 0, 1)
    per_b = lambda shape: pl.BlockSpec((None,) + shape, lambda b: (b,) + (0,) * len(shape))
    y_t, st = pl.pallas_call(
        _sample_state_kernel,
        grid=(bs,),
        in_specs=[per_b((P_SSM, NH_SSM)), per_b((1, 2 * D_BC)), per_b((2, LANE)), _const_spec((1, LANE)),
                  _layer_b_spec(layer, (NH_SSM, P_SSM, N_STATE))],
        out_specs=[per_b((P_SSM, NH_SSM)), per_b((NH_SSM, P_SSM, N_STATE))],
        out_shape=[jax.ShapeDtypeStruct((bs, P_SSM, NH_SSM), F32), jax.ShapeDtypeStruct(state.shape[1:], F32)],
        compiler_params=_cparams("parallel"),
        name="sample_state",
    )(xs_t, bc, dtd_b, _pad_lanes(dsk), state)
    return jnp.swapaxes(y_t, 1, 2).reshape(bs, D_INNER), hn_t, st


def _even_weight(w):
    q, k, v, iq, ik, iw, rq, rk, rv, rg = jnp.split(w, np.cumsum(EVEN_SIZES)[:-1].tolist(), axis=1)
    pad = jnp.zeros((w.shape[0], EVEN_PAD - sum(EVEN_SIZES)), w.dtype)
    return jnp.concatenate([q, k, v, iq, rv, rg, rq, rk, ik, iw, pad], axis=1).astype(BF16)


def _odd_weight(w):
    return jnp.pad(w, ((0, 0), (0, ODD_PAD - ODD_IN))).astype(BF16)


def _t5_bucket(dist):
    max_exact = NUM_BUCKETS // 2
    d = jnp.maximum(dist, 0)
    ratio = jnp.log(jnp.maximum(d, max_exact).astype(F32) / max_exact) / math.log(MAX_DISTANCE / max_exact)
    large = max_exact + (ratio * (NUM_BUCKETS - max_exact)).astype(I32)
    return jnp.where(d < max_exact, d, jnp.minimum(large, NUM_BUCKETS - 1))


def _bias_tables(rel_bias, tq):
    assert tq >= MAX_DISTANCE
    f = rel_bias[_t5_bucket(jnp.arange(2 * tq))]
    w = jnp.concatenate([f[tq:], jnp.zeros((tq, H_A), f.dtype), f[:tq]], axis=0)
    n, period = 2 * tq, 3 * tq
    tb = jnp.tile(w, (n, 1))[:n * (period - 1)].reshape(n, period - 1, H_A)[:, :tq, :]
    tb = jnp.moveaxis(tb, -1, 0)
    b31 = rel_bias[NUM_BUCKETS - 1].reshape(1, H_A)
    tbs = rel_bias[_t5_bucket(PAGE_SIZE - jnp.arange(PAGE_SIZE))].T
    return tb, jnp.pad(b31, ((0, 0), (0, LANE - H_A))), tbs


def kernel(x_prompt, x_sample, cache_k, cache_v, cache_idx_k, state_ret, state_conv, state_ssm, page_table, rel_bias, norm_mix, norm_mlp, norm_final, w_in_even, w_out_even, w_in_odd, conv_w, conv_b, dt_bias, a_log, d_skip, ssm_norm, w_out_odd, w_up, w_down):
    b, s, d = x_prompt.shape
    bs = x_sample.shape[0]
    n_pages = page_table.shape[1]
    past = n_pages * PAGE_SIZE
    tm = min(512, b * s)
    tq = 128
    tk = min(512, s)
    ch = 128
    ch_ret = min(256, s)
    pg = min(32, n_pages)

    xp = x_prompt.reshape(b * s, d)
    xs = x_sample.reshape(bs, d)
    tb, b31, tbs = _bias_tables(rel_bias, tq)
    b31c = rel_bias[NUM_BUCKETS - 1].reshape(H_A, 1)
    b0c = rel_bias[0].reshape(H_A, 1)
    cos_p, sin_p = _rope_tables(jnp.arange(s))
    cos_s, sin_s = _rope_tables(jnp.full((1,), past))
    bf_slices = ((0, 4 * D_ATT), (E_IK, LANE))
    n_layers, n_pool = cache_k.shape[0], cache_k.shape[1]
    ck_t = jnp.transpose(cache_k, (0, 1, 3, 4, 2)).reshape(n_layers, n_pool, D_ATT, PAGE_SIZE)
    cv_t = jnp.transpose(cache_v, (0, 1, 3, 4, 2)).reshape(n_layers, n_pool, D_ATT, PAGE_SIZE)
    cik_t = jnp.transpose(cache_idx_k, (0, 1, 3, 2))
    hist_t = jnp.transpose(state_conv, (0, 2, 1, 3))

    k_p, v_p, ik_p, k_s, v_s, ik_s, ret_p, ret_s = [], [], [], [], [], [], [], []
    conv_p, conv_s, ssm_p, ssm_s = [], [], [], []
    for layer in range(DEPTH):
        final = layer == DEPTH - 1
        wup, wdn = w_up[layer].astype(BF16), w_down[layer].astype(BF16)
        if layer % 2 == 0:
            e = layer // 2
            w_in = _even_weight(w_in_even[e])
            wo = w_out_even[e].astype(BF16)
            pf, pbq, pbi, k_t, v_t, ik_t, vtb = _even_proj_prompt(xp, norm_mix[layer], w_in, b, s, tk)
            sf, sbq, _ = _rms_matmul(xs, norm_mix[layer], w_in, bs, bf_slices)
            k_p.append(k_t)
            v_p.append(v_t)
            ik_p.append(ik_t)
            k_s.append(sf[:, E_K:E_K + D_ATT].reshape(bs, 1, H_A, DH_A))
            v_s.append(sf[:, E_V:E_V + D_ATT].reshape(bs, 1, H_A, DH_A))
            ik_s.append(sf[:, E_IK:E_IK + D_IDX].reshape(bs, 1, D_IDX))
            att = _attn_prompt(pf, pbq, pbi, vtb, tb, b31, b, s, tq, tk)
            o, st = _ret_prompt(pf, cos_p, sin_p, b, s, ch_ret)
            ret_p.append(st)
            xp = _mlp(_even_merge(xp, att, o, pf, wo, tm), norm_mlp[layer], wup, wdn, norm_final, tm, final)
            iq_bf = sbq[:, E_IQ:E_IQ + D_ATT].reshape(bs, H_IDX, D_IDX)
            iwf = sf[:, E_IK + E_IW:E_IK + E_IW + H_IDX].reshape(bs, H_IDX, 1)
            scores = _sample_scores(iq_bf, iwf, cik_t, e, page_table, pg)
            sel, sel_new = _sample_select(scores.reshape(bs, past), sf[:, E_IQ:E_IQ + D_ATT],
                                          sf[:, E_IK:E_IK + LANE])
            att_s = _sample_attn(sel.reshape(bs, n_pages, PAGE_SIZE), sel_new.reshape(bs, 1, LANE),
                                 sf[:, E_Q:E_Q + D_ATT].reshape(bs, D_ATT, 1),
                                 sf[:, E_K:E_K + D_ATT].reshape(bs, D_ATT, 1),
                                 sf[:, E_V:E_V + D_ATT].reshape(bs, D_ATT, 1),
                                 ck_t, cv_t, e, page_table, tbs, b31c, b0c, pg)
            o_s, st_s = _sample_ret(sf[:, E_RQ:E_RQ + D_RK].reshape(bs, 1, D_RK), sf[:, E_RK:E_RK + D_RK].reshape(bs, 1, D_RK),
                                    sf[:, E_RV:E_RV + D_RV].reshape(bs, 1, D_RV), cos_s, sin_s, state_ret, e)
            ret_s.append(st_s)
            xs = _mlp(_even_merge(xs, att_s.reshape(bs, D_ATT), o_s.reshape(bs, D_RV), sf, wo, bs),
                      norm_mlp[layer], wup, wdn, norm_final, bs, final)
        else:
            o_ = layer // 2
            w_in = _odd_weight(w_in_odd[o_])
            wo = w_out_odd[o_].astype(BF16)
            (pf,) = _rms_matmul(xp, norm_mix[layer], w_in, tm // 2)
            (sf,) = _rms_matmul(xs, norm_mix[layer], w_in, bs)
            y, st = _ssd_prompt(pf, conv_w[o_], conv_b[o_], dt_bias[o_], a_log[o_], d_skip[o_], b, s, ch)
            conv_p.append(pf.reshape(b, s, ODD_PAD)[:, s - (CONV_W - 1):, O_XS:O_XS + CONV_DIM])
            ssm_p.append(st)
            xp = _mlp(_odd_merge(xp, y, pf, ssm_norm[o_], wo, tm), norm_mlp[layer], wup, wdn, norm_final, tm, final)
            y_s, hn_t, st_s = _sample_ssd(sf[:, O_XS:O_XS + CONV_DIM], sf[:, O_DT:O_DT + LANE], hist_t,
                                          conv_w[o_], conv_b[o_], dt_bias[o_], a_log[o_], d_skip[o_], state_ssm, o_)
            conv_s.append(jnp.swapaxes(hn_t, 0, 1))
            ssm_s.append(st_s)
            xs = _mlp(_odd_merge(xs, y_s, sf, ssm_norm[o_], wo, bs),
                      norm_mlp[layer], wup, wdn, norm_final, bs, final)

    n_even = len(k_p)
    k_p = jnp.transpose(jnp.stack(k_p).reshape(n_even, b, H_A, DH_A, s), (0, 1, 4, 2, 3))
    v_p = jnp.transpose(jnp.stack(v_p).reshape(n_even, b, H_A, DH_A, s), (0, 1, 4, 2, 3))
    ik_p = jnp.transpose(jnp.stack(ik_p), (0, 1, 3, 2))
    return (xp.reshape(b, s, d), xs.reshape(bs, 1, d), k_p, v_p, ik_p,
            jnp.stack(k_s), jnp.stack(v_s), jnp.stack(ik_s), jnp.stack(ret_p), jnp.stack(ret_s),
            jnp.stack(conv_p), jnp.stack(conv_s), jnp.stack(ssm_p), jnp.stack(ssm_s))
```
